```python
import jax, jax.numpy as jnp
from jax import lax
import numpy as np

D_MODEL = 4096
BATCH = 1
SEQ = 8192
DEPTH = 1

D_CONV = D_MODEL // 2
D_GMLP = D_MODEL // 2
D_MIX = D_CONV + D_GMLP
CONV_GROUPS = 16
CONV_WIDTH = 31
GMLP_HEADS = 16
GMLP_HEAD_DIM = D_GMLP // GMLP_HEADS
CHUNK = 128
N_GROUPS = 8
EXPERTS_PER_GROUP = 8
N_EXPERTS = N_GROUPS * EXPERTS_PER_GROUP
TOP_K_INNER = 2
D_EXPERT = D_MODEL // 8
ROW_BLOCK = 128
LN_EPS = 1e-5
ALPHA = (2.0 * DEPTH) ** 0.25
BETA = (8.0 * DEPTH) ** -0.25

kernel_name = "hybrid_conformer_conv_chunked_gmlp_hier_moe_deepnorm"


def _layernorm(x, g, b):
    xf = x.astype(jnp.float32)
    mu = jnp.mean(xf, axis=-1, keepdims=True)
    var = jnp.mean(jnp.square(xf - mu), axis=-1, keepdims=True)
    y = (xf - mu) * lax.rsqrt(var + LN_EPS) * g.astype(jnp.float32) + b.astype(jnp.float32)
    return y.astype(x.dtype)


def _group_layernorm(x, g, b, groups):
    shp = x.shape
    xf = x.astype(jnp.float32).reshape(shp[:-1] + (groups, shp[-1] // groups))
    mu = jnp.mean(xf, axis=-1, keepdims=True)
    var = jnp.mean(jnp.square(xf - mu), axis=-1, keepdims=True)
    xn = ((xf - mu) * lax.rsqrt(var + LN_EPS)).reshape(shp)
    return (xn * g.astype(jnp.float32) + b.astype(jnp.float32)).astype(x.dtype)


def _causal_depthwise_conv(a, w, b):
    out = lax.conv_general_dilated(
        a, w[:, None, :].astype(a.dtype), window_strides=(1,),
        padding=[(CONV_WIDTH - 1, 0)], dimension_numbers=("NWC", "WIO", "NWC"),
        feature_group_count=a.shape[-1])
    return out + b.astype(a.dtype)


def _chunked_spatial_gate(v, w_spatial, b_spatial):
    bsz, t, c = v.shape
    vc = v.reshape(bsz, t // CHUNK, CHUNK, GMLP_HEADS, GMLP_HEAD_DIM)
    causal = jnp.tril(jnp.ones((CHUNK, CHUNK), dtype=bool))
    w = jnp.where(causal[None], w_spatial, 0.0).astype(v.dtype)
    s = jnp.einsum("hij,bnjhd->bnihd", w, vc)
    s = s + b_spatial.T.astype(v.dtype)[None, None, :, :, None]
    return s.reshape(bsz, t, c)


def _token_mixer(x, w_in, b_in, conv_w, conv_b, conv_norm_g, conv_norm_b,
                 gmlp_norm_g, gmlp_norm_b, w_spatial, b_spatial, w_out, b_out):
    h = jnp.einsum("btd,de->bte", x, w_in) + b_in
    a, a_gate, u, v = jnp.split(h, [D_CONV, 2 * D_CONV, 2 * D_CONV + D_GMLP], axis=-1)
    a = a * jax.nn.sigmoid(a_gate)
    a = _causal_depthwise_conv(a, conv_w, conv_b)
    a = jax.nn.silu(_group_layernorm(a, conv_norm_g, conv_norm_b, CONV_GROUPS))
    u = jax.nn.gelu(u)
    v = _group_layernorm(jax.nn.gelu(v), gmlp_norm_g, gmlp_norm_b, GMLP_HEADS)
    g = u * _chunked_spatial_gate(v, w_spatial, b_spatial)
    y = jnp.concatenate([a, g], axis=-1)
    return jnp.einsum("bte,ed->btd", y, w_out) + b_out


def _hierarchical_moe(x, w_router_group, b_router_group, w_router_expert, b_router_expert,
                      w_expert_gate, w_expert_up, w_expert_down):
    bsz, t, d = x.shape
    n_tok = bsz * t
    xf = x.reshape(n_tok, d)
    x32 = xf.astype(jnp.float32)
    group_logits = x32 @ w_router_group.astype(jnp.float32) + b_router_group.astype(jnp.float32)
    group_prob = jax.nn.softmax(group_logits, axis=-1)
    g_sel = jnp.argmax(group_logits, axis=-1).astype(jnp.int32)
    g_weight = jnp.take_along_axis(group_prob, g_sel[:, None], axis=1)[:, 0]
    inner_all = jnp.einsum("nd,gde->nge", x32, w_router_expert.astype(jnp.float32)) \
        + b_router_expert.astype(jnp.float32)
    inner = jnp.take_along_axis(inner_all, g_sel[:, None, None], axis=1)[:, 0]
    top_logits, top_idx = lax.top_k(inner, TOP_K_INNER)
    top_w = jax.nn.softmax(top_logits, axis=-1)
    gate = (g_weight[:, None] * top_w).reshape(-1)
    expert_id = (g_sel[:, None] * EXPERTS_PER_GROUP + top_idx).reshape(-1).astype(jnp.int32)
    tok = jnp.repeat(jnp.arange(n_tok, dtype=jnp.int32), TOP_K_INNER)
    m = n_tok * TOP_K_INNER
    order = jnp.argsort(expert_id)
    e_s, t_s, g_s = expert_id[order], tok[order], gate[order]
    counts = jnp.bincount(expert_id, length=N_EXPERTS).astype(jnp.int32)
    offs = jnp.cumsum(counts) - counts
    pcounts = ((counts + ROW_BLOCK - 1) // ROW_BLOCK) * ROW_BLOCK
    pend = jnp.cumsum(pcounts)
    poffs = pend - pcounts
    pos = poffs[e_s] + (jnp.arange(m, dtype=jnp.int32) - offs[e_s])
    n_blocks = -(-m // ROW_BLOCK) + N_EXPERTS
    buf_tok = jnp.zeros((n_blocks * ROW_BLOCK,), jnp.int32).at[pos].set(t_s)
    buf_gate = jnp.zeros((n_blocks * ROW_BLOCK,), jnp.float32).at[pos].set(g_s)
    starts = jnp.arange(n_blocks, dtype=jnp.int32) * ROW_BLOCK
    block_expert = jnp.minimum(jnp.searchsorted(pend, starts, side="right"), N_EXPERTS - 1).astype(jnp.int32)

    def expert_block(args):
        tok_b, gate_b, e = args
        xb = xf[tok_b]
        hb = jax.nn.silu(xb @ w_expert_gate[e]) * (xb @ w_expert_up[e])
        return (hb @ w_expert_down[e]) * gate_b[:, None].astype(x.dtype)

    ys = lax.map(expert_block, (buf_tok.reshape(n_blocks, ROW_BLOCK),
                                buf_gate.reshape(n_blocks, ROW_BLOCK), block_expert))
    out = jnp.zeros_like(xf).at[buf_tok].add(ys.reshape(-1, d))
    return out.reshape(bsz, t, d)


def setup_inputs(seed: int = 0) -> dict:
    key = jax.random.key(seed)
    ks = jax.random.split(key, 24)
    L = DEPTH

    def nrm(k, shape, scale):
        return jax.random.normal(k, shape, jnp.float32) * scale

    return {
        "x": nrm(ks[0], (BATCH, SEQ, D_MODEL), 1.0),
        "w_in": nrm(ks[1], (L, D_MODEL, 2 * D_MIX), D_MODEL ** -0.5),
        "b_in": nrm(ks[2], (L, 2 * D_MIX), 0.02),
        "conv_w": nrm(ks[3], (L, CONV_WIDTH, D_CONV), CONV_WIDTH ** -0.5),
        "conv_b": nrm(ks[4], (L, D_CONV), 0.02),
        "conv_norm_g": 1.0 + nrm(ks[5], (L, D_CONV), 0.02),
        "conv_norm_b": nrm(ks[6], (L, D_CONV), 0.02),
        "gmlp_norm_g": 1.0 + nrm(ks[7], (L, D_GMLP), 0.02),
        "gmlp_norm_b": nrm(ks[8], (L, D_GMLP), 0.02),
        "w_spatial": nrm(ks[9], (L, GMLP_HEADS, CHUNK, CHUNK), 0.5 * CHUNK ** -0.5),
        "b_spatial": 1.0 + nrm(ks[10], (L, GMLP_HEADS, CHUNK), 0.1),
        "w_out": nrm(ks[11], (L, D_MIX, D_MODEL), BETA * D_MIX ** -0.5),
        "b_out": nrm(ks[12], (L, D_MODEL), 0.02),
        "ln1_g": 1.0 + nrm(ks[13], (L, D_MODEL), 0.02),
        "ln1_b": nrm(ks[14], (L, D_MODEL), 0.02),
        "w_router_group": nrm(ks[15], (L, D_MODEL, N_GROUPS), D_MODEL ** -0.5),
        "b_router_group": nrm(ks[16], (L, N_GROUPS), 0.01),
        "w_router_expert": nrm(ks[17], (L, N_GROUPS, D_MODEL, EXPERTS_PER_GROUP), D_MODEL ** -0.5),
        "b_router_expert": nrm(ks[18], (L, N_GROUPS, EXPERTS_PER_GROUP), 0.01),
        "w_expert_gate": nrm(ks[19], (L, N_EXPERTS, D_MODEL, D_EXPERT), D_MODEL ** -0.5),
        "w_expert_up": nrm(ks[20], (L, N_EXPERTS, D_MODEL, D_EXPERT), D_MODEL ** -0.5),
        "w_expert_down": nrm(ks[21], (L, N_EXPERTS, D_EXPERT, D_MODEL), BETA * D_EXPERT ** -0.5),
        "ln2_g": 1.0 + nrm(ks[22], (L, D_MODEL), 0.02),
        "ln2_b": nrm(ks[23], (L, D_MODEL), 0.02),
    }


def reference(x, w_in, b_in, conv_w, conv_b, conv_norm_g, conv_norm_b, gmlp_norm_g, gmlp_norm_b,
              w_spatial, b_spatial, w_out, b_out, ln1_g, ln1_b, w_router_group, b_router_group,
              w_router_expert, b_router_expert, w_expert_gate, w_expert_up, w_expert_down,
              ln2_g, ln2_b):
    for l in range(DEPTH):
        mix = _token_mixer(x, w_in[l], b_in[l], conv_w[l], conv_b[l], conv_norm_g[l], conv_norm_b[l],
                           gmlp_norm_g[l], gmlp_norm_b[l], w_spatial[l], b_spatial[l], w_out[l], b_out[l])
        x = _layernorm(ALPHA * x + mix, ln1_g[l], ln1_b[l])
        ffn = _hierarchical_moe(x, w_router_group[l], b_router_group[l], w_router_expert[l],
                                b_router_expert[l], w_expert_gate[l], w_expert_up[l], w_expert_down[l])
        x = _layernorm(ALPHA * x + ffn, ln2_g[l], ln2_b[l])
    return x
```

```python
import functools

import jax
import jax.numpy as jnp
from jax import lax
from jax.experimental import pallas as pl
from jax.experimental.pallas import tpu as pltpu

D_MODEL = 4096
SEQ = 8192
D_CONV = D_MODEL // 2
D_GMLP = D_MODEL // 2
CONV_WIDTH = 31
GROUP_DIM = 128
GMLP_HEADS = 16
CHUNK = 128
N_GROUPS = 8
EXPERTS_PER_GROUP = 8
N_EXPERTS = N_GROUPS * EXPERTS_PER_GROUP
TOP_K = 2
D_EXPERT = D_MODEL // 8
ROW_BLOCK = 128
LN_EPS = 1e-5
ALPHA = 2.0 ** 0.25

N_ROWS = SEQ * TOP_K
N_BLOCKS = N_ROWS // ROW_BLOCK + N_EXPERTS
ROUTER_COLS = 128

VMEM_LIMIT_BYTES = 60 * 1024 * 1024

MIX_BM = 512
MIX_BN = 256
HALO = 32
CONV_ROWS = 64

OUT_BM = 256
COMB_BM = 128


def _group_norm(v, g, b):
    outs = []
    for s in range(v.shape[1] // GROUP_DIM):
        blk = v[:, s * GROUP_DIM:(s + 1) * GROUP_DIM]
        mu = jnp.mean(blk, axis=-1, keepdims=True)
        d = blk - mu
        var = jnp.mean(d * d, axis=-1, keepdims=True)
        outs.append(d * lax.rsqrt(var + LN_EPS))
    return jnp.concatenate(outs, axis=-1) * g + b


def _mixer_kernel(x_ref, wa_ref, wg_ref, wu_ref, wv_ref, ba_ref, bg_ref, bu_ref, bv_ref,
                  cw_ref, cb_ref, cng_ref, cnb_ref, gng_ref, gnb_ref, wsp_ref, bsp_ref,
                  ya_ref, yg_ref, xs_ref):
    i = pl.program_id(1)
    x = x_ref[...]
    f32 = jnp.float32

    a = jnp.dot(x, wa_ref[...], preferred_element_type=f32) + ba_ref[...]
    gate = jnp.dot(x, wg_ref[...], preferred_element_type=f32) + bg_ref[...]
    a = a * jax.nn.sigmoid(gate)

    @pl.when(i == 0)
    def _():
        xs_ref[0:HALO, :] = jnp.zeros((HALO, MIX_BN), f32)

    xs_ref[HALO:HALO + MIX_BM, :] = a
    off = HALO - (CONV_WIDTH - 1)
    conv_chunks = []
    for r in range(MIX_BM // CONV_ROWS):
        acc = jnp.zeros((CONV_ROWS, MIX_BN), f32) + cb_ref[...]
        for k in range(CONV_WIDTH):
            start = r * CONV_ROWS + off + k
            acc = acc + cw_ref[k:k + 1, :] * xs_ref[start:start + CONV_ROWS, :]
        conv_chunks.append(acc)
    conv = jnp.concatenate(conv_chunks, axis=0)
    xs_ref[0:HALO, :] = a[MIX_BM - HALO:, :]
    ya_ref[...] = jax.nn.silu(_group_norm(conv, cng_ref[...], cnb_ref[...])).astype(ya_ref.dtype)

    u = jax.nn.gelu(jnp.dot(x, wu_ref[...], preferred_element_type=f32) + bu_ref[...])
    v = jax.nn.gelu(jnp.dot(x, wv_ref[...], preferred_element_type=f32) + bv_ref[...])
    v = _group_norm(v, gng_ref[...], gnb_ref[...]).astype(jnp.bfloat16)
    row = lax.broadcasted_iota(jnp.int32, (CHUNK, CHUNK), 0)
    col = lax.broadcasted_iota(jnp.int32, (CHUNK, CHUNK), 1)
    n_chunks = MIX_BM // CHUNK
    gated = []
    for h in range(MIX_BN // GROUP_DIM):
        w_h = jnp.where(row >= col, wsp_ref[h], 0.0).astype(jnp.bfloat16)
        v_h = jnp.concatenate(
            [v[c * CHUNK:(c + 1) * CHUNK, h * GROUP_DIM:(h + 1) * GROUP_DIM] for c in range(n_chunks)],
            axis=1)
        s_h = jnp.dot(w_h, v_h, preferred_element_type=f32) + bsp_ref[h]
        gated.append(jnp.concatenate(
            [s_h[:, c * GROUP_DIM:(c + 1) * GROUP_DIM] for c in range(n_chunks)], axis=0))
    yg_ref[...] = (u * jnp.concatenate(gated, axis=1)).astype(yg_ref.dtype)


def _mixer_call(xb, w_in_b, b_in, conv_w, conv_b, cng, cnb, gng, gnb, w_spatial, b_spatial_col):
    nq = D_CONV // MIX_BN
    grid = (nq, SEQ // MIX_BM)

    def wspec(q):
        return pl.BlockSpec((D_MODEL, MIX_BN), lambda j, i, q=q: (0, q * nq + j))

    def bspec(q):
        return pl.BlockSpec((1, MIX_BN), lambda j, i, q=q: (0, q * nq + j))

    vec = pl.BlockSpec((1, MIX_BN), lambda j, i: (0, j))
    heads_per_step = MIX_BN // GROUP_DIM
    in_specs = [
        pl.BlockSpec((MIX_BM, D_MODEL), lambda j, i: (i, 0)),
        wspec(0), wspec(1), wspec(2), wspec(3),
        bspec(0), bspec(1), bspec(2), bspec(3),
        pl.BlockSpec((CONV_WIDTH, MIX_BN), lambda j, i: (0, j)),
        vec, vec, vec, vec, vec,
        pl.BlockSpec((heads_per_step, CHUNK, CHUNK), lambda j, i: (j, 0, 0)),
        pl.BlockSpec((heads_per_step, CHUNK, 1), lambda j, i: (j, 0, 0)),
    ]
    out_spec = pl.BlockSpec((MIX_BM, MIX_BN), lambda j, i: (i, j))
    return pl.pallas_call(
        _mixer_kernel,
        grid=grid,
        in_specs=in_specs,
        out_specs=[out_spec, out_spec],
        out_shape=[jax.ShapeDtypeStruct((SEQ, D_CONV), jnp.bfloat16),
                   jax.ShapeDtypeStruct((SEQ, D_GMLP), jnp.bfloat16)],
        scratch_shapes=[pltpu.VMEM((HALO + MIX_BM, MIX_BN), jnp.float32)],
        compiler_params=pltpu.CompilerParams(
            dimension_semantics=("arbitrary", "arbitrary"),
            vmem_limit_bytes=VMEM_LIMIT_BYTES),
        name="mixer",
    )(xb, w_in_b, w_in_b, w_in_b, w_in_b, b_in, b_in, b_in, b_in,
      conv_w, conv_b, cng, cnb, gng, gnb, w_spatial, b_spatial_col)


def _outproj_kernel(ya_ref, yg_ref, x_ref, w_ref, bo_ref, g_ref, b_ref, wr_ref, br_ref,
                    x1_ref, logit_ref):
    f32 = jnp.float32
    z = jnp.dot(ya_ref[...], w_ref[0:D_CONV, :], preferred_element_type=f32)
    z = z + jnp.dot(yg_ref[...], w_ref[D_CONV:, :], preferred_element_type=f32)
    z = z + bo_ref[...] + ALPHA * x_ref[...]
    mu = jnp.mean(z, axis=-1, keepdims=True)
    d = z - mu
    var = jnp.mean(d * d, axis=-1, keepdims=True)
    x1 = d * lax.rsqrt(var + LN_EPS) * g_ref[...] + b_ref[...]
    x1_ref[...] = x1
    wr = wr_ref[...]
    wr_hi = wr.astype(jnp.bfloat16)
    wr_lo = (wr - wr_hi.astype(f32)).astype(jnp.bfloat16)
    x_hi = x1.astype(jnp.bfloat16)
    x_lo = (x1 - x_hi.astype(f32)).astype(jnp.bfloat16)
    logits = jnp.dot(x_hi, wr_hi, preferred_element_type=f32)
    logits = logits + (jnp.dot(x_lo, wr_hi, preferred_element_type=f32)
                       + jnp.dot(x_hi, wr_lo, preferred_element_type=f32))
    logit_ref[...] = logits + br_ref[...]


def _outproj_call(ya, yg, x2d, w_out_b, b_out, ln_g, ln_b, w_router, b_router):
    grid = (SEQ // OUT_BM,)
    row_half = pl.BlockSpec((OUT_BM, D_CONV), lambda i: (i, 0))
    row_full = pl.BlockSpec((OUT_BM, D_MODEL), lambda i: (i, 0))
    vec = pl.BlockSpec((1, D_MODEL), lambda i: (0, 0))
    in_specs = [
        row_half, row_half, row_full,
        pl.BlockSpec((D_MODEL, D_MODEL), lambda i: (0, 0), pipeline_mode=pl.Buffered(1)),
        vec, vec, vec,
        pl.BlockSpec((D_MODEL, ROUTER_COLS), lambda i: (0, 0)),
        pl.BlockSpec((1, ROUTER_COLS), lambda i: (0, 0)),
    ]
    return pl.pallas_call(
        _outproj_kernel,
        grid=grid,
        in_specs=in_specs,
        out_specs=[row_full, pl.BlockSpec((OUT_BM, ROUTER_COLS), lambda i: (i, 0))],
        out_shape=[jax.ShapeDtypeStruct((SEQ, D_MODEL), jnp.float32),
                   jax.ShapeDtypeStruct((SEQ, ROUTER_COLS), jnp.float32)],
        compiler_params=pltpu.CompilerParams(
            dimension_semantics=("arbitrary",),
            vmem_limit_bytes=VMEM_LIMIT_BYTES),
        name="outproj",
    )(ya, yg, x2d, w_out_b, b_out, ln_g, ln_b, w_router, b_router)


def _expert_kernel(be_ref, nused_ref, tok_ref, x1_hbm, wg_ref, wu_ref, wd_ref, out_ref, xbuf, sem):
    b = pl.program_id(0)

    def row_copy(r):
        t = tok_ref[b * ROW_BLOCK + r]
        return pltpu.make_async_copy(x1_hbm.at[pl.ds(t, 1)], xbuf.at[pl.ds(r, 1)], sem)

    @pl.when(b < nused_ref[0])
    def _():
        def issue(r, carry):
            row_copy(r).start()
            return carry
        lax.fori_loop(0, ROW_BLOCK, issue, 0)
        pltpu.make_async_copy(x1_hbm.at[pl.ds(0, ROW_BLOCK)], xbuf, sem).wait()
        xb = xbuf[...]
        g = jnp.dot(xb, wg_ref[...], preferred_element_type=jnp.float32)
        u = jnp.dot(xb, wu_ref[...], preferred_element_type=jnp.float32)
        h = jax.nn.silu(g) * u
        out_ref[...] = jnp.dot(h, wd_ref[...], preferred_element_type=jnp.float32)

    @pl.when(b >= nused_ref[0])
    def _():
        out_ref[...] = jnp.zeros(out_ref.shape, out_ref.dtype)


def _expert_call(block_expert, n_used, tok_of, x1, w_gate, w_up, w_down):
    grid_spec = pltpu.PrefetchScalarGridSpec(
        num_scalar_prefetch=3,
        grid=(N_BLOCKS,),
        in_specs=[
            pl.BlockSpec(memory_space=pl.ANY),
            pl.BlockSpec((None, D_MODEL, D_EXPERT), lambda b, be, nu, tk: (be[b], 0, 0)),
            pl.BlockSpec((None, D_MODEL, D_EXPERT), lambda b, be, nu, tk: (be[b], 0, 0)),
            pl.BlockSpec((None, D_EXPERT, D_MODEL), lambda b, be, nu, tk: (be[b], 0, 0)),
        ],
        out_specs=pl.BlockSpec((ROW_BLOCK, D_MODEL), lambda b, be, nu, tk: (b, 0)),
        scratch_shapes=[pltpu.VMEM((ROW_BLOCK, D_MODEL), jnp.float32),
                        pltpu.SemaphoreType.DMA(())],
    )
    return pl.pallas_call(
        _expert_kernel,
        grid_spec=grid_spec,
        out_shape=jax.ShapeDtypeStruct((N_BLOCKS * ROW_BLOCK, D_MODEL), jnp.float32),
        compiler_params=pltpu.CompilerParams(
            dimension_semantics=("arbitrary",),
            vmem_limit_bytes=VMEM_LIMIT_BYTES),
        name="experts",
    )(block_expert, n_used, tok_of, x1, w_gate, w_up, w_down)


def _combine_kernel(pos_ref, ys_hbm, x1_ref, gate_ref, g_ref, b_ref, out_ref, ybuf, sem):
    i = pl.program_id(0)

    def issue(r, carry):
        for k in range(TOP_K):
            p = pos_ref[(i * COMB_BM + r) * TOP_K + k]
            pltpu.make_async_copy(ys_hbm.at[pl.ds(p, 1)], ybuf.at[pl.ds(k * COMB_BM + r, 1)], sem).start()
        return carry
    lax.fori_loop(0, COMB_BM, issue, 0)
    pltpu.make_async_copy(ys_hbm.at[pl.ds(0, TOP_K * COMB_BM)], ybuf, sem).wait()
    gate = gate_ref[...]
    ffn = ybuf[0:COMB_BM, :] * gate[:, 0:1] + ybuf[COMB_BM:, :] * gate[:, 1:2]
    z = ALPHA * x1_ref[...] + ffn
    mu = jnp.mean(z, axis=-1, keepdims=True)
    d = z - mu
    var = jnp.mean(d * d, axis=-1, keepdims=True)
    out_ref[...] = d * lax.rsqrt(var + LN_EPS) * g_ref[...] + b_ref[...]


def _combine_call(pos, ys, x1, gates, ln_g, ln_b):
    row_full = pl.BlockSpec((COMB_BM, D_MODEL), lambda i, pos: (i, 0))
    vec = pl.BlockSpec((1, D_MODEL), lambda i, pos: (0, 0))
    grid_spec = pltpu.PrefetchScalarGridSpec(
        num_scalar_prefetch=1,
        grid=(SEQ // COMB_BM,),
        in_specs=[
            pl.BlockSpec(memory_space=pl.ANY),
            row_full,
            pl.BlockSpec((COMB_BM, TOP_K), lambda i, pos: (i, 0)),
            vec, vec,
        ],
        out_specs=row_full,
        scratch_shapes=[pltpu.VMEM((TOP_K * COMB_BM, D_MODEL), jnp.float32),
                        pltpu.SemaphoreType.DMA(())],
    )
    return pl.pallas_call(
        _combine_kernel,
        grid_spec=grid_spec,
        out_shape=jax.ShapeDtypeStruct((SEQ, D_MODEL), jnp.float32),
        compiler_params=pltpu.CompilerParams(
            dimension_semantics=("arbitrary",),
            vmem_limit_bytes=VMEM_LIMIT_BYTES),
        name="combine",
    )(pos, ys, x1, gates, ln_g, ln_b)


def _route(logits):
    group_logits = logits[:, :N_GROUPS]
    g_sel = jnp.argmax(group_logits, axis=-1).astype(jnp.int32)
    g_weight = jnp.take_along_axis(jax.nn.softmax(group_logits, axis=-1), g_sel[:, None], axis=1)[:, 0]
    inner_all = logits[:, N_GROUPS:N_GROUPS + N_EXPERTS].reshape(SEQ, N_GROUPS, EXPERTS_PER_GROUP)
    inner = jnp.take_along_axis(inner_all, g_sel[:, None, None], axis=1)[:, 0]
    top_logits, top_idx = lax.top_k(inner, TOP_K)
    gates = g_weight[:, None] * jax.nn.softmax(top_logits, axis=-1)
    expert_id = (g_sel[:, None] * EXPERTS_PER_GROUP + top_idx).reshape(-1).astype(jnp.int32)
    onehot = (expert_id[:, None] == jnp.arange(N_EXPERTS, dtype=jnp.int32)[None, :]).astype(jnp.int32)
    before = jnp.cumsum(onehot, axis=0) - onehot
    rank = jnp.sum(before * onehot, axis=1)
    counts = jnp.sum(onehot, axis=0)
    pcounts = ((counts + ROW_BLOCK - 1) // ROW_BLOCK) * ROW_BLOCK
    pend = jnp.cumsum(pcounts)
    poffs = pend - pcounts
    pos = (jnp.sum(onehot * poffs[None, :], axis=1) + rank).astype(jnp.int32)
    tok = jnp.arange(N_ROWS, dtype=jnp.int32) // TOP_K
    tok_of = jnp.zeros((N_BLOCKS * ROW_BLOCK,), jnp.int32).at[pos].set(tok)
    starts = jnp.arange(N_BLOCKS, dtype=jnp.int32) * ROW_BLOCK
    block_expert = jnp.minimum(jnp.searchsorted(pend, starts, side="right"),
                               N_EXPERTS - 1).astype(jnp.int32)
    n_used = (pend[-1] // ROW_BLOCK).astype(jnp.int32).reshape(1)
    return gates, pos, tok_of, block_expert, n_used


def kernel(x, w_in, b_in, conv_w, conv_b, conv_norm_g, conv_norm_b, gmlp_norm_g, gmlp_norm_b,
           w_spatial, b_spatial, w_out, b_out, ln1_g, ln1_b, w_router_group, b_router_group,
           w_router_expert, b_router_expert, w_expert_gate, w_expert_up, w_expert_down,
           ln2_g, ln2_b):
    assert x.shape == (1, SEQ, D_MODEL) and w_in.shape[0] == 1
    x2d = x.reshape(SEQ, D_MODEL)
    xb = x2d.astype(jnp.bfloat16)
    ya, yg = _mixer_call(
        xb, w_in[0].astype(jnp.bfloat16), b_in, conv_w[0], conv_b, conv_norm_g, conv_norm_b,
        gmlp_norm_g, gmlp_norm_b, w_spatial[0], b_spatial[0][:, :, None])

    pad = ROUTER_COLS - N_GROUPS - N_EXPERTS
    w_router = jnp.concatenate(
        [w_router_group[0],
         jnp.transpose(w_router_expert[0], (1, 0, 2)).reshape(D_MODEL, N_EXPERTS),
         jnp.zeros((D_MODEL, pad), jnp.float32)], axis=1)
    b_router = jnp.concatenate(
        [b_router_group[0], b_router_expert[0].reshape(-1), jnp.zeros((pad,), jnp.float32)])[None, :]
    x1, logits = _outproj_call(ya, yg, x2d, w_out[0].astype(jnp.bfloat16), b_out, ln1_g, ln1_b,
                               w_router, b_router)

    gates, pos, tok_of, block_expert, n_used = _route(logits)
    ys = _expert_call(block_expert, n_used, tok_of, x1,
                      w_expert_gate[0], w_expert_up[0], w_expert_down[0])
    out = _combine_call(pos, ys, x1, gates, ln2_g, ln2_b)
    return out.reshape(1, SEQ, D_MODEL)
```

```python
import functools

import jax
import jax.numpy as jnp
from jax import lax
from jax.experimental import pallas as pl
from jax.experimental.pallas import tpu as pltpu

D_MODEL = 4096
SEQ = 8192
D_CONV = D_MODEL // 2
D_GMLP = D_MODEL // 2
CONV_WIDTH = 31
GROUP_DIM = 128
GMLP_HEADS = 16
CHUNK = 128
N_GROUPS = 8
EXPERTS_PER_GROUP = 8
N_EXPERTS = N_GROUPS * EXPERTS_PER_GROUP
TOP_K = 2
D_EXPERT = D_MODEL // 8
ROW_BLOCK = 128
LN_EPS = 1e-5
ALPHA = 2.0 ** 0.25

N_ROWS = SEQ * TOP_K
N_BLOCKS = N_ROWS // ROW_BLOCK + N_EXPERTS
ROUTER_COLS = 128

VMEM_LIMIT_BYTES = 60 * 1024 * 1024

MIX_BM = 512
MIX_BN = 256
HALO = 32
CONV_ROWS = 64

OUT_BM = 256
COMB_BM = 128


def _group_norm(v, g, b):
    outs = []
    for s in range(v.shape[1] // GROUP_DIM):
        blk = v[:, s * GROUP_DIM:(s + 1) * GROUP_DIM]
        mu = jnp.mean(blk, axis=-1, keepdims=True)
        d = blk - mu
        var = jnp.mean(d * d, axis=-1, keepdims=True)
        outs.append(d * lax.rsqrt(var + LN_EPS))
    return jnp.concatenate(outs, axis=-1) * g + b


def _mixer_kernel(x_ref, wa_ref, wg_ref, wu_ref, wv_ref, ba_ref, bg_ref, bu_ref, bv_ref,
                  cw_ref, cb_ref, cng_ref, cnb_ref, gng_ref, gnb_ref, wsp_ref, bsp_ref,
                  ya_ref, yg_ref, xs_ref):
    i = pl.program_id(1)
    x = x_ref[...]
    f32 = jnp.float32

    a = jnp.dot(x, wa_ref[...], preferred_element_type=f32) + ba_ref[...]
    gate = jnp.dot(x, wg_ref[...], preferred_element_type=f32) + bg_ref[...]
    a = a * jax.nn.sigmoid(gate)

    @pl.when(i == 0)
    def _():
        xs_ref[0:HALO, :] = jnp.zeros((HALO, MIX_BN), f32)

    xs_ref[HALO:HALO + MIX_BM, :] = a
    off = HALO - (CONV_WIDTH - 1)
    conv_chunks = []
    for r in range(MIX_BM // CONV_ROWS):
        acc = jnp.zeros((CONV_ROWS, MIX_BN), f32) + cb_ref[...]
        for k in range(CONV_WIDTH):
            start = r * CONV_ROWS + off + k
            acc = acc + cw_ref[k:k + 1, :] * xs_ref[start:start + CONV_ROWS, :]
        conv_chunks.append(acc)
    conv = jnp.concatenate(conv_chunks, axis=0)
    xs_ref[0:HALO, :] = a[MIX_BM - HALO:, :]
    ya_ref[...] = jax.nn.silu(_group_norm(conv, cng_ref[...], cnb_ref[...])).astype(ya_ref.dtype)

    u = jax.nn.gelu(jnp.dot(x, wu_ref[...], preferred_element_type=f32) + bu_ref[...])
    v = jax.nn.gelu(jnp.dot(x, wv_ref[...], preferred_element_type=f32) + bv_ref[...])
    v = _group_norm(v, gng_ref[...], gnb_ref[...]).astype(jnp.bfloat16)
    row = lax.broadcasted_iota(jnp.int32, (CHUNK, CHUNK), 0)
    col = lax.broadcasted_iota(jnp.int32, (CHUNK, CHUNK), 1)
    n_chunks = MIX_BM // CHUNK
    gated = []
    for h in range(MIX_BN // GROUP_DIM):
        w_h = jnp.where(row >= col, wsp_ref[h], 0.0).astype(jnp.bfloat16)
        v_h = jnp.concatenate(
            [v[c * CHUNK:(c + 1) * CHUNK, h * GROUP_DIM:(h + 1) * GROUP_DIM] for c in range(n_chunks)],
            axis=1)
        s_h = jnp.dot(w_h, v_h, preferred_element_type=f32) + bsp_ref[h]
        gated.append(jnp.concatenate(
            [s_h[:, c * GROUP_DIM:(c + 1) * GROUP_DIM] for c in range(n_chunks)], axis=0))
    yg_ref[...] = (u * jnp.concatenate(gated, axis=1)).astype(yg_ref.dtype)


def _mixer_call(xb, w_in_b, b_in, conv_w, conv_b, cng, cnb, gng, gnb, w_spatial, b_spatial_col):
    nq = D_CONV // MIX_BN
    grid = (nq, SEQ // MIX_BM)

    def wspec(q):
        return pl.BlockSpec((D_MODEL, MIX_BN), lambda j, i, q=q: (0, q * nq + j))

    def bspec(q):
        return pl.BlockSpec((1, MIX_BN), lambda j, i, q=q: (0, q * nq + j))

    vec = pl.BlockSpec((1, MIX_BN), lambda j, i: (0, j))
    heads_per_step = MIX_BN // GROUP_DIM
    in_specs = [
        pl.BlockSpec((MIX_BM, D_MODEL), lambda j, i: (i, 0)),
        wspec(0), wspec(1), wspec(2), wspec(3),
        bspec(0), bspec(1), bspec(2), bspec(3),
        pl.BlockSpec((CONV_WIDTH, MIX_BN), lambda j, i: (0, j)),
        vec, vec, vec, vec, vec,
        pl.BlockSpec((heads_per_step, CHUNK, CHUNK), lambda j, i: (j, 0, 0)),
        pl.BlockSpec((heads_per_step, CHUNK, 1), lambda j, i: (j, 0, 0)),
    ]
    out_spec = pl.BlockSpec((MIX_BM, MIX_BN), lambda j, i: (i, j))
    return pl.pallas_call(
        _mixer_kernel,
        grid=grid,
        in_specs=in_specs,
        out_specs=[out_spec, out_spec],
        out_shape=[jax.ShapeDtypeStruct((SEQ, D_CONV), jnp.bfloat16),
                   jax.ShapeDtypeStruct((SEQ, D_GMLP), jnp.bfloat16)],
        scratch_shapes=[pltpu.VMEM((HALO + MIX_BM, MIX_BN), jnp.float32)],
        compiler_params=pltpu.CompilerParams(
            dimension_semantics=("arbitrary", "arbitrary"),
            vmem_limit_bytes=VMEM_LIMIT_BYTES),
        name="mixer",
    )(xb, w_in_b, w_in_b, w_in_b, w_in_b, b_in, b_in, b_in, b_in,
      conv_w, conv_b, cng, cnb, gng, gnb, w_spatial, b_spatial_col)


def _outproj_kernel(ya_ref, yg_ref, x_ref, w_ref, bo_ref, g_ref, b_ref, wr_ref, br_ref,
                    x1_ref, logit_ref):
    f32 = jnp.float32
    z = jnp.dot(ya_ref[...], w_ref[0:D_CONV, :], preferred_element_type=f32)
    z = z + jnp.dot(yg_ref[...], w_ref[D_CONV:, :], preferred_element_type=f32)
    z = z + bo_ref[...] + ALPHA * x_ref[...]
    mu = jnp.mean(z, axis=-1, keepdims=True)
    d = z - mu
    var = jnp.mean(d * d, axis=-1, keepdims=True)
    x1 = d * lax.rsqrt(var + LN_EPS) * g_ref[...] + b_ref[...]
    x1_ref[...] = x1
    wr = wr_ref[...]
    wr_hi = wr.astype(jnp.bfloat16)
    wr_lo = (wr - wr_hi.astype(f32)).astype(jnp.bfloat16)
    x_hi = x1.astype(jnp.bfloat16)
    x_lo = (x1 - x_hi.astype(f32)).astype(jnp.bfloat16)
    logits = jnp.dot(x_hi, wr_hi, preferred_element_type=f32)
    logits = logits + (jnp.dot(x_lo, wr_hi, preferred_element_type=f32)
                       + jnp.dot(x_hi, wr_lo, preferred_element_type=f32))
    logit_ref[...] = logits + br_ref[...]


def _outproj_call(ya, yg, x2d, w_out_b, b_out, ln_g, ln_b, w_router, b_router):
    grid = (SEQ // OUT_BM,)
    row_half = pl.BlockSpec((OUT_BM, D_CONV), lambda i: (i, 0))
    row_full = pl.BlockSpec((OUT_BM, D_MODEL), lambda i: (i, 0))
    vec = pl.BlockSpec((1, D_MODEL), lambda i: (0, 0))
    in_specs = [
        row_half, row_half, row_full,
        pl.BlockSpec((D_MODEL, D_MODEL), lambda i: (0, 0), pipeline_mode=pl.Buffered(1)),
        vec, vec, vec,
        pl.BlockSpec((D_MODEL, ROUTER_COLS), lambda i: (0, 0)),
        pl.BlockSpec((1, ROUTER_COLS), lambda i: (0, 0)),
    ]
    return pl.pallas_call(
        _outproj_kernel,
        grid=grid,
        in_specs=in_specs,
        out_specs=[row_full, pl.BlockSpec((OUT_BM, ROUTER_COLS), lambda i: (i, 0))],
        out_shape=[jax.ShapeDtypeStruct((SEQ, D_MODEL), jnp.float32),
                   jax.ShapeDtypeStruct((SEQ, ROUTER_COLS), jnp.float32)],
        compiler_params=pltpu.CompilerParams(
            dimension_semantics=("arbitrary",),
            vmem_limit_bytes=VMEM_LIMIT_BYTES),
        name="outproj",
    )(ya, yg, x2d, w_out_b, b_out, ln_g, ln_b, w_router, b_router)


def _expert_kernel(be_ref, nused_ref, first_ref, slot_ref, nxt_ref, tok_ref,
                   x1_hbm, wg_hbm, wu_hbm, wd_hbm, out_ref, xbuf, wgb, wub, wdb, gsem, wsem):
    b = pl.program_id(0)
    n_used = nused_ref[0]

    def weight_copies(e, s):
        return (pltpu.make_async_copy(wg_hbm.at[e], wgb.at[s], wsem.at[s, 0]),
                pltpu.make_async_copy(wu_hbm.at[e], wub.at[s], wsem.at[s, 1]),
                pltpu.make_async_copy(wd_hbm.at[e], wdb.at[s], wsem.at[s, 2]))

    def gather_start(blk, s):
        def issue(r, carry):
            t = tok_ref[blk * ROW_BLOCK + r]
            pltpu.make_async_copy(x1_hbm.at[pl.ds(t, 1)], xbuf.at[s, pl.ds(r, 1)], gsem.at[s]).start()
            return carry
        lax.fori_loop(0, ROW_BLOCK, issue, 0)

    def gather_wait(s):
        pltpu.make_async_copy(x1_hbm.at[pl.ds(0, ROW_BLOCK)], xbuf.at[s], gsem.at[s]).wait()

    @pl.when(b == 0)
    def _():
        for c in weight_copies(be_ref[0], 0):
            c.start(priority=1)
        gather_start(0, 0)

    @pl.when(b < n_used)
    def _():
        ws = slot_ref[b]
        xs = lax.rem(b, 2)

        @pl.when(first_ref[b] == 1)
        def _():
            @pl.when(nxt_ref[b] >= 0)
            def _():
                for c in weight_copies(nxt_ref[b], 1 - ws):
                    c.start(priority=1)
            for c in weight_copies(be_ref[b], ws):
                c.wait()

        @pl.when(b + 1 < n_used)
        def _():
            gather_start(b + 1, 1 - xs)

        gather_wait(xs)
        xb = xbuf[xs]
        g = jnp.dot(xb, wgb[ws], preferred_element_type=jnp.float32)
        u = jnp.dot(xb, wub[ws], preferred_element_type=jnp.float32)
        h = jax.nn.silu(g) * u
        out_ref[...] = jnp.dot(h, wdb[ws], preferred_element_type=jnp.float32)

    @pl.when(b >= n_used)
    def _():
        out_ref[...] = jnp.zeros(out_ref.shape, out_ref.dtype)


def _expert_call(block_expert, n_used, first, slot, nxt, tok_of, x1, w_gate, w_up, w_down):
    any_spec = pl.BlockSpec(memory_space=pl.ANY)
    grid_spec = pltpu.PrefetchScalarGridSpec(
        num_scalar_prefetch=6,
        grid=(N_BLOCKS,),
        in_specs=[any_spec, any_spec, any_spec, any_spec],
        out_specs=pl.BlockSpec((ROW_BLOCK, D_MODEL), lambda b, *_: (b, 0)),
        scratch_shapes=[pltpu.VMEM((2, ROW_BLOCK, D_MODEL), jnp.float32),
                        pltpu.VMEM((2, D_MODEL, D_EXPERT), jnp.float32),
                        pltpu.VMEM((2, D_MODEL, D_EXPERT), jnp.float32),
                        pltpu.VMEM((2, D_EXPERT, D_MODEL), jnp.float32),
                        pltpu.SemaphoreType.DMA((2,)),
                        pltpu.SemaphoreType.DMA((2, 3))],
    )
    return pl.pallas_call(
        _expert_kernel,
        grid_spec=grid_spec,
        out_shape=jax.ShapeDtypeStruct((N_BLOCKS * ROW_BLOCK, D_MODEL), jnp.float32),
        compiler_params=pltpu.CompilerParams(
            dimension_semantics=("arbitrary",),
            vmem_limit_bytes=VMEM_LIMIT_BYTES),
        name="experts",
    )(block_expert, n_used, first, slot, nxt, tok_of, x1, w_gate, w_up, w_down)


def _combine_kernel(pos_ref, ys_hbm, x1_ref, gate_ref, g_ref, b_ref, out_ref, ybuf, sem):
    i = pl.program_id(0)
    n_steps = pl.num_programs(0)
    rows = TOP_K * COMB_BM

    def gather_start(tile, s):
        def issue(r, carry):
            for k in range(TOP_K):
                p = pos_ref[(tile * COMB_BM + r) * TOP_K + k]
                pltpu.make_async_copy(ys_hbm.at[pl.ds(p, 1)],
                                      ybuf.at[s, pl.ds(k * COMB_BM + r, 1)], sem.at[s]).start()
            return carry
        lax.fori_loop(0, COMB_BM, issue, 0)

    @pl.when(i == 0)
    def _():
        gather_start(0, 0)

    s = lax.rem(i, 2)

    @pl.when(i + 1 < n_steps)
    def _():
        gather_start(i + 1, 1 - s)

    pltpu.make_async_copy(ys_hbm.at[pl.ds(0, rows)], ybuf.at[s], sem.at[s]).wait()
    gate = gate_ref[...]
    ffn = ybuf[s, 0:COMB_BM, :] * gate[:, 0:1] + ybuf[s, COMB_BM:, :] * gate[:, 1:2]
    z = ALPHA * x1_ref[...] + ffn
    mu = jnp.mean(z, axis=-1, keepdims=True)
    d = z - mu
    var = jnp.mean(d * d, axis=-1, keepdims=True)
    out_ref[...] = d * lax.rsqrt(var + LN_EPS) * g_ref[...] + b_ref[...]


def _combine_call(pos, ys, x1, gates, ln_g, ln_b):
    row_full = pl.BlockSpec((COMB_BM, D_MODEL), lambda i, pos: (i, 0))
    vec = pl.BlockSpec((1, D_MODEL), lambda i, pos: (0, 0))
    grid_spec = pltpu.PrefetchScalarGridSpec(
        num_scalar_prefetch=1,
        grid=(SEQ // COMB_BM,),
        in_specs=[
            pl.BlockSpec(memory_space=pl.ANY),
            row_full,
            pl.BlockSpec((COMB_BM, TOP_K), lambda i, pos: (i, 0)),
            vec, vec,
        ],
        out_specs=row_full,
        scratch_shapes=[pltpu.VMEM((2, TOP_K * COMB_BM, D_MODEL), jnp.float32),
                        pltpu.SemaphoreType.DMA((2,))],
    )
    return pl.pallas_call(
        _combine_kernel,
        grid_spec=grid_spec,
        out_shape=jax.ShapeDtypeStruct((SEQ, D_MODEL), jnp.float32),
        compiler_params=pltpu.CompilerParams(
            dimension_semantics=("arbitrary",),
            vmem_limit_bytes=VMEM_LIMIT_BYTES),
        name="combine",
    )(pos, ys, x1, gates, ln_g, ln_b)


def _route(logits):
    group_logits = logits[:, :N_GROUPS]
    g_sel = jnp.argmax(group_logits, axis=-1).astype(jnp.int32)
    g_weight = jnp.take_along_axis(jax.nn.softmax(group_logits, axis=-1), g_sel[:, None], axis=1)[:, 0]
    inner_all = logits[:, N_GROUPS:N_GROUPS + N_EXPERTS].reshape(SEQ, N_GROUPS, EXPERTS_PER_GROUP)
    inner = jnp.take_along_axis(inner_all, g_sel[:, None, None], axis=1)[:, 0]
    top_logits, top_idx = lax.top_k(inner, TOP_K)
    gates = g_weight[:, None] * jax.nn.softmax(top_logits, axis=-1)
    expert_id = (g_sel[:, None] * EXPERTS_PER_GROUP + top_idx).reshape(-1).astype(jnp.int32)
    onehot = (expert_id[:, None] == jnp.arange(N_EXPERTS, dtype=jnp.int32)[None, :]).astype(jnp.int32)
    before = jnp.cumsum(onehot, axis=0) - onehot
    rank = jnp.sum(before * onehot, axis=1)
    counts = jnp.sum(onehot, axis=0)
    pcounts = ((counts + ROW_BLOCK - 1) // ROW_BLOCK) * ROW_BLOCK
    pend = jnp.cumsum(pcounts)
    poffs = pend - pcounts
    pos = (jnp.sum(onehot * poffs[None, :], axis=1) + rank).astype(jnp.int32)
    tok = jnp.arange(N_ROWS, dtype=jnp.int32) // TOP_K
    tok_of = jnp.zeros((N_BLOCKS * ROW_BLOCK,), jnp.int32).at[pos].set(tok)
    starts = jnp.arange(N_BLOCKS, dtype=jnp.int32) * ROW_BLOCK
    block_expert = jnp.minimum(jnp.searchsorted(pend, starts, side="right"),
                               N_EXPERTS - 1).astype(jnp.int32)
    n_used = (pend[-1] // ROW_BLOCK).astype(jnp.int32).reshape(1)
    prev = jnp.concatenate([jnp.full((1,), -1, jnp.int32), block_expert[:-1]])
    first = (block_expert != prev).astype(jnp.int32)
    slot = lax.rem(jnp.cumsum(first) - 1, 2).astype(jnp.int32)
    eids = jnp.arange(N_EXPERTS, dtype=jnp.int32)
    cand = jnp.where(counts > 0, eids, N_EXPERTS)
    later = lax.cummin(cand, axis=0, reverse=True)
    nxt_e = jnp.concatenate([later[1:], jnp.full((1,), N_EXPERTS, jnp.int32)])
    nxt_e = jnp.where(nxt_e >= N_EXPERTS, -1, nxt_e).astype(jnp.int32)
    nxt = nxt_e[block_expert]
    return gates, pos, tok_of, block_expert, n_used, first, slot, nxt


def kernel(x, w_in, b_in, conv_w, conv_b, conv_norm_g, conv_norm_b, gmlp_norm_g, gmlp_norm_b,
           w_spatial, b_spatial, w_out, b_out, ln1_g, ln1_b, w_router_group, b_router_group,
           w_router_expert, b_router_expert, w_expert_gate, w_expert_up, w_expert_down,
           ln2_g, ln2_b):
    assert x.shape == (1, SEQ, D_MODEL) and w_in.shape[0] == 1
    x2d = x.reshape(SEQ, D_MODEL)
    xb = x2d.astype(jnp.bfloat16)
    ya, yg = _mixer_call(
        xb, w_in[0].astype(jnp.bfloat16), b_in, conv_w[0], conv_b, conv_norm_g, conv_norm_b,
        gmlp_norm_g, gmlp_norm_b, w_spatial[0], b_spatial[0][:, :, None])

    pad = ROUTER_COLS - N_GROUPS - N_EXPERTS
    w_router = jnp.concatenate(
        [w_router_group[0],
         jnp.transpose(w_router_expert[0], (1, 0, 2)).reshape(D_MODEL, N_EXPERTS),
         jnp.zeros((D_MODEL, pad), jnp.float32)], axis=1)
    b_router = jnp.concatenate(
        [b_router_group[0], b_router_expert[0].reshape(-1), jnp.zeros((pad,), jnp.float32)])[None, :]
    x1, logits = _outproj_call(ya, yg, x2d, w_out[0].astype(jnp.bfloat16), b_out, ln1_g, ln1_b,
                               w_router, b_router)

    gates, pos, tok_of, block_expert, n_used, first, slot, nxt = _route(logits)
    ys = _expert_call(block_expert, n_used, first, slot, nxt, tok_of, x1,
                      w_expert_gate[0], w_expert_up[0], w_expert_down[0])
    out = _combine_call(pos, ys, x1, gates, ln2_g, ln2_b)
    return out.reshape(1, SEQ, D_MODEL)
```

```python
import functools

import jax
import jax.numpy as jnp
from jax import lax
from jax.experimental import pallas as pl
from jax.experimental.pallas import tpu as pltpu

D_MODEL = 4096
SEQ = 8192
D_CONV = D_MODEL // 2
D_GMLP = D_MODEL // 2
CONV_WIDTH = 31
GROUP_DIM = 128
GMLP_HEADS = 16
CHUNK = 128
N_GROUPS = 8
EXPERTS_PER_GROUP = 8
N_EXPERTS = N_GROUPS * EXPERTS_PER_GROUP
TOP_K = 2
D_EXPERT = D_MODEL // 8
ROW_BLOCK = 128
LN_EPS = 1e-5
ALPHA = 2.0 ** 0.25

N_ROWS = SEQ * TOP_K
N_BLOCKS = N_ROWS // ROW_BLOCK + N_EXPERTS
ROUTER_COLS = 128

VMEM_LIMIT_BYTES = 60 * 1024 * 1024

MIX_BM = 1024
MIX_SUB = 256
MIX_BN = 256
HALO = 32
CONV_ROWS = 64

OUT_BM = 256
COMB_BM = 128


def _group_norm(v, g, b):
    outs = []
    for s in range(v.shape[1] // GROUP_DIM):
        blk = v[:, s * GROUP_DIM:(s + 1) * GROUP_DIM]
        mu = jnp.mean(blk, axis=-1, keepdims=True)
        d = blk - mu
        var = jnp.mean(d * d, axis=-1, keepdims=True)
        outs.append(d * lax.rsqrt(var + LN_EPS))
    return jnp.concatenate(outs, axis=-1) * g + b


def _mixer_kernel(x_ref, wa_ref, wg_ref, wu_ref, wv_ref, ba_ref, bg_ref, bu_ref, bv_ref,
                  cw_ref, cb_ref, cng_ref, cnb_ref, gng_ref, gnb_ref, wsp_ref, bsp_ref,
                  ya_ref, yg_ref, xs_ref, r_ref):
    i = pl.program_id(1)
    f32 = jnp.float32
    bf16 = jnp.bfloat16
    w_refs = (wa_ref, wg_ref, wu_ref, wv_ref)
    off = HALO - (CONV_WIDTH - 1)
    win_rows = CONV_ROWS + HALO

    @pl.when(i == 0)
    def _():
        xs_ref[0:HALO, :] = jnp.zeros((HALO, MIX_BN), f32)

    row = lax.broadcasted_iota(jnp.int32, (CHUNK, CHUNK), 0)
    col = lax.broadcasted_iota(jnp.int32, (CHUNK, CHUNK), 1)
    w_sp = [jnp.where(row >= col, wsp_ref[h], 0.0).astype(bf16) for h in range(MIX_BN // GROUP_DIM)]

    def project(k):
        xk = x_ref[k * MIX_SUB:(k + 1) * MIX_SUB, :]
        for q in range(4):
            r_ref[k % 2, q] = jnp.dot(xk, w_refs[q][...], preferred_element_type=f32)

    def finish(k):
        slot = k % 2
        base = k * MIX_SUB
        for c in range(MIX_SUB // CONV_ROWS):
            rows = slice(c * CONV_ROWS, (c + 1) * CONV_ROWS)
            a = r_ref[slot, 0, rows, :] + ba_ref[...]
            gate = r_ref[slot, 1, rows, :] + bg_ref[...]
            lo = HALO + base + c * CONV_ROWS
            xs_ref[lo:lo + CONV_ROWS, :] = a * jax.nn.sigmoid(gate)
        for c in range(MIX_SUB // CONV_ROWS):
            t0 = base + c * CONV_ROWS
            win = xs_ref[t0:t0 + win_rows, :]
            acc = jnp.zeros((CONV_ROWS, MIX_BN), f32) + cb_ref[...]
            for m in range(8):
                taps = [k2 for k2 in range(CONV_WIDTH) if (off + k2) % 8 == m]
                span = 8 * max((off + k2) // 8 for k2 in taps) + CONV_ROWS
                shifted = win[m:m + span, :]
                for k2 in taps:
                    j8 = 8 * ((off + k2) // 8)
                    acc = acc + cw_ref[k2:k2 + 1, :] * shifted[j8:j8 + CONV_ROWS, :]
            y = jax.nn.silu(_group_norm(acc, cng_ref[...], cnb_ref[...]))
            ya_ref[t0:t0 + CONV_ROWS, :] = y.astype(ya_ref.dtype)
        n_chunks = MIX_SUB // CHUNK
        us, vs = [], []
        for c in range(n_chunks):
            rows = slice(c * CHUNK, (c + 1) * CHUNK)
            us.append(jax.nn.gelu(r_ref[slot, 2, rows, :] + bu_ref[...]))
            v = jax.nn.gelu(r_ref[slot, 3, rows, :] + bv_ref[...])
            vs.append(_group_norm(v, gng_ref[...], gnb_ref[...]).astype(bf16))
        for h in range(MIX_BN // GROUP_DIM):
            lanes = slice(h * GROUP_DIM, (h + 1) * GROUP_DIM)
            v_h = jnp.concatenate([v[:, lanes] for v in vs], axis=1)
            s_h = jnp.dot(w_sp[h], v_h, preferred_element_type=f32) + bsp_ref[h]
            for c in range(n_chunks):
                g = us[c][:, lanes] * s_h[:, c * GROUP_DIM:(c + 1) * GROUP_DIM]
                yg_ref[base + c * CHUNK:base + (c + 1) * CHUNK, lanes] = g.astype(yg_ref.dtype)

    n_sub = MIX_BM // MIX_SUB
    project(0)
    for k in range(1, n_sub):
        project(k)
        finish(k - 1)
    finish(n_sub - 1)
    xs_ref[0:HALO, :] = xs_ref[MIX_BM:MIX_BM + HALO, :]


def _mixer_call(xb, w_in_b, b_in, conv_w, conv_b, cng, cnb, gng, gnb, w_spatial, b_spatial_col):
    nq = D_CONV // MIX_BN
    grid = (nq, SEQ // MIX_BM)

    def wspec(q):
        return pl.BlockSpec((D_MODEL, MIX_BN), lambda j, i, q=q: (0, q * nq + j))

    def bspec(q):
        return pl.BlockSpec((1, MIX_BN), lambda j, i, q=q: (0, q * nq + j))

    vec = pl.BlockSpec((1, MIX_BN), lambda j, i: (0, j))
    heads_per_step = MIX_BN // GROUP_DIM
    in_specs = [
        pl.BlockSpec((MIX_BM, D_MODEL), lambda j, i: (i, 0)),
        wspec(0), wspec(1), wspec(2), wspec(3),
        bspec(0), bspec(1), bspec(2), bspec(3),
        pl.BlockSpec((CONV_WIDTH, MIX_BN), lambda j, i: (0, j)),
        vec, vec, vec, vec, vec,
        pl.BlockSpec((heads_per_step, CHUNK, CHUNK), lambda j, i: (j, 0, 0)),
        pl.BlockSpec((heads_per_step, CHUNK, 1), lambda j, i: (j, 0, 0)),
    ]
    out_spec = pl.BlockSpec((MIX_BM, MIX_BN), lambda j, i: (i, j))
    return pl.pallas_call(
        _mixer_kernel,
        grid=grid,
        in_specs=in_specs,
        out_specs=[out_spec, out_spec],
        out_shape=[jax.ShapeDtypeStruct((SEQ, D_CONV), jnp.bfloat16),
                   jax.ShapeDtypeStruct((SEQ, D_GMLP), jnp.bfloat16)],
        scratch_shapes=[pltpu.VMEM((HALO + MIX_BM, MIX_BN), jnp.float32),
                        pltpu.VMEM((2, 4, MIX_SUB, MIX_BN), jnp.float32)],
        compiler_params=pltpu.CompilerParams(
            dimension_semantics=("arbitrary", "arbitrary"),
            vmem_limit_bytes=VMEM_LIMIT_BYTES),
        name="mixer",
    )(xb, w_in_b, w_in_b, w_in_b, w_in_b, b_in, b_in, b_in, b_in,
      conv_w, conv_b, cng, cnb, gng, gnb, w_spatial, b_spatial_col)


def _outproj_kernel(ya_ref, yg_ref, x_ref, w_ref, bo_ref, g_ref, b_ref, wr_ref, br_ref,
                    x1_ref, logit_ref):
    f32 = jnp.float32
    z = jnp.dot(ya_ref[...], w_ref[0:D_CONV, :], preferred_element_type=f32)
    z = z + jnp.dot(yg_ref[...], w_ref[D_CONV:, :], preferred_element_type=f32)
    z = z + bo_ref[...] + ALPHA * x_ref[...]
    mu = jnp.mean(z, axis=-1, keepdims=True)
    d = z - mu
    var = jnp.mean(d * d, axis=-1, keepdims=True)
    x1 = d * lax.rsqrt(var + LN_EPS) * g_ref[...] + b_ref[...]
    x1_ref[...] = x1
    wr = wr_ref[...]
    wr_hi = wr.astype(jnp.bfloat16)
    wr_lo = (wr - wr_hi.astype(f32)).astype(jnp.bfloat16)
    x_hi = x1.astype(jnp.bfloat16)
    x_lo = (x1 - x_hi.astype(f32)).astype(jnp.bfloat16)
    logits = jnp.dot(x_hi, wr_hi, preferred_element_type=f32)
    logits = logits + (jnp.dot(x_lo, wr_hi, preferred_element_type=f32)
                       + jnp.dot(x_hi, wr_lo, preferred_element_type=f32))
    logit_ref[...] = logits + br_ref[...]


def _outproj_call(ya, yg, x2d, w_out_b, b_out, ln_g, ln_b, w_router, b_router):
    grid = (SEQ // OUT_BM,)
    row_half = pl.BlockSpec((OUT_BM, D_CONV), lambda i: (i, 0))
    row_full = pl.BlockSpec((OUT_BM, D_MODEL), lambda i: (i, 0))
    vec = pl.BlockSpec((1, D_MODEL), lambda i: (0, 0))
    in_specs = [
        row_half, row_half, row_full,
        pl.BlockSpec((D_MODEL, D_MODEL), lambda i: (0, 0), pipeline_mode=pl.Buffered(1)),
        vec, vec, vec,
        pl.BlockSpec((D_MODEL, ROUTER_COLS), lambda i: (0, 0)),
        pl.BlockSpec((1, ROUTER_COLS), lambda i: (0, 0)),
    ]
    return pl.pallas_call(
        _outproj_kernel,
        grid=grid,
        in_specs=in_specs,
        out_specs=[row_full, pl.BlockSpec((OUT_BM, ROUTER_COLS), lambda i: (i, 0))],
        out_shape=[jax.ShapeDtypeStruct((SEQ, D_MODEL), jnp.float32),
                   jax.ShapeDtypeStruct((SEQ, ROUTER_COLS), jnp.float32)],
        compiler_params=pltpu.CompilerParams(
            dimension_semantics=("arbitrary",),
            vmem_limit_bytes=VMEM_LIMIT_BYTES),
        name="outproj",
    )(ya, yg, x2d, w_out_b, b_out, ln_g, ln_b, w_router, b_router)


def _expert_kernel(be_ref, nused_ref, first_ref, slot_ref, nxt_ref, tok_ref,
                   x1_hbm, wg_hbm, wu_hbm, wd_hbm, out_ref, xbuf, wgb, wub, wdb, gsem, wsem):
    b = pl.program_id(0)
    n_used = nused_ref[0]

    def weight_copies(e, s):
        return (pltpu.make_async_copy(wg_hbm.at[e], wgb.at[s], wsem.at[s, 0]),
                pltpu.make_async_copy(wu_hbm.at[e], wub.at[s], wsem.at[s, 1]),
                pltpu.make_async_copy(wd_hbm.at[e], wdb.at[s], wsem.at[s, 2]))

    def gather_start(blk, s):
        def issue(r, carry):
            t = tok_ref[blk * ROW_BLOCK + r]
            pltpu.make_async_copy(x1_hbm.at[pl.ds(t, 1)], xbuf.at[s, pl.ds(r, 1)], gsem.at[s]).start()
            return carry
        lax.fori_loop(0, ROW_BLOCK, issue, 0)

    def gather_wait(s):
        pltpu.make_async_copy(x1_hbm.at[pl.ds(0, ROW_BLOCK)], xbuf.at[s], gsem.at[s]).wait()

    @pl.when(b == 0)
    def _():
        for n, c in enumerate(weight_copies(be_ref[0], 0)):
            c.start(priority=n % 2)
        gather_start(0, 0)

    @pl.when(b < n_used)
    def _():
        ws = slot_ref[b]
        xs = lax.rem(b, 2)

        @pl.when(first_ref[b] == 1)
        def _():
            @pl.when(nxt_ref[b] >= 0)
            def _():
                for n, c in enumerate(weight_copies(nxt_ref[b], 1 - ws)):
                    c.start(priority=n % 2)
            for c in weight_copies(be_ref[b], ws):
                c.wait()

        @pl.when(b + 1 < n_used)
        def _():
            gather_start(b + 1, 1 - xs)

        gather_wait(xs)
        xb = xbuf[xs]
        g = jnp.dot(xb, wgb[ws], preferred_element_type=jnp.float32)
        u = jnp.dot(xb, wub[ws], preferred_element_type=jnp.float32)
        h = jax.nn.silu(g) * u
        out_ref[...] = jnp.dot(h, wdb[ws], preferred_element_type=jnp.float32)

    @pl.when(b >= n_used)
    def _():
        out_ref[...] = jnp.zeros(out_ref.shape, out_ref.dtype)


def _expert_call(block_expert, n_used, first, slot, nxt, tok_of, x1, w_gate, w_up, w_down):
    any_spec = pl.BlockSpec(memory_space=pl.ANY)
    grid_spec = pltpu.PrefetchScalarGridSpec(
        num_scalar_prefetch=6,
        grid=(N_BLOCKS,),
        in_specs=[any_spec, any_spec, any_spec, any_spec],
        out_specs=pl.BlockSpec((ROW_BLOCK, D_MODEL), lambda b, *_: (b, 0)),
        scratch_shapes=[pltpu.VMEM((2, ROW_BLOCK, D_MODEL), jnp.float32),
                        pltpu.VMEM((2, D_MODEL, D_EXPERT), jnp.float32),
                        pltpu.VMEM((2, D_MODEL, D_EXPERT), jnp.float32),
                        pltpu.VMEM((2, D_EXPERT, D_MODEL), jnp.float32),
                        pltpu.SemaphoreType.DMA((2,)),
                        pltpu.SemaphoreType.DMA((2, 3))],
    )
    return pl.pallas_call(
        _expert_kernel,
        grid_spec=grid_spec,
        out_shape=jax.ShapeDtypeStruct((N_BLOCKS * ROW_BLOCK, D_MODEL), jnp.float32),
        compiler_params=pltpu.CompilerParams(
            dimension_semantics=("arbitrary",),
            vmem_limit_bytes=VMEM_LIMIT_BYTES),
        name="experts",
    )(block_expert, n_used, first, slot, nxt, tok_of, x1, w_gate, w_up, w_down)


def _combine_kernel(pos_ref, ys_hbm, x1_ref, gate_ref, g_ref, b_ref, out_ref, ybuf, sem):
    i = pl.program_id(0)
    n_steps = pl.num_programs(0)
    rows = TOP_K * COMB_BM

    def gather_start(tile, s):
        def issue(r, carry):
            for k in range(TOP_K):
                p = pos_ref[(tile * COMB_BM + r) * TOP_K + k]
                pltpu.make_async_copy(ys_hbm.at[pl.ds(p, 1)],
                                      ybuf.at[s, pl.ds(k * COMB_BM + r, 1)], sem.at[s]).start()
            return carry
        lax.fori_loop(0, COMB_BM, issue, 0)

    @pl.when(i == 0)
    def _():
        gather_start(0, 0)

    s = lax.rem(i, 2)

    @pl.when(i + 1 < n_steps)
    def _():
        gather_start(i + 1, 1 - s)

    pltpu.make_async_copy(ys_hbm.at[pl.ds(0, rows)], ybuf.at[s], sem.at[s]).wait()
    gate = gate_ref[...]
    ffn = ybuf[s, 0:COMB_BM, :] * gate[:, 0:1] + ybuf[s, COMB_BM:, :] * gate[:, 1:2]
    z = ALPHA * x1_ref[...] + ffn
    mu = jnp.mean(z, axis=-1, keepdims=True)
    d = z - mu
    var = jnp.mean(d * d, axis=-1, keepdims=True)
    out_ref[...] = d * lax.rsqrt(var + LN_EPS) * g_ref[...] + b_ref[...]


def _combine_call(pos, ys, x1, gates, ln_g, ln_b):
    row_full = pl.BlockSpec((COMB_BM, D_MODEL), lambda i, pos: (i, 0))
    vec = pl.BlockSpec((1, D_MODEL), lambda i, pos: (0, 0))
    grid_spec = pltpu.PrefetchScalarGridSpec(
        num_scalar_prefetch=1,
        grid=(SEQ // COMB_BM,),
        in_specs=[
            pl.BlockSpec(memory_space=pl.ANY),
            row_full,
            pl.BlockSpec((COMB_BM, TOP_K), lambda i, pos: (i, 0)),
            vec, vec,
        ],
        out_specs=row_full,
        scratch_shapes=[pltpu.VMEM((2, TOP_K * COMB_BM, D_MODEL), jnp.float32),
                        pltpu.SemaphoreType.DMA((2,))],
    )
    return pl.pallas_call(
        _combine_kernel,
        grid_spec=grid_spec,
        out_shape=jax.ShapeDtypeStruct((SEQ, D_MODEL), jnp.float32),
        compiler_params=pltpu.CompilerParams(
            dimension_semantics=("arbitrary",),
            vmem_limit_bytes=VMEM_LIMIT_BYTES),
        name="combine",
    )(pos, ys, x1, gates, ln_g, ln_b)


def _route(logits):
    group_logits = logits[:, :N_GROUPS]
    g_sel = jnp.argmax(group_logits, axis=-1).astype(jnp.int32)
    g_weight = jnp.take_along_axis(jax.nn.softmax(group_logits, axis=-1), g_sel[:, None], axis=1)[:, 0]
    inner_all = logits[:, N_GROUPS:N_GROUPS + N_EXPERTS].reshape(SEQ, N_GROUPS, EXPERTS_PER_GROUP)
    inner = jnp.take_along_axis(inner_all, g_sel[:, None, None], axis=1)[:, 0]
    top_logits, top_idx = lax.top_k(inner, TOP_K)
    gates = g_weight[:, None] * jax.nn.softmax(top_logits, axis=-1)
    expert_id = (g_sel[:, None] * EXPERTS_PER_GROUP + top_idx).reshape(-1).astype(jnp.int32)
    onehot = (expert_id[:, None] == jnp.arange(N_EXPERTS, dtype=jnp.int32)[None, :]).astype(jnp.int32)
    before = jnp.cumsum(onehot, axis=0) - onehot
    rank = jnp.sum(before * onehot, axis=1)
    counts = jnp.sum(onehot, axis=0)
    pcounts = ((counts + ROW_BLOCK - 1) // ROW_BLOCK) * ROW_BLOCK
    pend = jnp.cumsum(pcounts)
    poffs = pend - pcounts
    pos = (jnp.sum(onehot * poffs[None, :], axis=1) + rank).astype(jnp.int32)
    tok = jnp.arange(N_ROWS, dtype=jnp.int32) // TOP_K
    tok_of = jnp.zeros((N_BLOCKS * ROW_BLOCK,), jnp.int32).at[pos].set(tok)
    starts = jnp.arange(N_BLOCKS, dtype=jnp.int32) * ROW_BLOCK
    block_expert = jnp.minimum(jnp.searchsorted(pend, starts, side="right"),
                               N_EXPERTS - 1).astype(jnp.int32)
    n_used = (pend[-1] // ROW_BLOCK).astype(jnp.int32).reshape(1)
    prev = jnp.concatenate([jnp.full((1,), -1, jnp.int32), block_expert[:-1]])
    first = (block_expert != prev).astype(jnp.int32)
    slot = lax.rem(jnp.cumsum(first) - 1, 2).astype(jnp.int32)
    eids = jnp.arange(N_EXPERTS, dtype=jnp.int32)
    cand = jnp.where(counts > 0, eids, N_EXPERTS)
    later = lax.cummin(cand, axis=0, reverse=True)
    nxt_e = jnp.concatenate([later[1:], jnp.full((1,), N_EXPERTS, jnp.int32)])
    nxt_e = jnp.where(nxt_e >= N_EXPERTS, -1, nxt_e).astype(jnp.int32)
    nxt = nxt_e[block_expert]
    return gates, pos, tok_of, block_expert, n_used, first, slot, nxt


def kernel(x, w_in, b_in, conv_w, conv_b, conv_norm_g, conv_norm_b, gmlp_norm_g, gmlp_norm_b,
           w_spatial, b_spatial, w_out, b_out, ln1_g, ln1_b, w_router_group, b_router_group,
           w_router_expert, b_router_expert, w_expert_gate, w_expert_up, w_expert_down,
           ln2_g, ln2_b):
    assert x.shape == (1, SEQ, D_MODEL) and w_in.shape[0] == 1
    x2d = x.reshape(SEQ, D_MODEL)
    xb = x2d.astype(jnp.bfloat16)
    ya, yg = _mixer_call(
        xb, w_in[0].astype(jnp.bfloat16), b_in, conv_w[0], conv_b, conv_norm_g, conv_norm_b,
        gmlp_norm_g, gmlp_norm_b, w_spatial[0], b_spatial[0][:, :, None])

    pad = ROUTER_COLS - N_GROUPS - N_EXPERTS
    w_router = jnp.concatenate(
        [w_router_group[0],
         jnp.transpose(w_router_expert[0], (1, 0, 2)).reshape(D_MODEL, N_EXPERTS),
         jnp.zeros((D_MODEL, pad), jnp.float32)], axis=1)
    b_router = jnp.concatenate(
        [b_router_group[0], b_router_expert[0].reshape(-1), jnp.zeros((pad,), jnp.float32)])[None, :]
    x1, logits = _outproj_call(ya, yg, x2d, w_out[0].astype(jnp.bfloat16), b_out, ln1_g, ln1_b,
                               w_router, b_router)

    gates, pos, tok_of, block_expert, n_used, first, slot, nxt = _route(logits)
    ys = _expert_call(block_expert, n_used, first, slot, nxt, tok_of, x1,
                      w_expert_gate[0], w_expert_up[0], w_expert_down[0])
    out = _combine_call(pos, ys, x1, gates, ln2_g, ln2_b)
    return out.reshape(1, SEQ, D_MODEL)
```

```python
import jax
import jax.numpy as jnp
from jax import lax
from jax.experimental import pallas as pl
from jax.experimental.pallas import tpu as pltpu

D_MODEL = 4096
SEQ = 8192
D_CONV = D_MODEL // 2
D_GMLP = D_MODEL // 2
CONV_WIDTH = 31
GROUP_DIM = 128
GMLP_HEADS = 16
CHUNK = 128
N_GROUPS = 8
EXPERTS_PER_GROUP = 8
N_EXPERTS = N_GROUPS * EXPERTS_PER_GROUP
TOP_K = 2
D_EXPERT = D_MODEL // 8
ROW_BLOCK = 128
LN_EPS = 1e-5
ALPHA = 2.0 ** 0.25

N_ROWS = SEQ * TOP_K
N_BLOCKS = N_ROWS // ROW_BLOCK + N_EXPERTS
ROUTER_COLS = 128
SLAB = D_MODEL // 2 // 128

VMEM_LIMIT_BYTES = 60 * 1024 * 1024

MIX_BM = 1024
MIX_SUB = 256
MIX_BN = 256
HALO = 32
CONV_ROWS = 64

OUT_BM = 128
COMB_BM = 128


def _group_norm(v, g, b):
    outs = []
    for s in range(v.shape[1] // GROUP_DIM):
        blk = v[:, s * GROUP_DIM:(s + 1) * GROUP_DIM]
        mu = jnp.mean(blk, axis=-1, keepdims=True)
        d = blk - mu
        var = jnp.mean(d * d, axis=-1, keepdims=True)
        outs.append(d * lax.rsqrt(var + LN_EPS))
    return jnp.concatenate(outs, axis=-1) * g + b


def _mixer_kernel(x_ref, wa_ref, wg_ref, wu_ref, wv_ref, ba_ref, bg_ref, bu_ref, bv_ref,
                  cw_ref, cb_ref, cng_ref, cnb_ref, gng_ref, gnb_ref, wsp_ref, bsp_ref,
                  ya_ref, yg_ref, xs_ref, r_ref):
    i = pl.program_id(1)
    f32 = jnp.float32
    bf16 = jnp.bfloat16
    w_refs = (wa_ref, wg_ref, wu_ref, wv_ref)
    off = HALO - (CONV_WIDTH - 1)
    win_rows = CONV_ROWS + HALO

    @pl.when(i == 0)
    def _():
        xs_ref[0:HALO, :] = jnp.zeros((HALO, MIX_BN), f32)

    row = lax.broadcasted_iota(jnp.int32, (CHUNK, CHUNK), 0)
    col = lax.broadcasted_iota(jnp.int32, (CHUNK, CHUNK), 1)
    w_sp = [jnp.where(row >= col, wsp_ref[h], 0.0).astype(bf16) for h in range(MIX_BN // GROUP_DIM)]

    def project(k):
        xk = x_ref[k * MIX_SUB:(k + 1) * MIX_SUB, :]
        for q in range(4):
            r_ref[k % 2, q] = jnp.dot(xk, w_refs[q][...], preferred_element_type=f32)

    def finish(k):
        slot = k % 2
        base = k * MIX_SUB
        for c in range(MIX_SUB // CONV_ROWS):
            rows = slice(c * CONV_ROWS, (c + 1) * CONV_ROWS)
            a = r_ref[slot, 0, rows, :] + ba_ref[...]
            gate = r_ref[slot, 1, rows, :] + bg_ref[...]
            lo = HALO + base + c * CONV_ROWS
            xs_ref[lo:lo + CONV_ROWS, :] = a * jax.nn.sigmoid(gate)
        for c in range(MIX_SUB // CONV_ROWS):
            t0 = base + c * CONV_ROWS
            win = xs_ref[t0:t0 + win_rows, :]
            acc = jnp.zeros((CONV_ROWS, MIX_BN), f32) + cb_ref[...]
            for m in range(8):
                taps = [k2 for k2 in range(CONV_WIDTH) if (off + k2) % 8 == m]
                span = 8 * max((off + k2) // 8 for k2 in taps) + CONV_ROWS
                shifted = win[m:m + span, :]
                for k2 in taps:
                    j8 = 8 * ((off + k2) // 8)
                    acc = acc + cw_ref[k2:k2 + 1, :] * shifted[j8:j8 + CONV_ROWS, :]
            y = jax.nn.silu(_group_norm(acc, cng_ref[...], cnb_ref[...]))
            ya_ref[t0:t0 + CONV_ROWS, :] = y.astype(ya_ref.dtype)
        n_chunks = MIX_SUB // CHUNK
        us, vs = [], []
        for c in range(n_chunks):
            rows = slice(c * CHUNK, (c + 1) * CHUNK)
            us.append(jax.nn.gelu(r_ref[slot, 2, rows, :] + bu_ref[...]))
            v = jax.nn.gelu(r_ref[slot, 3, rows, :] + bv_ref[...])
            vs.append(_group_norm(v, gng_ref[...], gnb_ref[...]).astype(bf16))
        for h in range(MIX_BN // GROUP_DIM):
            lanes = slice(h * GROUP_DIM, (h + 1) * GROUP_DIM)
            v_h = jnp.concatenate([v[:, lanes] for v in vs], axis=1)
            s_h = jnp.dot(w_sp[h], v_h, preferred_element_type=f32) + bsp_ref[h]
            for c in range(n_chunks):
                g = us[c][:, lanes] * s_h[:, c * GROUP_DIM:(c + 1) * GROUP_DIM]
                yg_ref[base + c * CHUNK:base + (c + 1) * CHUNK, lanes] = g.astype(yg_ref.dtype)

    n_sub = MIX_BM // MIX_SUB
    project(0)
    for k in range(1, n_sub):
        project(k)
        finish(k - 1)
    finish(n_sub - 1)
    xs_ref[0:HALO, :] = xs_ref[MIX_BM:MIX_BM + HALO, :]


def _mixer_call(xb, w_in_b, b_in, conv_w, conv_b, cng, cnb, gng, gnb, w_spatial, b_spatial_col):
    nq = D_CONV // MIX_BN
    grid = (nq, SEQ // MIX_BM)

    def wspec(q):
        return pl.BlockSpec((D_MODEL, MIX_BN), lambda j, i, q=q: (0, q * nq + j))

    def bspec(q):
        return pl.BlockSpec((1, MIX_BN), lambda j, i, q=q: (0, q * nq + j))

    vec = pl.BlockSpec((1, MIX_BN), lambda j, i: (0, j))
    heads_per_step = MIX_BN // GROUP_DIM
    in_specs = [
        pl.BlockSpec((MIX_BM, D_MODEL), lambda j, i: (i, 0)),
        wspec(0), wspec(1), wspec(2), wspec(3),
        bspec(0), bspec(1), bspec(2), bspec(3),
        pl.BlockSpec((CONV_WIDTH, MIX_BN), lambda j, i: (0, j)),
        vec, vec, vec, vec, vec,
        pl.BlockSpec((heads_per_step, CHUNK, CHUNK), lambda j, i: (j, 0, 0)),
        pl.BlockSpec((heads_per_step, CHUNK, 1), lambda j, i: (j, 0, 0)),
    ]
    out_spec = pl.BlockSpec((MIX_BM, MIX_BN), lambda j, i: (i, j))
    return pl.pallas_call(
        _mixer_kernel,
        grid=grid,
        in_specs=in_specs,
        out_specs=[out_spec, out_spec],
        out_shape=[jax.ShapeDtypeStruct((SEQ, D_CONV), jnp.bfloat16),
                   jax.ShapeDtypeStruct((SEQ, D_GMLP), jnp.bfloat16)],
        scratch_shapes=[pltpu.VMEM((HALO + MIX_BM, MIX_BN), jnp.float32),
                        pltpu.VMEM((2, 4, MIX_SUB, MIX_BN), jnp.float32)],
        compiler_params=pltpu.CompilerParams(
            dimension_semantics=("arbitrary", "arbitrary"),
            vmem_limit_bytes=VMEM_LIMIT_BYTES),
        name="mixer",
    )(xb, w_in_b, w_in_b, w_in_b, w_in_b, b_in, b_in, b_in, b_in,
      conv_w, conv_b, cng, cnb, gng, gnb, w_spatial, b_spatial_col)


def _store_packed_rows(dst_ref, x):
    f32, u32 = jnp.float32, jnp.uint32
    half = D_MODEL // 2
    hi = lax.bitcast_convert_type(x[:, :half].astype(jnp.bfloat16).astype(f32), u32)
    lo = lax.bitcast_convert_type(x[:, half:].astype(jnp.bfloat16).astype(f32), u32)
    words = hi | (lo >> 16)
    for j in range(SLAB):
        dst_ref[pl.ds(j, x.shape[0], stride=SLAB), :] = words[:, j * 128:(j + 1) * 128]


def _load_packed_rows(src_ref, rows):
    f32, u32 = jnp.float32, jnp.uint32
    his, los = [], []
    for j in range(SLAB):
        w = src_ref[pl.ds(j, rows, stride=SLAB), :]
        his.append(lax.bitcast_convert_type(w & u32(0xFFFF0000), f32))
        los.append(lax.bitcast_convert_type(w << 16, f32))
    return jnp.concatenate(his + los, axis=1)


def _outproj_kernel(ya_ref, yg_ref, x_ref, w_ref, bo_ref, g_ref, b_ref, wr_ref, br_ref,
                    x1_ref, x1p_ref, logit_ref):
    f32 = jnp.float32
    z = jnp.dot(ya_ref[...], w_ref[0:D_CONV, :], preferred_element_type=f32)
    z = z + jnp.dot(yg_ref[...], w_ref[D_CONV:, :], preferred_element_type=f32)
    z = z + bo_ref[...] + ALPHA * x_ref[...]
    mu = jnp.mean(z, axis=-1, keepdims=True)
    d = z - mu
    var = jnp.mean(d * d, axis=-1, keepdims=True)
    x1 = d * lax.rsqrt(var + LN_EPS) * g_ref[...] + b_ref[...]
    x1_ref[...] = x1
    _store_packed_rows(x1p_ref, x1)
    wr = wr_ref[...]
    wr_hi = wr.astype(jnp.bfloat16)
    wr_lo = (wr - wr_hi.astype(f32)).astype(jnp.bfloat16)
    x_hi = x1.astype(jnp.bfloat16)
    x_lo = (x1 - x_hi.astype(f32)).astype(jnp.bfloat16)
    logits = jnp.dot(x_hi, wr_hi, preferred_element_type=f32)
    logits = logits + (jnp.dot(x_lo, wr_hi, preferred_element_type=f32)
                       + jnp.dot(x_hi, wr_lo, preferred_element_type=f32))
    logit_ref[...] = logits + br_ref[...]


def _outproj_call(ya, yg, x2d, w_out_b, b_out, ln_g, ln_b, w_router, b_router):
    grid = (SEQ // OUT_BM,)
    row_half = pl.BlockSpec((OUT_BM, D_CONV), lambda i: (i, 0))
    row_full = pl.BlockSpec((OUT_BM, D_MODEL), lambda i: (i, 0))
    vec = pl.BlockSpec((1, D_MODEL), lambda i: (0, 0))
    in_specs = [
        row_half, row_half, row_full,
        pl.BlockSpec((D_MODEL, D_MODEL), lambda i: (0, 0), pipeline_mode=pl.Buffered(1)),
        vec, vec, vec,
        pl.BlockSpec((D_MODEL, ROUTER_COLS), lambda i: (0, 0)),
        pl.BlockSpec((1, ROUTER_COLS), lambda i: (0, 0)),
    ]
    return pl.pallas_call(
        _outproj_kernel,
        grid=grid,
        in_specs=in_specs,
        out_specs=[row_full,
                   pl.BlockSpec((OUT_BM * SLAB, 128), lambda i: (i, 0)),
                   pl.BlockSpec((OUT_BM, ROUTER_COLS), lambda i: (i, 0))],
        out_shape=[jax.ShapeDtypeStruct((SEQ, D_MODEL), jnp.float32),
                   jax.ShapeDtypeStruct((SEQ * SLAB, 128), jnp.uint32),
                   jax.ShapeDtypeStruct((SEQ, ROUTER_COLS), jnp.float32)],
        compiler_params=pltpu.CompilerParams(
            dimension_semantics=("arbitrary",),
            vmem_limit_bytes=VMEM_LIMIT_BYTES),
        name="outproj",
    )(ya, yg, x2d, w_out_b, b_out, ln_g, ln_b, w_router, b_router)


def _expert_kernel(be_ref, nused_ref, first_ref, slot_ref, nxt_ref, tok_ref,
                   x1p_hbm, wg_hbm, wu_hbm, wd_hbm, out_ref, xbuf, wgb, wub, wdb, gsem, wsem):
    b = pl.program_id(0)
    n_used = nused_ref[0]

    def weight_copies(e, s):
        return (pltpu.make_async_copy(wg_hbm.at[e], wgb.at[s], wsem.at[s, 0]),
                pltpu.make_async_copy(wu_hbm.at[e], wub.at[s], wsem.at[s, 1]),
                pltpu.make_async_copy(wd_hbm.at[e], wdb.at[s], wsem.at[s, 2]))

    def gather_start(blk, s):
        def issue(r, carry):
            src = pl.multiple_of(tok_ref[blk * ROW_BLOCK + r] * SLAB, SLAB)
            dst = pl.multiple_of(r * SLAB, SLAB)
            pltpu.make_async_copy(x1p_hbm.at[pl.ds(src, SLAB)], xbuf.at[s, pl.ds(dst, SLAB)],
                                  gsem.at[s]).start()
            return carry
        lax.fori_loop(0, ROW_BLOCK, issue, 0, unroll=8)

    def gather_wait(s):
        pltpu.make_async_copy(x1p_hbm.at[pl.ds(0, ROW_BLOCK * SLAB)], xbuf.at[s], gsem.at[s]).wait()

    @pl.when(b == 0)
    def _():
        for c in weight_copies(be_ref[0], 0):
            c.start(priority=1)
        gather_start(0, 0)

    @pl.when(b < n_used)
    def _():
        ws = slot_ref[b]
        xs = lax.rem(b, 2)

        @pl.when(first_ref[b] == 1)
        def _():
            @pl.when(nxt_ref[b] >= 0)
            def _():
                for c in weight_copies(nxt_ref[b], 1 - ws):
                    c.start(priority=1)
            for c in weight_copies(be_ref[b], ws):
                c.wait()

        @pl.when(b + 1 < n_used)
        def _():
            gather_start(b + 1, 1 - xs)

        gather_wait(xs)
        xb = _load_packed_rows(xbuf.at[xs], ROW_BLOCK).astype(jnp.bfloat16)
        g = jnp.dot(xb, wgb[ws], preferred_element_type=jnp.float32)
        u = jnp.dot(xb, wub[ws], preferred_element_type=jnp.float32)
        h = jax.nn.silu(g) * u
        _store_packed_rows(out_ref, jnp.dot(h, wdb[ws], preferred_element_type=jnp.float32))

    @pl.when(b >= n_used)
    def _():
        out_ref[...] = jnp.zeros(out_ref.shape, out_ref.dtype)


def _expert_call(block_expert, n_used, first, slot, nxt, tok_of, x1p, w_gate, w_up, w_down):
    any_spec = pl.BlockSpec(memory_space=pl.ANY)
    grid_spec = pltpu.PrefetchScalarGridSpec(
        num_scalar_prefetch=6,
        grid=(N_BLOCKS,),
        in_specs=[any_spec, any_spec, any_spec, any_spec],
        out_specs=pl.BlockSpec((ROW_BLOCK * SLAB, 128), lambda b, *_: (b, 0)),
        scratch_shapes=[pltpu.VMEM((2, ROW_BLOCK * SLAB, 128), jnp.uint32),
                        pltpu.VMEM((2, D_MODEL, D_EXPERT), jnp.float32),
                        pltpu.VMEM((2, D_MODEL, D_EXPERT), jnp.float32),
                        pltpu.VMEM((2, D_EXPERT, D_MODEL), jnp.float32),
                        pltpu.SemaphoreType.DMA((2,)),
                        pltpu.SemaphoreType.DMA((2, 3))],
    )
    return pl.pallas_call(
        _expert_kernel,
        grid_spec=grid_spec,
        out_shape=jax.ShapeDtypeStruct((N_BLOCKS * ROW_BLOCK * SLAB, 128), jnp.uint32),
        compiler_params=pltpu.CompilerParams(
            dimension_semantics=("arbitrary",),
            vmem_limit_bytes=VMEM_LIMIT_BYTES),
        name="experts",
    )(block_expert, n_used, first, slot, nxt, tok_of, x1p, w_gate, w_up, w_down)


def _combine_kernel(pos_ref, ys_hbm, x1_ref, gate_ref, g_ref, b_ref, out_ref, ybuf, sem):
    i = pl.program_id(0)
    n_steps = pl.num_programs(0)
    rows = TOP_K * COMB_BM

    def gather_start(tile, s):
        def issue(r, carry):
            for k in range(TOP_K):
                src = pl.multiple_of(pos_ref[(tile * COMB_BM + r) * TOP_K + k] * SLAB, SLAB)
                dst = pl.multiple_of((k * COMB_BM + r) * SLAB, SLAB)
                pltpu.make_async_copy(ys_hbm.at[pl.ds(src, SLAB)], ybuf.at[s, pl.ds(dst, SLAB)],
                                      sem.at[s]).start()
            return carry
        lax.fori_loop(0, COMB_BM, issue, 0, unroll=4)

    @pl.when(i == 0)
    def _():
        gather_start(0, 0)

    s = lax.rem(i, 2)

    @pl.when(i + 1 < n_steps)
    def _():
        gather_start(i + 1, 1 - s)

    pltpu.make_async_copy(ys_hbm.at[pl.ds(0, rows * SLAB)], ybuf.at[s], sem.at[s]).wait()
    gate = gate_ref[...]
    y0 = _load_packed_rows(ybuf.at[s, pl.ds(0, COMB_BM * SLAB)], COMB_BM)
    y1 = _load_packed_rows(ybuf.at[s, pl.ds(COMB_BM * SLAB, COMB_BM * SLAB)], COMB_BM)
    ffn = y0 * gate[:, 0:1] + y1 * gate[:, 1:2]
    z = ALPHA * x1_ref[...] + ffn
    mu = jnp.mean(z, axis=-1, keepdims=True)
    d = z - mu
    var = jnp.mean(d * d, axis=-1, keepdims=True)
    out_ref[...] = d * lax.rsqrt(var + LN_EPS) * g_ref[...] + b_ref[...]


def _combine_call(pos, ys, x1, gates, ln_g, ln_b):
    row_full = pl.BlockSpec((COMB_BM, D_MODEL), lambda i, pos: (i, 0))
    vec = pl.BlockSpec((1, D_MODEL), lambda i, pos: (0, 0))
    grid_spec = pltpu.PrefetchScalarGridSpec(
        num_scalar_prefetch=1,
        grid=(SEQ // COMB_BM,),
        in_specs=[
            pl.BlockSpec(memory_space=pl.ANY),
            row_full,
            pl.BlockSpec((COMB_BM, TOP_K), lambda i, pos: (i, 0)),
            vec, vec,
        ],
        out_specs=row_full,
        scratch_shapes=[pltpu.VMEM((2, TOP_K * COMB_BM * SLAB, 128), jnp.uint32),
                        pltpu.SemaphoreType.DMA((2,))],
    )
    return pl.pallas_call(
        _combine_kernel,
        grid_spec=grid_spec,
        out_shape=jax.ShapeDtypeStruct((SEQ, D_MODEL), jnp.float32),
        compiler_params=pltpu.CompilerParams(
            dimension_semantics=("arbitrary",),
            vmem_limit_bytes=VMEM_LIMIT_BYTES),
        name="combine",
    )(pos, ys, x1, gates, ln_g, ln_b)


def _route(logits):
    group_logits = logits[:, :N_GROUPS]
    g_sel = jnp.argmax(group_logits, axis=-1).astype(jnp.int32)
    g_weight = jnp.take_along_axis(jax.nn.softmax(group_logits, axis=-1), g_sel[:, None], axis=1)[:, 0]
    inner_all = logits[:, N_GROUPS:N_GROUPS + N_EXPERTS].reshape(SEQ, N_GROUPS, EXPERTS_PER_GROUP)
    inner = jnp.take_along_axis(inner_all, g_sel[:, None, None], axis=1)[:, 0]
    top_logits, top_idx = lax.top_k(inner, TOP_K)
    gates = g_weight[:, None] * jax.nn.softmax(top_logits, axis=-1)
    expert_id = (g_sel[:, None] * EXPERTS_PER_GROUP + top_idx).reshape(-1).astype(jnp.int32)
    onehot = (expert_id[:, None] == jnp.arange(N_EXPERTS, dtype=jnp.int32)[None, :]).astype(jnp.int32)
    before = jnp.cumsum(onehot, axis=0) - onehot
    rank = jnp.sum(before * onehot, axis=1)
    counts = jnp.sum(onehot, axis=0)
    pcounts = ((counts + ROW_BLOCK - 1) // ROW_BLOCK) * ROW_BLOCK
    pend = jnp.cumsum(pcounts)
    poffs = pend - pcounts
    pos = (jnp.sum(onehot * poffs[None, :], axis=1) + rank).astype(jnp.int32)
    tok = jnp.arange(N_ROWS, dtype=jnp.int32) // TOP_K
    tok_of = jnp.zeros((N_BLOCKS * ROW_BLOCK,), jnp.int32).at[pos].set(tok)
    starts = jnp.arange(N_BLOCKS, dtype=jnp.int32) * ROW_BLOCK
    block_expert = jnp.minimum(jnp.searchsorted(pend, starts, side="right"),
                               N_EXPERTS - 1).astype(jnp.int32)
    n_used = (pend[-1] // ROW_BLOCK).astype(jnp.int32).reshape(1)
    prev = jnp.concatenate([jnp.full((1,), -1, jnp.int32), block_expert[:-1]])
    first = (block_expert != prev).astype(jnp.int32)
    slot = lax.rem(jnp.cumsum(first) - 1, 2).astype(jnp.int32)
    eids = jnp.arange(N_EXPERTS, dtype=jnp.int32)
    cand = jnp.where(counts > 0, eids, N_EXPERTS)
    later = lax.cummin(cand, axis=0, reverse=True)
    nxt_e = jnp.concatenate([later[1:], jnp.full((1,), N_EXPERTS, jnp.int32)])
    nxt_e = jnp.where(nxt_e >= N_EXPERTS, -1, nxt_e).astype(jnp.int32)
    nxt = nxt_e[block_expert]
    return gates, pos, tok_of, block_expert, n_used, first, slot, nxt


def kernel(x, w_in, b_in, conv_w, conv_b, conv_norm_g, conv_norm_b, gmlp_norm_g, gmlp_norm_b,
           w_spatial, b_spatial, w_out, b_out, ln1_g, ln1_b, w_router_group, b_router_group,
           w_router_expert, b_router_expert, w_expert_gate, w_expert_up, w_expert_down,
           ln2_g, ln2_b):
    assert x.shape == (1, SEQ, D_MODEL) and w_in.shape[0] == 1
    x2d = x.reshape(SEQ, D_MODEL)
    xb = x2d.astype(jnp.bfloat16)
    ya, yg = _mixer_call(
        xb, w_in[0].astype(jnp.bfloat16), b_in, conv_w[0], conv_b, conv_norm_g, conv_norm_b,
        gmlp_norm_g, gmlp_norm_b, w_spatial[0], b_spatial[0][:, :, None])

    pad = ROUTER_COLS - N_GROUPS - N_EXPERTS
    w_router = jnp.concatenate(
        [w_router_group[0],
         jnp.transpose(w_router_expert[0], (1, 0, 2)).reshape(D_MODEL, N_EXPERTS),
         jnp.zeros((D_MODEL, pad), jnp.float32)], axis=1)
    b_router = jnp.concatenate(
        [b_router_group[0], b_router_expert[0].reshape(-1), jnp.zeros((pad,), jnp.float32)])[None, :]
    x1, x1p, logits = _outproj_call(ya, yg, x2d, w_out[0].astype(jnp.bfloat16), b_out, ln1_g, ln1_b,
                                    w_router, b_router)

    gates, pos, tok_of, block_expert, n_used, first, slot, nxt = _route(logits)
    ys = _expert_call(block_expert, n_used, first, slot, nxt, tok_of, x1p,
                      w_expert_gate[0], w_expert_up[0], w_expert_down[0])
    out = _combine_call(pos, ys, x1, gates, ln2_g, ln2_b)
    return out.reshape(1, SEQ, D_MODEL)
```

```python
import jax
import jax.numpy as jnp
from jax import lax
from jax.experimental import pallas as pl
from jax.experimental.pallas import tpu as pltpu

D_MODEL = 4096
SEQ = 8192
D_CONV = D_MODEL // 2
D_GMLP = D_MODEL // 2
CONV_WIDTH = 31
GROUP_DIM = 128
GMLP_HEADS = 16
CHUNK = 128
N_GROUPS = 8
EXPERTS_PER_GROUP = 8
N_EXPERTS = N_GROUPS * EXPERTS_PER_GROUP
TOP_K = 2
D_EXPERT = D_MODEL // 8
ROW_BLOCK = 128
LN_EPS = 1e-5
ALPHA = 2.0 ** 0.25

N_ROWS = SEQ * TOP_K
N_BLOCKS = N_ROWS // ROW_BLOCK + N_EXPERTS
ROUTER_COLS = 128
SLAB = D_MODEL // 2 // 128

VMEM_LIMIT_BYTES = 60 * 1024 * 1024

MIX_BM = 1024
MIX_SUB = 256
MIX_BN = 256
HALO = 32
CONV_ROWS = 128
SHIFT_ROWS = CONV_ROWS + HALO - 8

OUT_BM = 128
COMB_BM = 128


def _group_norm(v, g, b):
    outs = []
    for s in range(v.shape[1] // GROUP_DIM):
        blk = v[:, s * GROUP_DIM:(s + 1) * GROUP_DIM]
        mu = jnp.mean(blk, axis=-1, keepdims=True)
        d = blk - mu
        var = jnp.mean(d * d, axis=-1, keepdims=True)
        outs.append(d * lax.rsqrt(var + LN_EPS))
    return jnp.concatenate(outs, axis=-1) * g + b


def _mixer_kernel(x_ref, wa_ref, wg_ref, wu_ref, wv_ref, ba_ref, bg_ref, bu_ref, bv_ref,
                  cw_ref, cb_ref, cng_ref, cnb_ref, gng_ref, gnb_ref, wsp_ref, bsp_ref,
                  ya_ref, yg_ref, xs_ref, r_ref, sh_ref):
    i = pl.program_id(1)
    f32 = jnp.float32
    bf16 = jnp.bfloat16
    w_refs = (wa_ref, wg_ref, wu_ref, wv_ref)
    off = HALO - (CONV_WIDTH - 1)

    @pl.when(i == 0)
    def _():
        xs_ref[0:HALO, :] = jnp.zeros((HALO, MIX_BN), f32)

    row = lax.broadcasted_iota(jnp.int32, (CHUNK, CHUNK), 0)
    col = lax.broadcasted_iota(jnp.int32, (CHUNK, CHUNK), 1)
    w_sp = [jnp.where(row >= col, wsp_ref[h], 0.0).astype(bf16) for h in range(MIX_BN // GROUP_DIM)]

    def project(k, q):
        rows = slice(k * MIX_SUB, (k + 1) * MIX_SUB)
        r_ref[k % 2, q] = jnp.dot(x_ref[rows, :], w_refs[q][...], preferred_element_type=f32)

    def finish_conv(k, c):
        t0 = k * MIX_SUB + c * CONV_ROWS
        rows = slice(c * CONV_ROWS, (c + 1) * CONV_ROWS)
        a = r_ref[k % 2, 0, rows, :] + ba_ref[...]
        gate = r_ref[k % 2, 1, rows, :] + bg_ref[...]
        xs_ref[HALO + t0:HALO + t0 + CONV_ROWS, :] = a * jax.nn.sigmoid(gate)
        sh = sh_ref.at[c % 2]
        for m in range(1, 8):
            sh[m - 1] = xs_ref[t0 + m:t0 + m + SHIFT_ROWS, :]
        acc = jnp.zeros((CONV_ROWS, MIX_BN), f32) + cb_ref[...]
        for k2 in range(CONV_WIDTH):
            m, j8 = (off + k2) % 8, 8 * ((off + k2) // 8)
            if m == 0:
                src = xs_ref[t0 + j8:t0 + j8 + CONV_ROWS, :]
            else:
                src = sh[m - 1, j8:j8 + CONV_ROWS, :]
            acc = acc + cw_ref[k2:k2 + 1, :] * src
        y = jax.nn.silu(_group_norm(acc, cng_ref[...], cnb_ref[...]))
        ya_ref[t0:t0 + CONV_ROWS, :] = y.astype(ya_ref.dtype)

    def finish_gmlp(k):
        base = k * MIX_SUB
        n_chunks = MIX_SUB // CHUNK
        us, vs = [], []
        for c in range(n_chunks):
            rows = slice(c * CHUNK, (c + 1) * CHUNK)
            us.append(jax.nn.gelu(r_ref[k % 2, 2, rows, :] + bu_ref[...]))
            v = jax.nn.gelu(r_ref[k % 2, 3, rows, :] + bv_ref[...])
            vs.append(_group_norm(v, gng_ref[...], gnb_ref[...]).astype(bf16))
        for h in range(MIX_BN // GROUP_DIM):
            lanes = slice(h * GROUP_DIM, (h + 1) * GROUP_DIM)
            v_h = jnp.concatenate([v[:, lanes] for v in vs], axis=1)
            s_h = jnp.dot(w_sp[h], v_h, preferred_element_type=f32) + bsp_ref[h]
            for c in range(n_chunks):
                g = us[c][:, lanes] * s_h[:, c * GROUP_DIM:(c + 1) * GROUP_DIM]
                yg_ref[base + c * CHUNK:base + (c + 1) * CHUNK, lanes] = g.astype(yg_ref.dtype)

    def finish(k):
        for c in range(MIX_SUB // CONV_ROWS):
            finish_conv(k, c)
        finish_gmlp(k)

    n_sub = MIX_BM // MIX_SUB
    for k in range(n_sub + 1):
        if k < n_sub:
            for q in range(4):
                project(k, q)
        if k > 0:
            finish(k - 1)
    xs_ref[0:HALO, :] = xs_ref[MIX_BM:MIX_BM + HALO, :]


def _mixer_call(xb, w_in_b, b_in, conv_w, conv_b, cng, cnb, gng, gnb, w_spatial, b_spatial_col):
    nq = D_CONV // MIX_BN
    grid = (nq, SEQ // MIX_BM)

    def wspec(q):
        return pl.BlockSpec((D_MODEL, MIX_BN), lambda j, i, q=q: (0, q * nq + j))

    def bspec(q):
        return pl.BlockSpec((1, MIX_BN), lambda j, i, q=q: (0, q * nq + j))

    vec = pl.BlockSpec((1, MIX_BN), lambda j, i: (0, j))
    heads_per_step = MIX_BN // GROUP_DIM
    in_specs = [
        pl.BlockSpec((MIX_BM, D_MODEL), lambda j, i: (i, 0)),
        wspec(0), wspec(1), wspec(2), wspec(3),
        bspec(0), bspec(1), bspec(2), bspec(3),
        pl.BlockSpec((CONV_WIDTH, MIX_BN), lambda j, i: (0, j)),
        vec, vec, vec, vec, vec,
        pl.BlockSpec((heads_per_step, CHUNK, CHUNK), lambda j, i: (j, 0, 0)),
        pl.BlockSpec((heads_per_step, CHUNK, 1), lambda j, i: (j, 0, 0)),
    ]
    out_spec = pl.BlockSpec((MIX_BM, MIX_BN), lambda j, i: (i, j))
    return pl.pallas_call(
        _mixer_kernel,
        grid=grid,
        in_specs=in_specs,
        out_specs=[out_spec, out_spec],
        out_shape=[jax.ShapeDtypeStruct((SEQ, D_CONV), jnp.bfloat16),
                   jax.ShapeDtypeStruct((SEQ, D_GMLP), jnp.bfloat16)],
        scratch_shapes=[pltpu.VMEM((HALO + MIX_BM, MIX_BN), jnp.float32),
                        pltpu.VMEM((2, 4, MIX_SUB, MIX_BN), jnp.float32),
                        pltpu.VMEM((2, 7, SHIFT_ROWS, MIX_BN), jnp.float32)],
        compiler_params=pltpu.CompilerParams(
            dimension_semantics=("arbitrary", "arbitrary"),
            vmem_limit_bytes=VMEM_LIMIT_BYTES),
        name="mixer",
    )(xb, w_in_b, w_in_b, w_in_b, w_in_b, b_in, b_in, b_in, b_in,
      conv_w, conv_b, cng, cnb, gng, gnb, w_spatial, b_spatial_col)


def _store_packed_rows(dst_ref, x):
    f32, u32 = jnp.float32, jnp.uint32
    half = D_MODEL // 2
    hi = lax.bitcast_convert_type(x[:, :half].astype(jnp.bfloat16).astype(f32), u32)
    lo = lax.bitcast_convert_type(x[:, half:].astype(jnp.bfloat16).astype(f32), u32)
    words = hi | (lo >> 16)
    for j in range(SLAB):
        dst_ref[pl.ds(j, x.shape[0], stride=SLAB), :] = words[:, j * 128:(j + 1) * 128]


def _load_packed_rows(src_ref, rows):
    f32, u32 = jnp.float32, jnp.uint32
    his, los = [], []
    for j in range(SLAB):
        w = src_ref[pl.ds(j, rows, stride=SLAB), :]
        his.append(lax.bitcast_convert_type(w & u32(0xFFFF0000), f32))
        los.append(lax.bitcast_convert_type(w << 16, f32))
    return jnp.concatenate(his + los, axis=1)


def _outproj_kernel(ya_ref, yg_ref, x_ref, w_ref, bo_ref, g_ref, b_ref, wr_ref, br_ref,
                    x1_ref, x1p_ref, logit_ref):
    f32 = jnp.float32
    z = jnp.dot(ya_ref[...], w_ref[0:D_CONV, :], preferred_element_type=f32)
    z = z + jnp.dot(yg_ref[...], w_ref[D_CONV:, :], preferred_element_type=f32)
    z = z + bo_ref[...] + ALPHA * x_ref[...]
    mu = jnp.mean(z, axis=-1, keepdims=True)
    d = z - mu
    var = jnp.mean(d * d, axis=-1, keepdims=True)
    x1 = d * lax.rsqrt(var + LN_EPS) * g_ref[...] + b_ref[...]
    x1_ref[...] = x1
    _store_packed_rows(x1p_ref, x1)
    wr = wr_ref[...]
    wr_hi = wr.astype(jnp.bfloat16)
    wr_lo = (wr - wr_hi.astype(f32)).astype(jnp.bfloat16)
    x_hi = x1.astype(jnp.bfloat16)
    x_lo = (x1 - x_hi.astype(f32)).astype(jnp.bfloat16)
    logits = jnp.dot(x_hi, wr_hi, preferred_element_type=f32)
    logits = logits + (jnp.dot(x_lo, wr_hi, preferred_element_type=f32)
                       + jnp.dot(x_hi, wr_lo, preferred_element_type=f32))
    logit_ref[...] = logits + br_ref[...]


def _outproj_call(ya, yg, x2d, w_out_b, b_out, ln_g, ln_b, w_router, b_router):
    grid = (SEQ // OUT_BM,)
    row_half = pl.BlockSpec((OUT_BM, D_CONV), lambda i: (i, 0))
    row_full = pl.BlockSpec((OUT_BM, D_MODEL), lambda i: (i, 0))
    vec = pl.BlockSpec((1, D_MODEL), lambda i: (0, 0))
    in_specs = [
        row_half, row_half, row_full,
        pl.BlockSpec((D_MODEL, D_MODEL), lambda i: (0, 0), pipeline_mode=pl.Buffered(1)),
        vec, vec, vec,
        pl.BlockSpec((D_MODEL, ROUTER_COLS), lambda i: (0, 0)),
        pl.BlockSpec((1, ROUTER_COLS), lambda i: (0, 0)),
    ]
    return pl.pallas_call(
        _outproj_kernel,
        grid=grid,
        in_specs=in_specs,
        out_specs=[row_full,
                   pl.BlockSpec((OUT_BM * SLAB, 128), lambda i: (i, 0)),
                   pl.BlockSpec((OUT_BM, ROUTER_COLS), lambda i: (i, 0))],
        out_shape=[jax.ShapeDtypeStruct((SEQ, D_MODEL), jnp.float32),
                   jax.ShapeDtypeStruct((SEQ * SLAB, 128), jnp.uint32),
                   jax.ShapeDtypeStruct((SEQ, ROUTER_COLS), jnp.float32)],
        compiler_params=pltpu.CompilerParams(
            dimension_semantics=("arbitrary",),
            vmem_limit_bytes=VMEM_LIMIT_BYTES),
        name="outproj",
    )(ya, yg, x2d, w_out_b, b_out, ln_g, ln_b, w_router, b_router)


def _expert_kernel(be_ref, nused_ref, first_ref, slot_ref, nxt_ref, tok_ref,
                   x1p_hbm, wg_hbm, wu_hbm, wd_hbm, out_ref, xbuf, wgb, wub, wdb, gsem, wsem):
    b = pl.program_id(0)
    n_used = nused_ref[0]

    def weight_copies(e, s):
        return (pltpu.make_async_copy(wg_hbm.at[e], wgb.at[s], wsem.at[s, 0]),
                pltpu.make_async_copy(wu_hbm.at[e], wub.at[s], wsem.at[s, 1]),
                pltpu.make_async_copy(wd_hbm.at[e], wdb.at[s], wsem.at[s, 2]))

    def gather_start(blk, s):
        def issue(r, carry):
            src = pl.multiple_of(tok_ref[blk * ROW_BLOCK + r] * SLAB, SLAB)
            dst = pl.multiple_of(r * SLAB, SLAB)
            pltpu.make_async_copy(x1p_hbm.at[pl.ds(src, SLAB)], xbuf.at[s, pl.ds(dst, SLAB)],
                                  gsem.at[s]).start()
            return carry
        lax.fori_loop(0, ROW_BLOCK, issue, 0, unroll=8)

    def gather_wait(s):
        pltpu.make_async_copy(x1p_hbm.at[pl.ds(0, ROW_BLOCK * SLAB)], xbuf.at[s], gsem.at[s]).wait()

    @pl.when(b == 0)
    def _():
        for c in weight_copies(be_ref[0], 0):
            c.start(priority=1)
        gather_start(0, 0)

    @pl.when(b < n_used)
    def _():
        ws = slot_ref[b]
        xs = lax.rem(b, 2)

        @pl.when(first_ref[b] == 1)
        def _():
            @pl.when(nxt_ref[b] >= 0)
            def _():
                for c in weight_copies(nxt_ref[b], 1 - ws):
                    c.start(priority=1)
            for c in weight_copies(be_ref[b], ws):
                c.wait()

        @pl.when(b + 1 < n_used)
        def _():
            gather_start(b + 1, 1 - xs)

        gather_wait(xs)
        xb = _load_packed_rows(xbuf.at[xs], ROW_BLOCK).astype(jnp.bfloat16)
        g = jnp.dot(xb, wgb[ws], preferred_element_type=jnp.float32)
        u = jnp.dot(xb, wub[ws], preferred_element_type=jnp.float32)
        h = jax.nn.silu(g) * u
        _store_packed_rows(out_ref, jnp.dot(h, wdb[ws], preferred_element_type=jnp.float32))

    @pl.when(b >= n_used)
    def _():
        out_ref[...] = jnp.zeros(out_ref.shape, out_ref.dtype)


def _expert_call(block_expert, n_used, first, slot, nxt, tok_of, x1p, w_gate, w_up, w_down):
    any_spec = pl.BlockSpec(memory_space=pl.ANY)
    grid_spec = pltpu.PrefetchScalarGridSpec(
        num_scalar_prefetch=6,
        grid=(N_BLOCKS,),
        in_specs=[any_spec, any_spec, any_spec, any_spec],
        out_specs=pl.BlockSpec((ROW_BLOCK * SLAB, 128), lambda b, *_: (b, 0)),
        scratch_shapes=[pltpu.VMEM((2, ROW_BLOCK * SLAB, 128), jnp.uint32),
                        pltpu.VMEM((2, D_MODEL, D_EXPERT), jnp.float32),
                        pltpu.VMEM((2, D_MODEL, D_EXPERT), jnp.float32),
                        pltpu.VMEM((2, D_EXPERT, D_MODEL), jnp.float32),
                        pltpu.SemaphoreType.DMA((2,)),
                        pltpu.SemaphoreType.DMA((2, 3))],
    )
    return pl.pallas_call(
        _expert_kernel,
        grid_spec=grid_spec,
        out_shape=jax.ShapeDtypeStruct((N_BLOCKS * ROW_BLOCK * SLAB, 128), jnp.uint32),
        compiler_params=pltpu.CompilerParams(
            dimension_semantics=("arbitrary",),
            vmem_limit_bytes=VMEM_LIMIT_BYTES),
        name="experts",
    )(block_expert, n_used, first, slot, nxt, tok_of, x1p, w_gate, w_up, w_down)


def _combine_kernel(pos_ref, ys_hbm, x1_ref, gate_ref, g_ref, b_ref, out_ref, ybuf, sem):
    i = pl.program_id(0)
    n_steps = pl.num_programs(0)
    rows = TOP_K * COMB_BM

    def gather_start(tile, s):
        def issue(r, carry):
            for k in range(TOP_K):
                src = pl.multiple_of(pos_ref[(tile * COMB_BM + r) * TOP_K + k] * SLAB, SLAB)
                dst = pl.multiple_of((k * COMB_BM + r) * SLAB, SLAB)
                pltpu.make_async_copy(ys_hbm.at[pl.ds(src, SLAB)], ybuf.at[s, pl.ds(dst, SLAB)],
                                      sem.at[s]).start()
            return carry
        lax.fori_loop(0, COMB_BM, issue, 0, unroll=4)

    @pl.when(i == 0)
    def _():
        gather_start(0, 0)

    s = lax.rem(i, 2)

    @pl.when(i + 1 < n_steps)
    def _():
        gather_start(i + 1, 1 - s)

    pltpu.make_async_copy(ys_hbm.at[pl.ds(0, rows * SLAB)], ybuf.at[s], sem.at[s]).wait()
    gate = gate_ref[...]
    y0 = _load_packed_rows(ybuf.at[s, pl.ds(0, COMB_BM * SLAB)], COMB_BM)
    y1 = _load_packed_rows(ybuf.at[s, pl.ds(COMB_BM * SLAB, COMB_BM * SLAB)], COMB_BM)
    ffn = y0 * gate[:, 0:1] + y1 * gate[:, 1:2]
    z = ALPHA * x1_ref[...] + ffn
    mu = jnp.mean(z, axis=-1, keepdims=True)
    d = z - mu
    var = jnp.mean(d * d, axis=-1, keepdims=True)
    out_ref[...] = d * lax.rsqrt(var + LN_EPS) * g_ref[...] + b_ref[...]


def _combine_call(pos, ys, x1, gates, ln_g, ln_b):
    row_full = pl.BlockSpec((COMB_BM, D_MODEL), lambda i, pos: (i, 0))
    vec = pl.BlockSpec((1, D_MODEL), lambda i, pos: (0, 0))
    grid_spec = pltpu.PrefetchScalarGridSpec(
        num_scalar_prefetch=1,
        grid=(SEQ // COMB_BM,),
        in_specs=[
            pl.BlockSpec(memory_space=pl.ANY),
            row_full,
            pl.BlockSpec((COMB_BM, TOP_K), lambda i, pos: (i, 0)),
            vec, vec,
        ],
        out_specs=row_full,
        scratch_shapes=[pltpu.VMEM((2, TOP_K * COMB_BM * SLAB, 128), jnp.uint32),
                        pltpu.SemaphoreType.DMA((2,))],
    )
    return pl.pallas_call(
        _combine_kernel,
        grid_spec=grid_spec,
        out_shape=jax.ShapeDtypeStruct((SEQ, D_MODEL), jnp.float32),
        compiler_params=pltpu.CompilerParams(
            dimension_semantics=("arbitrary",),
            vmem_limit_bytes=VMEM_LIMIT_BYTES),
        name="combine",
    )(pos, ys, x1, gates, ln_g, ln_b)


def _route(logits):
    group_logits = logits[:, :N_GROUPS]
    g_sel = jnp.argmax(group_logits, axis=-1).astype(jnp.int32)
    g_weight = jnp.take_along_axis(jax.nn.softmax(group_logits, axis=-1), g_sel[:, None], axis=1)[:, 0]
    inner_all = logits[:, N_GROUPS:N_GROUPS + N_EXPERTS].reshape(SEQ, N_GROUPS, EXPERTS_PER_GROUP)
    inner = jnp.take_along_axis(inner_all, g_sel[:, None, None], axis=1)[:, 0]
    top_logits, top_idx = lax.top_k(inner, TOP_K)
    gates = g_weight[:, None] * jax.nn.softmax(top_logits, axis=-1)
    expert_id = (g_sel[:, None] * EXPERTS_PER_GROUP + top_idx).reshape(-1).astype(jnp.int32)
    onehot = (expert_id[:, None] == jnp.arange(N_EXPERTS, dtype=jnp.int32)[None, :]).astype(jnp.int32)
    before = jnp.cumsum(onehot, axis=0) - onehot
    rank = jnp.sum(before * onehot, axis=1)
    counts = jnp.sum(onehot, axis=0)
    pcounts = ((counts + ROW_BLOCK - 1) // ROW_BLOCK) * ROW_BLOCK
    pend = jnp.cumsum(pcounts)
    poffs = pend - pcounts
    pos = (jnp.sum(onehot * poffs[None, :], axis=1) + rank).astype(jnp.int32)
    tok = jnp.arange(N_ROWS, dtype=jnp.int32) // TOP_K
    tok_of = jnp.zeros((N_BLOCKS * ROW_BLOCK,), jnp.int32).at[pos].set(tok)
    starts = jnp.arange(N_BLOCKS, dtype=jnp.int32) * ROW_BLOCK
    block_expert = jnp.minimum(jnp.searchsorted(pend, starts, side="right"),
                               N_EXPERTS - 1).astype(jnp.int32)
    n_used = (pend[-1] // ROW_BLOCK).astype(jnp.int32).reshape(1)
    prev = jnp.concatenate([jnp.full((1,), -1, jnp.int32), block_expert[:-1]])
    first = (block_expert != prev).astype(jnp.int32)
    slot = lax.rem(jnp.cumsum(first) - 1, 2).astype(jnp.int32)
    eids = jnp.arange(N_EXPERTS, dtype=jnp.int32)
    cand = jnp.where(counts > 0, eids, N_EXPERTS)
    later = lax.cummin(cand, axis=0, reverse=True)
    nxt_e = jnp.concatenate([later[1:], jnp.full((1,), N_EXPERTS, jnp.int32)])
    nxt_e = jnp.where(nxt_e >= N_EXPERTS, -1, nxt_e).astype(jnp.int32)
    nxt = nxt_e[block_expert]
    return gates, pos, tok_of, block_expert, n_used, first, slot, nxt


def kernel(x, w_in, b_in, conv_w, conv_b, conv_norm_g, conv_norm_b, gmlp_norm_g, gmlp_norm_b,
           w_spatial, b_spatial, w_out, b_out, ln1_g, ln1_b, w_router_group, b_router_group,
           w_router_expert, b_router_expert, w_expert_gate, w_expert_up, w_expert_down,
           ln2_g, ln2_b):
    assert x.shape == (1, SEQ, D_MODEL) and w_in.shape[0] == 1
    x2d = x.reshape(SEQ, D_MODEL)
    xb = x2d.astype(jnp.bfloat16)
    ya, yg = _mixer_call(
        xb, w_in[0].astype(jnp.bfloat16), b_in, conv_w[0], conv_b, conv_norm_g, conv_norm_b,
        gmlp_norm_g, gmlp_norm_b, w_spatial[0], b_spatial[0][:, :, None])

    pad = ROUTER_COLS - N_GROUPS - N_EXPERTS
    w_router = jnp.concatenate(
        [w_router_group[0],
         jnp.transpose(w_router_expert[0], (1, 0, 2)).reshape(D_MODEL, N_EXPERTS),
         jnp.zeros((D_MODEL, pad), jnp.float32)], axis=1)
    b_router = jnp.concatenate(
        [b_router_group[0], b_router_expert[0].reshape(-1), jnp.zeros((pad,), jnp.float32)])[None, :]
    x1, x1p, logits = _outproj_call(ya, yg, x2d, w_out[0].astype(jnp.bfloat16), b_out, ln1_g, ln1_b,
                                    w_router, b_router)

    gates, pos, tok_of, block_expert, n_used, first, slot, nxt = _route(logits)
    ys = _expert_call(block_expert, n_used, first, slot, nxt, tok_of, x1p,
                      w_expert_gate[0], w_expert_up[0], w_expert_down[0])
    out = _combine_call(pos, ys, x1, gates, ln2_g, ln2_b)
    return out.reshape(1, SEQ, D_MODEL)
```

```python
import jax
import jax.numpy as jnp
from jax import lax
from jax.experimental import pallas as pl
from jax.experimental.pallas import tpu as pltpu

D_MODEL = 4096
SEQ = 8192
D_CONV = D_MODEL // 2
D_GMLP = D_MODEL // 2
CONV_WIDTH = 31
GROUP_DIM = 128
GMLP_HEADS = 16
CHUNK = 128
N_GROUPS = 8
EXPERTS_PER_GROUP = 8
N_EXPERTS = N_GROUPS * EXPERTS_PER_GROUP
TOP_K = 2
D_EXPERT = D_MODEL // 8
ROW_BLOCK = 128
LN_EPS = 1e-5
ALPHA = 2.0 ** 0.25

N_ROWS = SEQ * TOP_K
N_BLOCKS = N_ROWS // ROW_BLOCK + N_EXPERTS
ROUTER_COLS = 128
SLAB = D_MODEL // 2 // 128

VMEM_LIMIT_BYTES = 60 * 1024 * 1024

MIX_BM = 1024
MIX_SUB = 256
MIX_BN = 256
HALO = 32
CONV_ROWS = 128
SHIFT_ROWS = CONV_ROWS + HALO - 8

OUT_BM = 128
ROUTE_TC = 512
BLK_LANES = 256
BLK_EXPERT, BLK_FIRST, BLK_SLOT, BLK_NEXT, BLK_USED = range(5)
COMB_BM = 128


def _group_norm(v, g, b):
    outs = []
    for s in range(v.shape[1] // GROUP_DIM):
        blk = v[:, s * GROUP_DIM:(s + 1) * GROUP_DIM]
        mu = jnp.mean(blk, axis=-1, keepdims=True)
        d = blk - mu
        var = jnp.mean(d * d, axis=-1, keepdims=True)
        outs.append(d * lax.rsqrt(var + LN_EPS))
    return jnp.concatenate(outs, axis=-1) * g + b


def _mixer_kernel(x_ref, wa_ref, wg_ref, wu_ref, wv_ref, ba_ref, bg_ref, bu_ref, bv_ref,
                  cw_ref, cb_ref, cng_ref, cnb_ref, gng_ref, gnb_ref, wsp_ref, bsp_ref,
                  ya_ref, yg_ref, xs_ref, r_ref, sh_ref):
    i = pl.program_id(1)
    f32 = jnp.float32
    bf16 = jnp.bfloat16
    w_refs = (wa_ref, wg_ref, wu_ref, wv_ref)
    off = HALO - (CONV_WIDTH - 1)

    @pl.when(i == 0)
    def _():
        xs_ref[0:HALO, :] = jnp.zeros((HALO, MIX_BN), f32)

    row = lax.broadcasted_iota(jnp.int32, (CHUNK, CHUNK), 0)
    col = lax.broadcasted_iota(jnp.int32, (CHUNK, CHUNK), 1)
    w_sp = [jnp.where(row >= col, wsp_ref[h], 0.0).astype(bf16) for h in range(MIX_BN // GROUP_DIM)]

    def project(k, q):
        rows = slice(k * MIX_SUB, (k + 1) * MIX_SUB)
        r_ref[k % 2, q] = jnp.dot(x_ref[rows, :], w_refs[q][...], preferred_element_type=f32)

    def finish_conv(k, c):
        t0 = k * MIX_SUB + c * CONV_ROWS
        rows = slice(c * CONV_ROWS, (c + 1) * CONV_ROWS)
        a = r_ref[k % 2, 0, rows, :] + ba_ref[...]
        gate = r_ref[k % 2, 1, rows, :] + bg_ref[...]
        xs_ref[HALO + t0:HALO + t0 + CONV_ROWS, :] = a * jax.nn.sigmoid(gate)
        sh = sh_ref.at[c % 2]
        for m in range(1, 8):
            sh[m - 1] = xs_ref[t0 + m:t0 + m + SHIFT_ROWS, :]
        acc = jnp.zeros((CONV_ROWS, MIX_BN), f32) + cb_ref[...]
        for k2 in range(CONV_WIDTH):
            m, j8 = (off + k2) % 8, 8 * ((off + k2) // 8)
            if m == 0:
                src = xs_ref[t0 + j8:t0 + j8 + CONV_ROWS, :]
            else:
                src = sh[m - 1, j8:j8 + CONV_ROWS, :]
            acc = acc + cw_ref[k2:k2 + 1, :] * src
        y = jax.nn.silu(_group_norm(acc, cng_ref[...], cnb_ref[...]))
        ya_ref[t0:t0 + CONV_ROWS, :] = y.astype(ya_ref.dtype)

    def finish_gmlp(k):
        base = k * MIX_SUB
        n_chunks = MIX_SUB // CHUNK
        us, vs = [], []
        for c in range(n_chunks):
            rows = slice(c * CHUNK, (c + 1) * CHUNK)
            us.append(jax.nn.gelu(r_ref[k % 2, 2, rows, :] + bu_ref[...]))
            v = jax.nn.gelu(r_ref[k % 2, 3, rows, :] + bv_ref[...])
            vs.append(_group_norm(v, gng_ref[...], gnb_ref[...]).astype(bf16))
        for h in range(MIX_BN // GROUP_DIM):
            lanes = slice(h * GROUP_DIM, (h + 1) * GROUP_DIM)
            v_h = jnp.concatenate([v[:, lanes] for v in vs], axis=1)
            s_h = jnp.dot(w_sp[h], v_h, preferred_element_type=f32) + bsp_ref[h]
            for c in range(n_chunks):
                g = us[c][:, lanes] * s_h[:, c * GROUP_DIM:(c + 1) * GROUP_DIM]
                yg_ref[base + c * CHUNK:base + (c + 1) * CHUNK, lanes] = g.astype(yg_ref.dtype)

    def finish(k):
        for c in range(MIX_SUB // CONV_ROWS):
            finish_conv(k, c)
        finish_gmlp(k)

    n_sub = MIX_BM // MIX_SUB
    for k in range(n_sub + 1):
        if k < n_sub:
            for q in range(4):
                project(k, q)
        if k > 0:
            finish(k - 1)
    xs_ref[0:HALO, :] = xs_ref[MIX_BM:MIX_BM + HALO, :]


def _mixer_call(xb, w_in_b, b_in, conv_w, conv_b, cng, cnb, gng, gnb, w_spatial, b_spatial_col):
    nq = D_CONV // MIX_BN
    grid = (nq, SEQ // MIX_BM)

    def wspec(q):
        return pl.BlockSpec((D_MODEL, MIX_BN), lambda j, i, q=q: (0, q * nq + j))

    def bspec(q):
        return pl.BlockSpec((1, MIX_BN), lambda j, i, q=q: (0, q * nq + j))

    vec = pl.BlockSpec((1, MIX_BN), lambda j, i: (0, j))
    heads_per_step = MIX_BN // GROUP_DIM
    in_specs = [
        pl.BlockSpec((MIX_BM, D_MODEL), lambda j, i: (i, 0)),
        wspec(0), wspec(1), wspec(2), wspec(3),
        bspec(0), bspec(1), bspec(2), bspec(3),
        pl.BlockSpec((CONV_WIDTH, MIX_BN), lambda j, i: (0, j)),
        vec, vec, vec, vec, vec,
        pl.BlockSpec((heads_per_step, CHUNK, CHUNK), lambda j, i: (j, 0, 0)),
        pl.BlockSpec((heads_per_step, CHUNK, 1), lambda j, i: (j, 0, 0)),
    ]
    out_spec = pl.BlockSpec((MIX_BM, MIX_BN), lambda j, i: (i, j))
    return pl.pallas_call(
        _mixer_kernel,
        grid=grid,
        in_specs=in_specs,
        out_specs=[out_spec, out_spec],
        out_shape=[jax.ShapeDtypeStruct((SEQ, D_CONV), jnp.bfloat16),
                   jax.ShapeDtypeStruct((SEQ, D_GMLP), jnp.bfloat16)],
        scratch_shapes=[pltpu.VMEM((HALO + MIX_BM, MIX_BN), jnp.float32),
                        pltpu.VMEM((2, 4, MIX_SUB, MIX_BN), jnp.float32),
                        pltpu.VMEM((2, 7, SHIFT_ROWS, MIX_BN), jnp.float32)],
        compiler_params=pltpu.CompilerParams(
            dimension_semantics=("arbitrary", "arbitrary"),
            vmem_limit_bytes=VMEM_LIMIT_BYTES),
        name="mixer",
    )(xb, w_in_b, w_in_b, w_in_b, w_in_b, b_in, b_in, b_in, b_in,
      conv_w, conv_b, cng, cnb, gng, gnb, w_spatial, b_spatial_col)


def _store_packed_rows(dst_ref, x):
    f32, u32 = jnp.float32, jnp.uint32
    half = D_MODEL // 2
    hi = lax.bitcast_convert_type(x[:, :half].astype(jnp.bfloat16).astype(f32), u32)
    lo = lax.bitcast_convert_type(x[:, half:].astype(jnp.bfloat16).astype(f32), u32)
    words = hi | (lo >> 16)
    for j in range(SLAB):
        dst_ref[pl.ds(j, x.shape[0], stride=SLAB), :] = words[:, j * 128:(j + 1) * 128]


def _load_packed_rows(src_ref, rows):
    f32, u32 = jnp.float32, jnp.uint32
    his, los = [], []
    for j in range(SLAB):
        w = src_ref[pl.ds(j, rows, stride=SLAB), :]
        his.append(lax.bitcast_convert_type(w & u32(0xFFFF0000), f32))
        los.append(lax.bitcast_convert_type(w << 16, f32))
    return jnp.concatenate(his + los, axis=1)


def _outproj_kernel(ya_ref, yg_ref, x_ref, w_ref, bo_ref, g_ref, b_ref, wr_ref, br_ref,
                    x1_ref, x1p_ref, logit_ref):
    f32 = jnp.float32
    z = jnp.dot(ya_ref[...], w_ref[0:D_CONV, :], preferred_element_type=f32)
    z = z + jnp.dot(yg_ref[...], w_ref[D_CONV:, :], preferred_element_type=f32)
    z = z + bo_ref[...] + ALPHA * x_ref[...]
    mu = jnp.mean(z, axis=-1, keepdims=True)
    d = z - mu
    var = jnp.mean(d * d, axis=-1, keepdims=True)
    x1 = d * lax.rsqrt(var + LN_EPS) * g_ref[...] + b_ref[...]
    x1_ref[...] = x1
    _store_packed_rows(x1p_ref, x1)
    wr = wr_ref[...]
    wr_hi = wr.astype(jnp.bfloat16)
    wr_lo = (wr - wr_hi.astype(f32)).astype(jnp.bfloat16)
    x_hi = x1.astype(jnp.bfloat16)
    x_lo = (x1 - x_hi.astype(f32)).astype(jnp.bfloat16)
    logits = jnp.dot(x_hi, wr_hi, preferred_element_type=f32)
    logits = logits + (jnp.dot(x_lo, wr_hi, preferred_element_type=f32)
                       + jnp.dot(x_hi, wr_lo, preferred_element_type=f32))
    logit_ref[...] = (logits + br_ref[...]).T


def _outproj_call(ya, yg, x2d, w_out_b, b_out, ln_g, ln_b, w_router, b_router):
    grid = (SEQ // OUT_BM,)
    row_half = pl.BlockSpec((OUT_BM, D_CONV), lambda i: (i, 0))
    row_full = pl.BlockSpec((OUT_BM, D_MODEL), lambda i: (i, 0))
    vec = pl.BlockSpec((1, D_MODEL), lambda i: (0, 0))
    in_specs = [
        row_half, row_half, row_full,
        pl.BlockSpec((D_MODEL, D_MODEL), lambda i: (0, 0), pipeline_mode=pl.Buffered(1)),
        vec, vec, vec,
        pl.BlockSpec((D_MODEL, ROUTER_COLS), lambda i: (0, 0)),
        pl.BlockSpec((1, ROUTER_COLS), lambda i: (0, 0)),
    ]
    return pl.pallas_call(
        _outproj_kernel,
        grid=grid,
        in_specs=in_specs,
        out_specs=[row_full,
                   pl.BlockSpec((OUT_BM * SLAB, 128), lambda i: (i, 0)),
                   pl.BlockSpec((ROUTER_COLS, OUT_BM), lambda i: (0, i))],
        out_shape=[jax.ShapeDtypeStruct((SEQ, D_MODEL), jnp.float32),
                   jax.ShapeDtypeStruct((SEQ * SLAB, 128), jnp.uint32),
                   jax.ShapeDtypeStruct((ROUTER_COLS, SEQ), jnp.float32)],
        compiler_params=pltpu.CompilerParams(
            dimension_semantics=("arbitrary",),
            vmem_limit_bytes=VMEM_LIMIT_BYTES),
        name="outproj",
    )(ya, yg, x2d, w_out_b, b_out, ln_g, ln_b, w_router, b_router)


def _expert_kernel(blk_ref, pos_ref,
                   x1p_hbm, wg_hbm, wu_hbm, wd_hbm, out_ref, xbuf, wgb, wub, wdb, tok_ref, gsem, wsem):
    b = pl.program_id(0)
    n_used = blk_ref[BLK_USED, 0]

    def weight_copies(e, s):
        return (pltpu.make_async_copy(wg_hbm.at[e], wgb.at[s], wsem.at[s, 0]),
                pltpu.make_async_copy(wu_hbm.at[e], wub.at[s], wsem.at[s, 1]),
                pltpu.make_async_copy(wd_hbm.at[e], wdb.at[s], wsem.at[s, 2]))

    def gather_start(blk, s):
        def issue(r, carry):
            src = pl.multiple_of(tok_ref[blk * ROW_BLOCK + r] * SLAB, SLAB)
            dst = pl.multiple_of(r * SLAB, SLAB)
            pltpu.make_async_copy(x1p_hbm.at[pl.ds(src, SLAB)], xbuf.at[s, pl.ds(dst, SLAB)],
                                  gsem.at[s]).start()
            return carry
        lax.fori_loop(0, ROW_BLOCK, issue, 0, unroll=8)

    def gather_wait(s):
        pltpu.make_async_copy(x1p_hbm.at[pl.ds(0, ROW_BLOCK * SLAB)], xbuf.at[s], gsem.at[s]).wait()

    @pl.when(b == 0)
    def _():
        for c in weight_copies(blk_ref[BLK_EXPERT, 0], 0):
            c.start(priority=1)

        def clear(r, carry):
            tok_ref[r] = 0
            return carry
        lax.fori_loop(0, N_BLOCKS * ROW_BLOCK, clear, 0, unroll=8)

        def scatter(t, carry):
            for k in range(TOP_K):
                tok_ref[pos_ref[k, t]] = t
            return carry
        lax.fori_loop(0, SEQ, scatter, 0, unroll=4)
        gather_start(0, 0)

    @pl.when(b < n_used)
    def _():
        ws = blk_ref[BLK_SLOT, b]
        xs = lax.rem(b, 2)

        @pl.when(blk_ref[BLK_FIRST, b] == 1)
        def _():
            @pl.when(blk_ref[BLK_NEXT, b] >= 0)
            def _():
                for c in weight_copies(blk_ref[BLK_NEXT, b], 1 - ws):
                    c.start(priority=1)
            for c in weight_copies(blk_ref[BLK_EXPERT, b], ws):
                c.wait()

        @pl.when(b + 1 < n_used)
        def _():
            gather_start(b + 1, 1 - xs)

        gather_wait(xs)
        xb = _load_packed_rows(xbuf.at[xs], ROW_BLOCK).astype(jnp.bfloat16)
        g = jnp.dot(xb, wgb[ws], preferred_element_type=jnp.float32)
        u = jnp.dot(xb, wub[ws], preferred_element_type=jnp.float32)
        h = jax.nn.silu(g) * u
        _store_packed_rows(out_ref, jnp.dot(h, wdb[ws], preferred_element_type=jnp.float32))

    @pl.when(b >= n_used)
    def _():
        out_ref[...] = jnp.zeros(out_ref.shape, out_ref.dtype)


def _expert_call(blk, pos, x1p, w_gate, w_up, w_down):
    any_spec = pl.BlockSpec(memory_space=pl.ANY)
    grid_spec = pltpu.PrefetchScalarGridSpec(
        num_scalar_prefetch=2,
        grid=(N_BLOCKS,),
        in_specs=[any_spec, any_spec, any_spec, any_spec],
        out_specs=pl.BlockSpec((ROW_BLOCK * SLAB, 128), lambda b, *_: (b, 0)),
        scratch_shapes=[pltpu.VMEM((2, ROW_BLOCK * SLAB, 128), jnp.uint32),
                        pltpu.VMEM((2, D_MODEL, D_EXPERT), jnp.float32),
                        pltpu.VMEM((2, D_MODEL, D_EXPERT), jnp.float32),
                        pltpu.VMEM((2, D_EXPERT, D_MODEL), jnp.float32),
                        pltpu.SMEM((N_BLOCKS * ROW_BLOCK,), jnp.int32),
                        pltpu.SemaphoreType.DMA((2,)),
                        pltpu.SemaphoreType.DMA((2, 3))],
    )
    return pl.pallas_call(
        _expert_kernel,
        grid_spec=grid_spec,
        out_shape=jax.ShapeDtypeStruct((N_BLOCKS * ROW_BLOCK * SLAB, 128), jnp.uint32),
        compiler_params=pltpu.CompilerParams(
            dimension_semantics=("arbitrary",),
            vmem_limit_bytes=VMEM_LIMIT_BYTES),
        name="experts",
    )(blk, pos, x1p, w_gate, w_up, w_down)


def _combine_kernel(pos_ref, ys_hbm, x1_ref, gate_ref, g_ref, b_ref, out_ref, ybuf, sem):
    i = pl.program_id(0)
    n_steps = pl.num_programs(0)
    rows = TOP_K * COMB_BM

    def gather_start(tile, s):
        def issue(r, carry):
            for k in range(TOP_K):
                src = pl.multiple_of(pos_ref[k, tile * COMB_BM + r] * SLAB, SLAB)
                dst = pl.multiple_of((k * COMB_BM + r) * SLAB, SLAB)
                pltpu.make_async_copy(ys_hbm.at[pl.ds(src, SLAB)], ybuf.at[s, pl.ds(dst, SLAB)],
                                      sem.at[s]).start()
            return carry
        lax.fori_loop(0, COMB_BM, issue, 0, unroll=4)

    @pl.when(i == 0)
    def _():
        gather_start(0, 0)

    s = lax.rem(i, 2)

    @pl.when(i + 1 < n_steps)
    def _():
        gather_start(i + 1, 1 - s)

    pltpu.make_async_copy(ys_hbm.at[pl.ds(0, rows * SLAB)], ybuf.at[s], sem.at[s]).wait()
    gate = gate_ref[...]
    y0 = _load_packed_rows(ybuf.at[s, pl.ds(0, COMB_BM * SLAB)], COMB_BM)
    y1 = _load_packed_rows(ybuf.at[s, pl.ds(COMB_BM * SLAB, COMB_BM * SLAB)], COMB_BM)
    ffn = y0 * gate[:, 0:1] + y1 * gate[:, 1:2]
    z = ALPHA * x1_ref[...] + ffn
    mu = jnp.mean(z, axis=-1, keepdims=True)
    d = z - mu
    var = jnp.mean(d * d, axis=-1, keepdims=True)
    out_ref[...] = d * lax.rsqrt(var + LN_EPS) * g_ref[...] + b_ref[...]


def _combine_call(pos, ys, x1, gates, ln_g, ln_b):
    row_full = pl.BlockSpec((COMB_BM, D_MODEL), lambda i, pos: (i, 0))
    vec = pl.BlockSpec((1, D_MODEL), lambda i, pos: (0, 0))
    grid_spec = pltpu.PrefetchScalarGridSpec(
        num_scalar_prefetch=1,
        grid=(SEQ // COMB_BM,),
        in_specs=[
            pl.BlockSpec(memory_space=pl.ANY),
            row_full,
            pl.BlockSpec((COMB_BM, TOP_K), lambda i, pos: (i, 0)),
            vec, vec,
        ],
        out_specs=row_full,
        scratch_shapes=[pltpu.VMEM((2, TOP_K * COMB_BM * SLAB, 128), jnp.uint32),
                        pltpu.SemaphoreType.DMA((2,))],
    )
    return pl.pallas_call(
        _combine_kernel,
        grid_spec=grid_spec,
        out_shape=jax.ShapeDtypeStruct((SEQ, D_MODEL), jnp.float32),
        compiler_params=pltpu.CompilerParams(
            dimension_semantics=("arbitrary",),
            vmem_limit_bytes=VMEM_LIMIT_BYTES),
        name="combine",
    )(pos, ys, x1, gates, ln_g, ln_b)


def _route_kernel(lt_ref, gates_ref, pos_ref, blk_ref, sel_ref, rank_ref):
    f32, i32 = jnp.float32, jnp.int32
    neg = jnp.float32(-jnp.inf)
    n_chunks = SEQ // ROUTE_TC
    erow = lax.broadcasted_iota(i32, (N_EXPERTS, ROUTE_TC), 0)
    grow = lax.broadcasted_iota(i32, (N_GROUPS, ROUTE_TC), 0)
    src_tok = lax.broadcasted_iota(i32, (ROUTE_TC, ROUTE_TC), 0)
    dst_tok = lax.broadcasted_iota(i32, (ROUTE_TC, ROUTE_TC), 1)
    before = (src_tok < dst_tok).astype(jnp.bfloat16)

    def first_index(mask, idx, size):
        return jnp.min(jnp.where(mask, idx, size), axis=0, keepdims=True)

    counts = jnp.zeros((N_EXPERTS, 1), f32)
    for c in range(n_chunks):
        lanes = slice(c * ROUTE_TC, (c + 1) * ROUTE_TC)
        gl = lt_ref[0:N_GROUPS, lanes]
        g_max = jnp.max(gl, axis=0, keepdims=True)
        g_sel = first_index(gl == g_max, grow, N_GROUPS)
        g_weight = 1.0 / jnp.sum(jnp.exp(gl - g_max), axis=0, keepdims=True)
        el = lt_ref[N_GROUPS:N_GROUPS + N_EXPERTS, lanes]
        in_group = (erow >= g_sel * EXPERTS_PER_GROUP) & (erow < (g_sel + 1) * EXPERTS_PER_GROUP)
        el = jnp.where(in_group, el, neg)
        t1 = jnp.max(el, axis=0, keepdims=True)
        e1 = first_index(el == t1, erow, N_EXPERTS)
        el2 = jnp.where(erow == e1, neg, el)
        t2 = jnp.max(el2, axis=0, keepdims=True)
        e2 = first_index(el2 == t2, erow, N_EXPERTS)
        r = jnp.exp(t2 - t1)
        gates_ref[0:1, lanes] = g_weight * (1.0 / (1.0 + r))
        gates_ref[1:2, lanes] = g_weight * (r / (1.0 + r))
        oh1 = erow == e1
        oh2 = erow == e2
        hits = oh1.astype(jnp.bfloat16) + oh2.astype(jnp.bfloat16)
        ahead = jnp.dot(hits, before, preferred_element_type=f32) + counts
        rank_ref[0:1, lanes] = jnp.sum(jnp.where(oh1, ahead, 0.0), axis=0, keepdims=True)
        rank_ref[1:2, lanes] = jnp.sum(jnp.where(oh2, ahead, 0.0), axis=0, keepdims=True)
        sel_ref[0:1, lanes] = e1
        sel_ref[1:2, lanes] = e2
        counts = counts + jnp.sum(hits.astype(f32), axis=1, keepdims=True)

    n_blk = lax.shift_right_logical(counts.astype(i32) + (ROW_BLOCK - 1), ROW_BLOCK.bit_length() - 1)
    e_src = lax.broadcasted_iota(i32, (N_EXPERTS, N_EXPERTS), 1)
    e_dst = lax.broadcasted_iota(i32, (N_EXPERTS, N_EXPERTS), 0)
    incl = (e_src <= e_dst).astype(jnp.bfloat16)
    n_blk_b = jnp.broadcast_to(n_blk.astype(f32), (N_EXPERTS, 128)).astype(jnp.bfloat16)
    blk_end = jnp.dot(incl, n_blk_b, preferred_element_type=f32)[:, 0:1].astype(i32)
    blk_start = blk_end - n_blk
    row_start = (blk_start * ROW_BLOCK).astype(f32)

    for c in range(n_chunks):
        lanes = slice(c * ROUTE_TC, (c + 1) * ROUTE_TC)
        for k in range(TOP_K):
            oh = erow == sel_ref[k:k + 1, lanes]
            start = jnp.sum(jnp.where(oh, row_start, 0.0), axis=0, keepdims=True)
            pos_ref[k:k + 1, lanes] = (start + rank_ref[k:k + 1, lanes]).astype(i32)

    b_lane = lax.broadcasted_iota(i32, (N_EXPERTS, BLK_LANES), 1)
    e_sub = lax.broadcasted_iota(i32, (N_EXPERTS, BLK_LANES), 0)
    used = n_blk > 0
    block_expert = jnp.minimum(jnp.sum((blk_end <= b_lane).astype(i32), axis=0, keepdims=True), N_EXPERTS - 1)
    first = jnp.sum((used & (blk_start == b_lane)).astype(i32), axis=0, keepdims=True)
    run_index = jnp.sum((used & (e_sub < block_expert)).astype(i32), axis=0, keepdims=True)
    nxt = jnp.min(jnp.where(used & (e_sub > block_expert), e_sub, N_EXPERTS), axis=0, keepdims=True)
    nxt = jnp.where(nxt >= N_EXPERTS, -1, nxt)
    n_used = jnp.broadcast_to(blk_end[N_EXPERTS - 1:N_EXPERTS, :], (1, BLK_LANES))
    blk_ref[...] = jnp.concatenate(
        [block_expert, first, run_index & 1, nxt, n_used, jnp.zeros((3, BLK_LANES), i32)], axis=0)


def _route_call(logits_t):
    return pl.pallas_call(
        _route_kernel,
        out_shape=[jax.ShapeDtypeStruct((TOP_K, SEQ), jnp.float32),
                   jax.ShapeDtypeStruct((TOP_K, SEQ), jnp.int32),
                   jax.ShapeDtypeStruct((8, BLK_LANES), jnp.int32)],
        scratch_shapes=[pltpu.VMEM((TOP_K, SEQ), jnp.int32),
                        pltpu.VMEM((TOP_K, SEQ), jnp.float32)],
        compiler_params=pltpu.CompilerParams(vmem_limit_bytes=VMEM_LIMIT_BYTES),
        name="route",
    )(logits_t)


def kernel(x, w_in, b_in, conv_w, conv_b, conv_norm_g, conv_norm_b, gmlp_norm_g, gmlp_norm_b,
           w_spatial, b_spatial, w_out, b_out, ln1_g, ln1_b, w_router_group, b_router_group,
           w_router_expert, b_router_expert, w_expert_gate, w_expert_up, w_expert_down,
           ln2_g, ln2_b):
    assert x.shape == (1, SEQ, D_MODEL) and w_in.shape[0] == 1
    x2d = x.reshape(SEQ, D_MODEL)
    xb = x2d.astype(jnp.bfloat16)
    ya, yg = _mixer_call(
        xb, w_in[0].astype(jnp.bfloat16), b_in, conv_w[0], conv_b, conv_norm_g, conv_norm_b,
        gmlp_norm_g, gmlp_norm_b, w_spatial[0], b_spatial[0][:, :, None])

    pad = ROUTER_COLS - N_GROUPS - N_EXPERTS
    w_router = jnp.concatenate(
        [w_router_group[0],
         jnp.transpose(w_router_expert[0], (1, 0, 2)).reshape(D_MODEL, N_EXPERTS),
         jnp.zeros((D_MODEL, pad), jnp.float32)], axis=1)
    b_router = jnp.concatenate(
        [b_router_group[0], b_router_expert[0].reshape(-1), jnp.zeros((pad,), jnp.float32)])[None, :]
    x1, x1p, logits_t = _outproj_call(ya, yg, x2d, w_out[0].astype(jnp.bfloat16), b_out, ln1_g, ln1_b,
                                    w_router, b_router)

    gates_t, pos, blk = _route_call(logits_t)
    ys = _expert_call(blk, pos, x1p, w_expert_gate[0], w_expert_up[0], w_expert_down[0])
    out = _combine_call(pos, ys, x1, gates_t.T, ln2_g, ln2_b)
    return out.reshape(1, SEQ, D_MODEL)
```

```python
import jax
import jax.numpy as jnp
from jax import lax
from jax.experimental import pallas as pl
from jax.experimental.pallas import tpu as pltpu

D_MODEL = 4096
SEQ = 8192
D_CONV = D_MODEL // 2
D_GMLP = D_MODEL // 2
CONV_WIDTH = 31
GROUP_DIM = 128
GMLP_HEADS = 16
CHUNK = 128
N_GROUPS = 8
EXPERTS_PER_GROUP = 8
N_EXPERTS = N_GROUPS * EXPERTS_PER_GROUP
TOP_K = 2
D_EXPERT = D_MODEL // 8
ROW_BLOCK = 128
LN_EPS = 1e-5
ALPHA = 2.0 ** 0.25

N_ROWS = SEQ * TOP_K
N_BLOCKS = N_ROWS // ROW_BLOCK + N_EXPERTS
ROUTER_COLS = 128
SLAB = D_MODEL // 2 // 128

VMEM_LIMIT_BYTES = 60 * 1024 * 1024

MIX_BM = 1024
MIX_SUB = 256
MIX_BN = 256
HALO = 32
CONV_ROWS = 128
SHIFT_ROWS = CONV_ROWS + HALO - 8

OUT_BM = 128
ROUTE_TC = 512
BLK_LANES = 256
BLK_EXPERT, BLK_FIRST, BLK_SLOT, BLK_NEXT, BLK_USED = range(5)
COMB_BM = 128


def _group_norm(v, g, b):
    outs = []
    for s in range(v.shape[1] // GROUP_DIM):
        blk = v[:, s * GROUP_DIM:(s + 1) * GROUP_DIM]
        mu = jnp.mean(blk, axis=-1, keepdims=True)
        d = blk - mu
        var = jnp.mean(d * d, axis=-1, keepdims=True)
        outs.append(d * lax.rsqrt(var + LN_EPS))
    return jnp.concatenate(outs, axis=-1) * g + b


def _mixer_kernel(x_ref, wa_ref, wg_ref, wu_ref, wv_ref, ba_ref, bg_ref, bu_ref, bv_ref,
                  cw_ref, cb_ref, cng_ref, cnb_ref, gng_ref, gnb_ref, wsp_ref, bsp_ref,
                  ya_ref, yg_ref, xs_ref, r_ref, sh_ref):
    i = pl.program_id(1)
    f32 = jnp.float32
    bf16 = jnp.bfloat16
    w_refs = (wa_ref, wg_ref, wu_ref, wv_ref)
    off = HALO - (CONV_WIDTH - 1)

    @pl.when(i == 0)
    def _():
        xs_ref[0:HALO, :] = jnp.zeros((HALO, MIX_BN), f32)

    row = lax.broadcasted_iota(jnp.int32, (CHUNK, CHUNK), 0)
    col = lax.broadcasted_iota(jnp.int32, (CHUNK, CHUNK), 1)
    w_sp = [jnp.where(row >= col, wsp_ref[h], 0.0).astype(bf16) for h in range(MIX_BN // GROUP_DIM)]

    def project(k):
        xk = x_ref[k * MIX_SUB:(k + 1) * MIX_SUB, :].astype(bf16)
        for q in range(4):
            r_ref[k % 2, q] = jnp.dot(xk, w_refs[q][...], preferred_element_type=f32)

    def finish_conv(k, c):
        t0 = k * MIX_SUB + c * CONV_ROWS
        rows = slice(c * CONV_ROWS, (c + 1) * CONV_ROWS)
        a = r_ref[k % 2, 0, rows, :] + ba_ref[...]
        gate = r_ref[k % 2, 1, rows, :] + bg_ref[...]
        xs_ref[HALO + t0:HALO + t0 + CONV_ROWS, :] = a * jax.nn.sigmoid(gate)
        sh = sh_ref.at[c % 2]
        for m in range(1, 8):
            sh[m - 1] = xs_ref[t0 + m:t0 + m + SHIFT_ROWS, :]
        acc = jnp.zeros((CONV_ROWS, MIX_BN), f32) + cb_ref[...]
        for k2 in range(CONV_WIDTH):
            m, j8 = (off + k2) % 8, 8 * ((off + k2) // 8)
            if m == 0:
                src = xs_ref[t0 + j8:t0 + j8 + CONV_ROWS, :]
            else:
                src = sh[m - 1, j8:j8 + CONV_ROWS, :]
            acc = acc + cw_ref[k2:k2 + 1, :] * src
        y = jax.nn.silu(_group_norm(acc, cng_ref[...], cnb_ref[...]))
        ya_ref[t0:t0 + CONV_ROWS, :] = y.astype(ya_ref.dtype)

    def finish_gmlp(k):
        base = k * MIX_SUB
        n_chunks = MIX_SUB // CHUNK
        us, vs = [], []
        for c in range(n_chunks):
            rows = slice(c * CHUNK, (c + 1) * CHUNK)
            us.append(jax.nn.gelu(r_ref[k % 2, 2, rows, :] + bu_ref[...]))
            v = jax.nn.gelu(r_ref[k % 2, 3, rows, :] + bv_ref[...])
            vs.append(_group_norm(v, gng_ref[...], gnb_ref[...]).astype(bf16))
        for h in range(MIX_BN // GROUP_DIM):
            lanes = slice(h * GROUP_DIM, (h + 1) * GROUP_DIM)
            v_h = jnp.concatenate([v[:, lanes] for v in vs], axis=1)
            s_h = jnp.dot(w_sp[h], v_h, preferred_element_type=f32) + bsp_ref[h]
            for c in range(n_chunks):
                g = us[c][:, lanes] * s_h[:, c * GROUP_DIM:(c + 1) * GROUP_DIM]
                yg_ref[base + c * CHUNK:base + (c + 1) * CHUNK, lanes] = g.astype(yg_ref.dtype)

    def finish(k):
        for c in range(MIX_SUB // CONV_ROWS):
            finish_conv(k, c)
        finish_gmlp(k)

    n_sub = MIX_BM // MIX_SUB
    for k in range(n_sub + 1):
        if k < n_sub:
            project(k)
        if k > 0:
            finish(k - 1)
    xs_ref[0:HALO, :] = xs_ref[MIX_BM:MIX_BM + HALO, :]


def _mixer_call(x2d, w_in_b, b_in, conv_w, conv_b, cng, cnb, gng, gnb, w_spatial, b_spatial_col):
    nq = D_CONV // MIX_BN
    grid = (nq, SEQ // MIX_BM)

    def wspec(q):
        return pl.BlockSpec((D_MODEL, MIX_BN), lambda j, i, q=q: (0, q * nq + j))

    def bspec(q):
        return pl.BlockSpec((1, MIX_BN), lambda j, i, q=q: (0, q * nq + j))

    vec = pl.BlockSpec((1, MIX_BN), lambda j, i: (0, j))
    heads_per_step = MIX_BN // GROUP_DIM
    in_specs = [
        pl.BlockSpec((MIX_BM, D_MODEL), lambda j, i: (i, 0)),
        wspec(0), wspec(1), wspec(2), wspec(3),
        bspec(0), bspec(1), bspec(2), bspec(3),
        pl.BlockSpec((CONV_WIDTH, MIX_BN), lambda j, i: (0, j)),
        vec, vec, vec, vec, vec,
        pl.BlockSpec((heads_per_step, CHUNK, CHUNK), lambda j, i: (j, 0, 0)),
        pl.BlockSpec((heads_per_step, CHUNK, 1), lambda j, i: (j, 0, 0)),
    ]
    out_spec = pl.BlockSpec((MIX_BM, MIX_BN), lambda j, i: (i, j))
    return pl.pallas_call(
        _mixer_kernel,
        grid=grid,
        in_specs=in_specs,
        out_specs=[out_spec, out_spec],
        out_shape=[jax.ShapeDtypeStruct((SEQ, D_CONV), jnp.bfloat16),
                   jax.ShapeDtypeStruct((SEQ, D_GMLP), jnp.bfloat16)],
        scratch_shapes=[pltpu.VMEM((HALO + MIX_BM, MIX_BN), jnp.float32),
                        pltpu.VMEM((2, 4, MIX_SUB, MIX_BN), jnp.float32),
                        pltpu.VMEM((2, 7, SHIFT_ROWS, MIX_BN), jnp.float32)],
        compiler_params=pltpu.CompilerParams(
            dimension_semantics=("arbitrary", "arbitrary"),
            vmem_limit_bytes=VMEM_LIMIT_BYTES),
        name="mixer",
    )(x2d, w_in_b, w_in_b, w_in_b, w_in_b, b_in, b_in, b_in, b_in,
      conv_w, conv_b, cng, cnb, gng, gnb, w_spatial, b_spatial_col)


def _store_packed_rows(dst_ref, x):
    f32, u32 = jnp.float32, jnp.uint32
    half = D_MODEL // 2
    hi = lax.bitcast_convert_type(x[:, :half].astype(jnp.bfloat16).astype(f32), u32)
    lo = lax.bitcast_convert_type(x[:, half:].astype(jnp.bfloat16).astype(f32), u32)
    words = hi | (lo >> 16)
    for j in range(SLAB):
        dst_ref[pl.ds(j, x.shape[0], stride=SLAB), :] = words[:, j * 128:(j + 1) * 128]


def _load_packed_rows(src_ref, rows):
    f32, u32 = jnp.float32, jnp.uint32
    his, los = [], []
    for j in range(SLAB):
        w = src_ref[pl.ds(j, rows, stride=SLAB), :]
        his.append(lax.bitcast_convert_type(w & u32(0xFFFF0000), f32))
        los.append(lax.bitcast_convert_type(w << 16, f32))
    return jnp.concatenate(his + los, axis=1)


def _outproj_kernel(ya_ref, yg_ref, x_ref, w_ref, bo_ref, g_ref, b_ref, wr_ref, br_ref,
                    x1_ref, x1p_ref, logit_ref):
    f32 = jnp.float32
    z = jnp.dot(ya_ref[...], w_ref[0:D_CONV, :], preferred_element_type=f32)
    z = z + jnp.dot(yg_ref[...], w_ref[D_CONV:, :], preferred_element_type=f32)
    z = z + bo_ref[...] + ALPHA * x_ref[...]
    mu = jnp.mean(z, axis=-1, keepdims=True)
    d = z - mu
    var = jnp.mean(d * d, axis=-1, keepdims=True)
    x1 = d * lax.rsqrt(var + LN_EPS) * g_ref[...] + b_ref[...]
    x1_ref[...] = x1
    _store_packed_rows(x1p_ref, x1)
    wr = wr_ref[...]
    wr_hi = wr.astype(jnp.bfloat16)
    wr_lo = (wr - wr_hi.astype(f32)).astype(jnp.bfloat16)
    x_hi = x1.astype(jnp.bfloat16)
    x_lo = (x1 - x_hi.astype(f32)).astype(jnp.bfloat16)
    logits = jnp.dot(x_hi, wr_hi, preferred_element_type=f32)
    logits = logits + (jnp.dot(x_lo, wr_hi, preferred_element_type=f32)
                       + jnp.dot(x_hi, wr_lo, preferred_element_type=f32))
    logit_ref[...] = (logits + br_ref[...]).T


def _outproj_call(ya, yg, x2d, w_out_b, b_out, ln_g, ln_b, w_router, b_router):
    grid = (SEQ // OUT_BM,)
    row_half = pl.BlockSpec((OUT_BM, D_CONV), lambda i: (i, 0))
    row_full = pl.BlockSpec((OUT_BM, D_MODEL), lambda i: (i, 0))
    vec = pl.BlockSpec((1, D_MODEL), lambda i: (0, 0))
    in_specs = [
        row_half, row_half, row_full,
        pl.BlockSpec((D_MODEL, D_MODEL), lambda i: (0, 0), pipeline_mode=pl.Buffered(1)),
        vec, vec, vec,
        pl.BlockSpec((D_MODEL, ROUTER_COLS), lambda i: (0, 0)),
        pl.BlockSpec((1, ROUTER_COLS), lambda i: (0, 0)),
    ]
    return pl.pallas_call(
        _outproj_kernel,
        grid=grid,
        in_specs=in_specs,
        out_specs=[row_full,
                   pl.BlockSpec((OUT_BM * SLAB, 128), lambda i: (i, 0)),
                   pl.BlockSpec((ROUTER_COLS, OUT_BM), lambda i: (0, i))],
        out_shape=[jax.ShapeDtypeStruct((SEQ, D_MODEL), jnp.float32),
                   jax.ShapeDtypeStruct((SEQ * SLAB, 128), jnp.uint32),
                   jax.ShapeDtypeStruct((ROUTER_COLS, SEQ), jnp.float32)],
        compiler_params=pltpu.CompilerParams(
            dimension_semantics=("arbitrary",),
            vmem_limit_bytes=VMEM_LIMIT_BYTES),
        name="outproj",
    )(ya, yg, x2d, w_out_b, b_out, ln_g, ln_b, w_router, b_router)


def _expert_kernel(blk_ref, pos_ref,
                   x1p_hbm, wg_hbm, wu_hbm, wd_hbm, out_ref, xbuf, wgb, wub, wdb, tok_ref, gsem, wsem):
    b = pl.program_id(0)
    n_used = blk_ref[BLK_USED, 0]

    def weight_copies(e, s):
        return (pltpu.make_async_copy(wg_hbm.at[e], wgb.at[s], wsem.at[s, 0]),
                pltpu.make_async_copy(wu_hbm.at[e], wub.at[s], wsem.at[s, 1]),
                pltpu.make_async_copy(wd_hbm.at[e], wdb.at[s], wsem.at[s, 2]))

    def gather_start(blk, s):
        def issue(r, carry):
            src = pl.multiple_of(tok_ref[blk * ROW_BLOCK + r] * SLAB, SLAB)
            dst = pl.multiple_of(r * SLAB, SLAB)
            pltpu.make_async_copy(x1p_hbm.at[pl.ds(src, SLAB)], xbuf.at[s, pl.ds(dst, SLAB)],
                                  gsem.at[s]).start()
            return carry
        lax.fori_loop(0, ROW_BLOCK, issue, 0, unroll=8)

    def gather_wait(s):
        pltpu.make_async_copy(x1p_hbm.at[pl.ds(0, ROW_BLOCK * SLAB)], xbuf.at[s], gsem.at[s]).wait()

    @pl.when(b == 0)
    def _():
        for c in weight_copies(blk_ref[BLK_EXPERT, 0], 0):
            c.start(priority=1)

        def clear(r, carry):
            tok_ref[r] = 0
            return carry
        lax.fori_loop(0, N_BLOCKS * ROW_BLOCK, clear, 0, unroll=32)

        def scatter(t, carry):
            for k in range(TOP_K):
                tok_ref[pos_ref[k, t]] = t
            return carry
        lax.fori_loop(0, SEQ, scatter, 0, unroll=8)
        gather_start(0, 0)

    @pl.when(b < n_used)
    def _():
        ws = blk_ref[BLK_SLOT, b]
        xs = lax.rem(b, 2)

        @pl.when(blk_ref[BLK_FIRST, b] == 1)
        def _():
            @pl.when(blk_ref[BLK_NEXT, b] >= 0)
            def _():
                for c in weight_copies(blk_ref[BLK_NEXT, b], 1 - ws):
                    c.start(priority=1)
            for c in weight_copies(blk_ref[BLK_EXPERT, b], ws):
                c.wait()

        @pl.when(b + 1 < n_used)
        def _():
            gather_start(b + 1, 1 - xs)

        gather_wait(xs)
        xb = _load_packed_rows(xbuf.at[xs], ROW_BLOCK).astype(jnp.bfloat16)
        g = jnp.dot(xb, wgb[ws], preferred_element_type=jnp.float32)
        u = jnp.dot(xb, wub[ws], preferred_element_type=jnp.float32)
        h = jax.nn.silu(g) * u
        _store_packed_rows(out_ref, jnp.dot(h, wdb[ws], preferred_element_type=jnp.float32))

    @pl.when(b >= n_used)
    def _():
        out_ref[...] = jnp.zeros(out_ref.shape, out_ref.dtype)


def _expert_call(blk, pos, x1p, w_gate, w_up, w_down):
    any_spec = pl.BlockSpec(memory_space=pl.ANY)
    grid_spec = pltpu.PrefetchScalarGridSpec(
        num_scalar_prefetch=2,
        grid=(N_BLOCKS,),
        in_specs=[any_spec, any_spec, any_spec, any_spec],
        out_specs=pl.BlockSpec((ROW_BLOCK * SLAB, 128), lambda b, *_: (b, 0)),
        scratch_shapes=[pltpu.VMEM((2, ROW_BLOCK * SLAB, 128), jnp.uint32),
                        pltpu.VMEM((2, D_MODEL, D_EXPERT), jnp.float32),
                        pltpu.VMEM((2, D_MODEL, D_EXPERT), jnp.float32),
                        pltpu.VMEM((2, D_EXPERT, D_MODEL), jnp.float32),
                        pltpu.SMEM((N_BLOCKS * ROW_BLOCK,), jnp.int32),
                        pltpu.SemaphoreType.DMA((2,)),
                        pltpu.SemaphoreType.DMA((2, 3))],
    )
    return pl.pallas_call(
        _expert_kernel,
        grid_spec=grid_spec,
        out_shape=jax.ShapeDtypeStruct((N_BLOCKS * ROW_BLOCK * SLAB, 128), jnp.uint32),
        compiler_params=pltpu.CompilerParams(
            dimension_semantics=("arbitrary",),
            vmem_limit_bytes=VMEM_LIMIT_BYTES),
        name="experts",
    )(blk, pos, x1p, w_gate, w_up, w_down)


def _combine_kernel(pos_ref, ys_hbm, x1_ref, gate_ref, g_ref, b_ref, out_ref, ybuf, sem):
    i = pl.program_id(0)
    n_steps = pl.num_programs(0)
    rows = TOP_K * COMB_BM

    def gather_start(tile, s):
        def issue(r, carry):
            for k in range(TOP_K):
                src = pl.multiple_of(pos_ref[k, tile * COMB_BM + r] * SLAB, SLAB)
                dst = pl.multiple_of((k * COMB_BM + r) * SLAB, SLAB)
                pltpu.make_async_copy(ys_hbm.at[pl.ds(src, SLAB)], ybuf.at[s, pl.ds(dst, SLAB)],
                                      sem.at[s]).start(priority=k % 2)
            return carry
        lax.fori_loop(0, COMB_BM, issue, 0, unroll=4)

    @pl.when(i == 0)
    def _():
        gather_start(0, 0)

    s = lax.rem(i, 2)

    @pl.when(i + 1 < n_steps)
    def _():
        gather_start(i + 1, 1 - s)

    pltpu.make_async_copy(ys_hbm.at[pl.ds(0, rows * SLAB)], ybuf.at[s], sem.at[s]).wait()
    gate = gate_ref[...]
    y0 = _load_packed_rows(ybuf.at[s, pl.ds(0, COMB_BM * SLAB)], COMB_BM)
    y1 = _load_packed_rows(ybuf.at[s, pl.ds(COMB_BM * SLAB, COMB_BM * SLAB)], COMB_BM)
    ffn = y0 * gate[:, 0:1] + y1 * gate[:, 1:2]
    z = ALPHA * x1_ref[...] + ffn
    mu = jnp.mean(z, axis=-1, keepdims=True)
    d = z - mu
    var = jnp.mean(d * d, axis=-1, keepdims=True)
    out_ref[...] = d * lax.rsqrt(var + LN_EPS) * g_ref[...] + b_ref[...]


def _combine_call(pos, ys, x1, gates, ln_g, ln_b):
    row_full = pl.BlockSpec((COMB_BM, D_MODEL), lambda i, pos: (i, 0))
    vec = pl.BlockSpec((1, D_MODEL), lambda i, pos: (0, 0))
    grid_spec = pltpu.PrefetchScalarGridSpec(
        num_scalar_prefetch=1,
        grid=(SEQ // COMB_BM,),
        in_specs=[
            pl.BlockSpec(memory_space=pl.ANY),
            row_full,
            pl.BlockSpec((COMB_BM, TOP_K), lambda i, pos: (i, 0)),
            vec, vec,
        ],
        out_specs=row_full,
        scratch_shapes=[pltpu.VMEM((2, TOP_K * COMB_BM * SLAB, 128), jnp.uint32),
                        pltpu.SemaphoreType.DMA((2,))],
    )
    return pl.pallas_call(
        _combine_kernel,
        grid_spec=grid_spec,
        out_shape=jax.ShapeDtypeStruct((SEQ, D_MODEL), jnp.float32),
        compiler_params=pltpu.CompilerParams(
            dimension_semantics=("arbitrary",),
            vmem_limit_bytes=VMEM_LIMIT_BYTES),
        name="combine",
    )(pos, ys, x1, gates, ln_g, ln_b)


def _route_kernel(lt_ref, gates_ref, pos_ref, blk_ref, sel_ref, rank_ref):
    f32, i32 = jnp.float32, jnp.int32
    neg = jnp.float32(-jnp.inf)
    n_chunks = SEQ // ROUTE_TC
    erow = lax.broadcasted_iota(i32, (N_EXPERTS, ROUTE_TC), 0)
    grow = lax.broadcasted_iota(i32, (N_GROUPS, ROUTE_TC), 0)
    src_tok = lax.broadcasted_iota(i32, (ROUTE_TC, ROUTE_TC), 0)
    dst_tok = lax.broadcasted_iota(i32, (ROUTE_TC, ROUTE_TC), 1)
    before = (src_tok < dst_tok).astype(jnp.bfloat16)

    def first_index(mask, idx, size):
        return jnp.min(jnp.where(mask, idx, size), axis=0, keepdims=True)

    counts = jnp.zeros((N_EXPERTS, 1), f32)
    for c in range(n_chunks):
        lanes = slice(c * ROUTE_TC, (c + 1) * ROUTE_TC)
        gl = lt_ref[0:N_GROUPS, lanes]
        g_max = jnp.max(gl, axis=0, keepdims=True)
        g_sel = first_index(gl == g_max, grow, N_GROUPS)
        g_weight = 1.0 / jnp.sum(jnp.exp(gl - g_max), axis=0, keepdims=True)
        el = lt_ref[N_GROUPS:N_GROUPS + N_EXPERTS, lanes]
        in_group = (erow >= g_sel * EXPERTS_PER_GROUP) & (erow < (g_sel + 1) * EXPERTS_PER_GROUP)
        el = jnp.where(in_group, el, neg)
        t1 = jnp.max(el, axis=0, keepdims=True)
        e1 = first_index(el == t1, erow, N_EXPERTS)
        el2 = jnp.where(erow == e1, neg, el)
        t2 = jnp.max(el2, axis=0, keepdims=True)
        e2 = first_index(el2 == t2, erow, N_EXPERTS)
        r = jnp.exp(t2 - t1)
        gates_ref[0:1, lanes] = g_weight * (1.0 / (1.0 + r))
        gates_ref[1:2, lanes] = g_weight * (r / (1.0 + r))
        oh1 = erow == e1
        oh2 = erow == e2
        hits = oh1.astype(jnp.bfloat16) + oh2.astype(jnp.bfloat16)
        ahead = jnp.dot(hits, before, preferred_element_type=f32) + counts
        rank_ref[0:1, lanes] = jnp.sum(jnp.where(oh1, ahead, 0.0), axis=0, keepdims=True)
        rank_ref[1:2, lanes] = jnp.sum(jnp.where(oh2, ahead, 0.0), axis=0, keepdims=True)
        sel_ref[0:1, lanes] = e1
        sel_ref[1:2, lanes] = e2
        counts = counts + jnp.sum(hits.astype(f32), axis=1, keepdims=True)

    n_blk = lax.shift_right_logical(counts.astype(i32) + (ROW_BLOCK - 1), ROW_BLOCK.bit_length() - 1)
    e_src = lax.broadcasted_iota(i32, (N_EXPERTS, N_EXPERTS), 1)
    e_dst = lax.broadcasted_iota(i32, (N_EXPERTS, N_EXPERTS), 0)
    incl = (e_src <= e_dst).astype(jnp.bfloat16)
    n_blk_b = jnp.broadcast_to(n_blk.astype(f32), (N_EXPERTS, 128)).astype(jnp.bfloat16)
    blk_end = jnp.dot(incl, n_blk_b, preferred_element_type=f32)[:, 0:1].astype(i32)
    blk_start = blk_end - n_blk
    row_start = (blk_start * ROW_BLOCK).astype(f32)

    for c in range(n_chunks):
        lanes = slice(c * ROUTE_TC, (c + 1) * ROUTE_TC)
        for k in range(TOP_K):
            oh = erow == sel_ref[k:k + 1, lanes]
            start = jnp.sum(jnp.where(oh, row_start, 0.0), axis=0, keepdims=True)
            pos_ref[k:k + 1, lanes] = (start + rank_ref[k:k + 1, lanes]).astype(i32)

    b_lane = lax.broadcasted_iota(i32, (N_EXPERTS, BLK_LANES), 1)
    e_sub = lax.broadcasted_iota(i32, (N_EXPERTS, BLK_LANES), 0)
    used = n_blk > 0
    block_expert = jnp.minimum(jnp.sum((blk_end <= b_lane).astype(i32), axis=0, keepdims=True), N_EXPERTS - 1)
    first = jnp.sum((used & (blk_start == b_lane)).astype(i32), axis=0, keepdims=True)
    run_index = jnp.sum((used & (e_sub < block_expert)).astype(i32), axis=0, keepdims=True)
    nxt = jnp.min(jnp.where(used & (e_sub > block_expert), e_sub, N_EXPERTS), axis=0, keepdims=True)
    nxt = jnp.where(nxt >= N_EXPERTS, -1, nxt)
    n_used = jnp.broadcast_to(blk_end[N_EXPERTS - 1:N_EXPERTS, :], (1, BLK_LANES))
    blk_ref[...] = jnp.concatenate(
        [block_expert, first, run_index & 1, nxt, n_used, jnp.zeros((3, BLK_LANES), i32)], axis=0)


def _route_call(logits_t):
    return pl.pallas_call(
        _route_kernel,
        out_shape=[jax.ShapeDtypeStruct((TOP_K, SEQ), jnp.float32),
                   jax.ShapeDtypeStruct((TOP_K, SEQ), jnp.int32),
                   jax.ShapeDtypeStruct((8, BLK_LANES), jnp.int32)],
        scratch_shapes=[pltpu.VMEM((TOP_K, SEQ), jnp.int32),
                        pltpu.VMEM((TOP_K, SEQ), jnp.float32)],
        compiler_params=pltpu.CompilerParams(vmem_limit_bytes=VMEM_LIMIT_BYTES),
        name="route",
    )(logits_t)


def kernel(x, w_in, b_in, conv_w, conv_b, conv_norm_g, conv_norm_b, gmlp_norm_g, gmlp_norm_b,
           w_spatial, b_spatial, w_out, b_out, ln1_g, ln1_b, w_router_group, b_router_group,
           w_router_expert, b_router_expert, w_expert_gate, w_expert_up, w_expert_down,
           ln2_g, ln2_b):
    assert x.shape == (1, SEQ, D_MODEL) and w_in.shape[0] == 1
    x2d = x.reshape(SEQ, D_MODEL)
    ya, yg = _mixer_call(
        x2d, w_in[0].astype(jnp.bfloat16), b_in, conv_w[0], conv_b, conv_norm_g, conv_norm_b,
        gmlp_norm_g, gmlp_norm_b, w_spatial[0], b_spatial[0][:, :, None])

    pad = ROUTER_COLS - N_GROUPS - N_EXPERTS
    w_router = jnp.concatenate(
        [w_router_group[0],
         jnp.transpose(w_router_expert[0], (1, 0, 2)).reshape(D_MODEL, N_EXPERTS),
         jnp.zeros((D_MODEL, pad), jnp.float32)], axis=1)
    b_router = jnp.concatenate(
        [b_router_group[0], b_router_expert[0].reshape(-1), jnp.zeros((pad,), jnp.float32)])[None, :]
    x1, x1p, logits_t = _outproj_call(ya, yg, x2d, w_out[0].astype(jnp.bfloat16), b_out, ln1_g, ln1_b,
                                    w_router, b_router)

    gates_t, pos, blk = _route_call(logits_t)
    ys = _expert_call(blk, pos, x1p, w_expert_gate[0], w_expert_up[0], w_expert_down[0])
    out = _combine_call(pos, ys, x1, gates_t.T, ln2_g, ln2_b)
    return out.reshape(1, SEQ, D_MODEL)
```

```python
import jax
import jax.numpy as jnp
from jax import lax
from jax.experimental import pallas as pl
from jax.experimental.pallas import tpu as pltpu

D_MODEL = 4096
SEQ = 8192
D_CONV = D_MODEL // 2
D_GMLP = D_MODEL // 2
CONV_WIDTH = 31
GROUP_DIM = 128
GMLP_HEADS = 16
CHUNK = 128
N_GROUPS = 8
EXPERTS_PER_GROUP = 8
N_EXPERTS = N_GROUPS * EXPERTS_PER_GROUP
TOP_K = 2
D_EXPERT = D_MODEL // 8
ROW_BLOCK = 128
LN_EPS = 1e-5
ALPHA = 2.0 ** 0.25

N_ROWS = SEQ * TOP_K
N_BLOCKS = N_ROWS // ROW_BLOCK + N_EXPERTS
ROUTER_COLS = 128
SLAB = D_MODEL // 2 // 128

VMEM_LIMIT_BYTES = 60 * 1024 * 1024

MIX_BM = 1024
MIX_SUB = 256
MIX_BN = 256
HALO = 32
CONV_ROWS = 128
SHIFT_ROWS = CONV_ROWS + HALO - 8

OUT_BM = 256
ROUTE_TC = 512
BLK_LANES = 256
BLK_EXPERT, BLK_FIRST, BLK_SLOT, BLK_NEXT, BLK_USED = range(5)
COMB_BM = 128


def _group_norm(v, g, b):
    outs = []
    for s in range(v.shape[1] // GROUP_DIM):
        blk = v[:, s * GROUP_DIM:(s + 1) * GROUP_DIM]
        mu = jnp.mean(blk, axis=-1, keepdims=True)
        d = blk - mu
        var = jnp.mean(d * d, axis=-1, keepdims=True)
        outs.append(d * lax.rsqrt(var + LN_EPS))
    return jnp.concatenate(outs, axis=-1) * g + b


def _mixer_kernel(x_ref, wa_ref, wg_ref, wu_ref, wv_ref, ba_ref, bg_ref, bu_ref, bv_ref,
                  cw_ref, cb_ref, cng_ref, cnb_ref, gng_ref, gnb_ref, wsp_ref, bsp_ref,
                  ya_ref, yg_ref, xs_ref, r_ref, sh_ref):
    i = pl.program_id(1)
    f32 = jnp.float32
    bf16 = jnp.bfloat16
    w_refs = (wa_ref, wg_ref, wu_ref, wv_ref)
    off = HALO - (CONV_WIDTH - 1)

    @pl.when(i == 0)
    def _():
        xs_ref[0:HALO, :] = jnp.zeros((HALO, MIX_BN), f32)

    row = lax.broadcasted_iota(jnp.int32, (CHUNK, CHUNK), 0)
    col = lax.broadcasted_iota(jnp.int32, (CHUNK, CHUNK), 1)
    w_sp = [jnp.where(row >= col, wsp_ref[h], 0.0).astype(bf16) for h in range(MIX_BN // GROUP_DIM)]

    def project(k):
        xk = x_ref[k * MIX_SUB:(k + 1) * MIX_SUB, :].astype(bf16)
        for q in range(4):
            r_ref[k % 2, q] = jnp.dot(xk, w_refs[q][...], preferred_element_type=f32)

    def finish_conv(k, c):
        t0 = k * MIX_SUB + c * CONV_ROWS
        rows = slice(c * CONV_ROWS, (c + 1) * CONV_ROWS)
        a = r_ref[k % 2, 0, rows, :] + ba_ref[...]
        gate = r_ref[k % 2, 1, rows, :] + bg_ref[...]
        xs_ref[HALO + t0:HALO + t0 + CONV_ROWS, :] = a * jax.nn.sigmoid(gate)
        sh = sh_ref.at[c % 2]
        for m in range(1, 8):
            sh[m - 1] = xs_ref[t0 + m:t0 + m + SHIFT_ROWS, :]
        acc = jnp.zeros((CONV_ROWS, MIX_BN), f32) + cb_ref[...]
        for k2 in range(CONV_WIDTH):
            m, j8 = (off + k2) % 8, 8 * ((off + k2) // 8)
            if m == 0:
                src = xs_ref[t0 + j8:t0 + j8 + CONV_ROWS, :]
            else:
                src = sh[m - 1, j8:j8 + CONV_ROWS, :]
            acc = acc + cw_ref[k2:k2 + 1, :] * src
        y = jax.nn.silu(_group_norm(acc, cng_ref[...], cnb_ref[...]))
        ya_ref[t0:t0 + CONV_ROWS, :] = y.astype(ya_ref.dtype)

    def finish_gmlp(k):
        base = k * MIX_SUB
        n_chunks = MIX_SUB // CHUNK
        us, vs = [], []
        for c in range(n_chunks):
            rows = slice(c * CHUNK, (c + 1) * CHUNK)
            us.append(jax.nn.gelu(r_ref[k % 2, 2, rows, :] + bu_ref[...]))
            v = jax.nn.gelu(r_ref[k % 2, 3, rows, :] + bv_ref[...])
            vs.append(_group_norm(v, gng_ref[...], gnb_ref[...]).astype(bf16))
        for h in range(MIX_BN // GROUP_DIM):
            lanes = slice(h * GROUP_DIM, (h + 1) * GROUP_DIM)
            v_h = jnp.concatenate([v[:, lanes] for v in vs], axis=1)
            s_h = jnp.dot(w_sp[h], v_h, preferred_element_type=f32) + bsp_ref[h]
            for c in range(n_chunks):
                g = us[c][:, lanes] * s_h[:, c * GROUP_DIM:(c + 1) * GROUP_DIM]
                yg_ref[base + c * CHUNK:base + (c + 1) * CHUNK, lanes] = g.astype(yg_ref.dtype)

    def finish(k):
        for c in range(MIX_SUB // CONV_ROWS):
            finish_conv(k, c)
        finish_gmlp(k)

    n_sub = MIX_BM // MIX_SUB
    for k in range(n_sub + 1):
        if k < n_sub:
            project(k)
        if k > 0:
            finish(k - 1)
    xs_ref[0:HALO, :] = xs_ref[MIX_BM:MIX_BM + HALO, :]


def _mixer_call(x2d, w_in_b, b_in, conv_w, conv_b, cng, cnb, gng, gnb, w_spatial, b_spatial_col):
    nq = D_CONV // MIX_BN
    grid = (nq, SEQ // MIX_BM)

    def wspec(q):
        return pl.BlockSpec((D_MODEL, MIX_BN), lambda j, i, q=q: (0, q * nq + j))

    def bspec(q):
        return pl.BlockSpec((1, MIX_BN), lambda j, i, q=q: (0, q * nq + j))

    vec = pl.BlockSpec((1, MIX_BN), lambda j, i: (0, j))
    heads_per_step = MIX_BN // GROUP_DIM
    in_specs = [
        pl.BlockSpec((MIX_BM, D_MODEL), lambda j, i: (i, 0)),
        wspec(0), wspec(1), wspec(2), wspec(3),
        bspec(0), bspec(1), bspec(2), bspec(3),
        pl.BlockSpec((CONV_WIDTH, MIX_BN), lambda j, i: (0, j)),
        vec, vec, vec, vec, vec,
        pl.BlockSpec((heads_per_step, CHUNK, CHUNK), lambda j, i: (j, 0, 0)),
        pl.BlockSpec((heads_per_step, CHUNK, 1), lambda j, i: (j, 0, 0)),
    ]
    out_spec = pl.BlockSpec((MIX_BM, MIX_BN), lambda j, i: (i, j))
    return pl.pallas_call(
        _mixer_kernel,
        grid=grid,
        in_specs=in_specs,
        out_specs=[out_spec, out_spec],
        out_shape=[jax.ShapeDtypeStruct((SEQ, D_CONV), jnp.bfloat16),
                   jax.ShapeDtypeStruct((SEQ, D_GMLP), jnp.bfloat16)],
        scratch_shapes=[pltpu.VMEM((HALO + MIX_BM, MIX_BN), jnp.float32),
                        pltpu.VMEM((2, 4, MIX_SUB, MIX_BN), jnp.float32),
                        pltpu.VMEM((2, 7, SHIFT_ROWS, MIX_BN), jnp.float32)],
        compiler_params=pltpu.CompilerParams(
            dimension_semantics=("arbitrary", "arbitrary"),
            vmem_limit_bytes=VMEM_LIMIT_BYTES),
        name="mixer",
    )(x2d, w_in_b, w_in_b, w_in_b, w_in_b, b_in, b_in, b_in, b_in,
      conv_w, conv_b, cng, cnb, gng, gnb, w_spatial, b_spatial_col)


def _store_packed_rows(dst_ref, x):
    f32, u32 = jnp.float32, jnp.uint32
    half = D_MODEL // 2
    hi = lax.bitcast_convert_type(x[:, :half].astype(jnp.bfloat16).astype(f32), u32)
    lo = lax.bitcast_convert_type(x[:, half:].astype(jnp.bfloat16).astype(f32), u32)
    words = hi | (lo >> 16)
    for j in range(SLAB):
        dst_ref[pl.ds(j, x.shape[0], stride=SLAB), :] = words[:, j * 128:(j + 1) * 128]


def _load_packed_rows(src_ref, rows):
    f32, u32 = jnp.float32, jnp.uint32
    his, los = [], []
    for j in range(SLAB):
        w = src_ref[pl.ds(j, rows, stride=SLAB), :]
        his.append(lax.bitcast_convert_type(w & u32(0xFFFF0000), f32))
        los.append(lax.bitcast_convert_type(w << 16, f32))
    return jnp.concatenate(his + los, axis=1)


def _outproj_kernel(ya_ref, yg_ref, x_ref, w_ref, bo_ref, g_ref, b_ref, wr_ref, br_ref,
                    x1p_ref, logit_ref):
    f32 = jnp.float32
    z = jnp.dot(ya_ref[...], w_ref[0:D_CONV, :], preferred_element_type=f32)
    z = z + jnp.dot(yg_ref[...], w_ref[D_CONV:, :], preferred_element_type=f32)
    z = z + bo_ref[...] + ALPHA * x_ref[...]
    mu = jnp.mean(z, axis=-1, keepdims=True)
    d = z - mu
    var = jnp.mean(d * d, axis=-1, keepdims=True)
    x1 = d * lax.rsqrt(var + LN_EPS) * g_ref[...] + b_ref[...]
    _store_packed_rows(x1p_ref, x1)
    wr = wr_ref[...]
    wr_hi = wr.astype(jnp.bfloat16)
    wr_lo = (wr - wr_hi.astype(f32)).astype(jnp.bfloat16)
    x_hi = x1.astype(jnp.bfloat16)
    x_lo = (x1 - x_hi.astype(f32)).astype(jnp.bfloat16)
    logits = jnp.dot(x_hi, wr_hi, preferred_element_type=f32)
    logits = logits + (jnp.dot(x_lo, wr_hi, preferred_element_type=f32)
                       + jnp.dot(x_hi, wr_lo, preferred_element_type=f32))
    logit_ref[...] = (logits + br_ref[...]).T


def _outproj_call(ya, yg, x2d, w_out_b, b_out, ln_g, ln_b, w_router, b_router):
    grid = (SEQ // OUT_BM,)
    row_half = pl.BlockSpec((OUT_BM, D_CONV), lambda i: (i, 0))
    row_full = pl.BlockSpec((OUT_BM, D_MODEL), lambda i: (i, 0))
    vec = pl.BlockSpec((1, D_MODEL), lambda i: (0, 0))
    in_specs = [
        row_half, row_half, row_full,
        pl.BlockSpec((D_MODEL, D_MODEL), lambda i: (0, 0), pipeline_mode=pl.Buffered(1)),
        vec, vec, vec,
        pl.BlockSpec((D_MODEL, ROUTER_COLS), lambda i: (0, 0), pipeline_mode=pl.Buffered(1)),
        pl.BlockSpec((1, ROUTER_COLS), lambda i: (0, 0)),
    ]
    return pl.pallas_call(
        _outproj_kernel,
        grid=grid,
        in_specs=in_specs,
        out_specs=[pl.BlockSpec((OUT_BM * SLAB, 128), lambda i: (i, 0)),
                   pl.BlockSpec((ROUTER_COLS, OUT_BM), lambda i: (0, i))],
        out_shape=[jax.ShapeDtypeStruct((SEQ * SLAB, 128), jnp.uint32),
                   jax.ShapeDtypeStruct((ROUTER_COLS, SEQ), jnp.float32)],
        compiler_params=pltpu.CompilerParams(
            dimension_semantics=("arbitrary",),
            vmem_limit_bytes=VMEM_LIMIT_BYTES),
        name="outproj",
    )(ya, yg, x2d, w_out_b, b_out, ln_g, ln_b, w_router, b_router)


def _expert_kernel(blk_ref, pos_ref,
                   x1p_hbm, wg_hbm, wu_hbm, wd_hbm, out_ref, xbuf, wgb, wub, wdb, tok_ref, gsem, wsem):
    b = pl.program_id(0)
    n_used = blk_ref[BLK_USED, 0]

    def weight_copies(e, s):
        return (pltpu.make_async_copy(wg_hbm.at[e], wgb.at[s], wsem.at[s, 0]),
                pltpu.make_async_copy(wu_hbm.at[e], wub.at[s], wsem.at[s, 1]),
                pltpu.make_async_copy(wd_hbm.at[e], wdb.at[s], wsem.at[s, 2]))

    def gather_start(blk, s):
        def issue(r, carry):
            src = pl.multiple_of(tok_ref[blk * ROW_BLOCK + r] * SLAB, SLAB)
            dst = pl.multiple_of(r * SLAB, SLAB)
            pltpu.make_async_copy(x1p_hbm.at[pl.ds(src, SLAB)], xbuf.at[s, pl.ds(dst, SLAB)],
                                  gsem.at[s]).start()
            return carry
        lax.fori_loop(0, ROW_BLOCK, issue, 0, unroll=8)

    def gather_wait(s):
        pltpu.make_async_copy(x1p_hbm.at[pl.ds(0, ROW_BLOCK * SLAB)], xbuf.at[s], gsem.at[s]).wait()

    @pl.when(b == 0)
    def _():
        for c in weight_copies(blk_ref[BLK_EXPERT, 0], 0):
            c.start(priority=1)

        def clear(r, carry):
            tok_ref[r] = 0
            return carry
        lax.fori_loop(0, N_BLOCKS * ROW_BLOCK, clear, 0, unroll=32)

        def scatter(t, carry):
            for k in range(TOP_K):
                tok_ref[pos_ref[k, t]] = t
            return carry
        lax.fori_loop(0, SEQ, scatter, 0, unroll=8)
        gather_start(0, 0)

    @pl.when(b < n_used)
    def _():
        ws = blk_ref[BLK_SLOT, b]
        xs = lax.rem(b, 2)

        @pl.when(blk_ref[BLK_FIRST, b] == 1)
        def _():
            @pl.when(blk_ref[BLK_NEXT, b] >= 0)
            def _():
                for c in weight_copies(blk_ref[BLK_NEXT, b], 1 - ws):
                    c.start(priority=1)
            for c in weight_copies(blk_ref[BLK_EXPERT, b], ws):
                c.wait()

        @pl.when(b + 1 < n_used)
        def _():
            gather_start(b + 1, 1 - xs)

        gather_wait(xs)
        xb = _load_packed_rows(xbuf.at[xs], ROW_BLOCK).astype(jnp.bfloat16)
        g = jnp.dot(xb, wgb[ws], preferred_element_type=jnp.float32)
        u = jnp.dot(xb, wub[ws], preferred_element_type=jnp.float32)
        h = jax.nn.silu(g) * u
        _store_packed_rows(out_ref, jnp.dot(h, wdb[ws], preferred_element_type=jnp.float32))

    @pl.when(b >= n_used)
    def _():
        out_ref[...] = jnp.zeros(out_ref.shape, out_ref.dtype)


def _expert_call(blk, pos, x1p, w_gate, w_up, w_down):
    any_spec = pl.BlockSpec(memory_space=pl.ANY)
    grid_spec = pltpu.PrefetchScalarGridSpec(
        num_scalar_prefetch=2,
        grid=(N_BLOCKS,),
        in_specs=[any_spec, any_spec, any_spec, any_spec],
        out_specs=pl.BlockSpec((ROW_BLOCK * SLAB, 128), lambda b, *_: (b, 0)),
        scratch_shapes=[pltpu.VMEM((2, ROW_BLOCK * SLAB, 128), jnp.uint32),
                        pltpu.VMEM((2, D_MODEL, D_EXPERT), jnp.float32),
                        pltpu.VMEM((2, D_MODEL, D_EXPERT), jnp.float32),
                        pltpu.VMEM((2, D_EXPERT, D_MODEL), jnp.float32),
                        pltpu.SMEM((N_BLOCKS * ROW_BLOCK,), jnp.int32),
                        pltpu.SemaphoreType.DMA((2,)),
                        pltpu.SemaphoreType.DMA((2, 3))],
    )
    return pl.pallas_call(
        _expert_kernel,
        grid_spec=grid_spec,
        out_shape=jax.ShapeDtypeStruct((N_BLOCKS * ROW_BLOCK * SLAB, 128), jnp.uint32),
        compiler_params=pltpu.CompilerParams(
            dimension_semantics=("arbitrary",),
            vmem_limit_bytes=VMEM_LIMIT_BYTES),
        name="experts",
    )(blk, pos, x1p, w_gate, w_up, w_down)


def _combine_kernel(pos_ref, ys_hbm, x1p_ref, gate_ref, g_ref, b_ref, out_ref, ybuf, sem):
    i = pl.program_id(0)
    n_steps = pl.num_programs(0)
    rows = TOP_K * COMB_BM

    def gather_start(tile, s):
        def issue(r, carry):
            for k in range(TOP_K):
                src = pl.multiple_of(pos_ref[k, tile * COMB_BM + r] * SLAB, SLAB)
                dst = pl.multiple_of((k * COMB_BM + r) * SLAB, SLAB)
                pltpu.make_async_copy(ys_hbm.at[pl.ds(src, SLAB)], ybuf.at[s, pl.ds(dst, SLAB)],
                                      sem.at[s]).start(priority=k % 2)
            return carry
        lax.fori_loop(0, COMB_BM, issue, 0, unroll=4)

    @pl.when(i == 0)
    def _():
        gather_start(0, 0)

    s = lax.rem(i, 2)

    @pl.when(i + 1 < n_steps)
    def _():
        gather_start(i + 1, 1 - s)

    pltpu.make_async_copy(ys_hbm.at[pl.ds(0, rows * SLAB)], ybuf.at[s], sem.at[s]).wait()
    gate = gate_ref[...]
    y0 = _load_packed_rows(ybuf.at[s, pl.ds(0, COMB_BM * SLAB)], COMB_BM)
    y1 = _load_packed_rows(ybuf.at[s, pl.ds(COMB_BM * SLAB, COMB_BM * SLAB)], COMB_BM)
    ffn = y0 * gate[:, 0:1] + y1 * gate[:, 1:2]
    z = ALPHA * _load_packed_rows(x1p_ref, COMB_BM) + ffn
    mu = jnp.mean(z, axis=-1, keepdims=True)
    d = z - mu
    var = jnp.mean(d * d, axis=-1, keepdims=True)
    out_ref[...] = d * lax.rsqrt(var + LN_EPS) * g_ref[...] + b_ref[...]


def _combine_call(pos, ys, x1p, gates, ln_g, ln_b):
    row_full = pl.BlockSpec((COMB_BM, D_MODEL), lambda i, pos: (i, 0))
    vec = pl.BlockSpec((1, D_MODEL), lambda i, pos: (0, 0))
    grid_spec = pltpu.PrefetchScalarGridSpec(
        num_scalar_prefetch=1,
        grid=(SEQ // COMB_BM,),
        in_specs=[
            pl.BlockSpec(memory_space=pl.ANY),
            pl.BlockSpec((COMB_BM * SLAB, 128), lambda i, pos: (i, 0)),
            pl.BlockSpec((COMB_BM, TOP_K), lambda i, pos: (i, 0)),
            vec, vec,
        ],
        out_specs=row_full,
        scratch_shapes=[pltpu.VMEM((2, TOP_K * COMB_BM * SLAB, 128), jnp.uint32),
                        pltpu.SemaphoreType.DMA((2,))],
    )
    return pl.pallas_call(
        _combine_kernel,
        grid_spec=grid_spec,
        out_shape=jax.ShapeDtypeStruct((SEQ, D_MODEL), jnp.float32),
        compiler_params=pltpu.CompilerParams(
            dimension_semantics=("arbitrary",),
            vmem_limit_bytes=VMEM_LIMIT_BYTES),
        name="combine",
    )(pos, ys, x1p, gates, ln_g, ln_b)


def _route_kernel(lt_ref, gates_ref, pos_ref, blk_ref, sel_ref, rank_ref):
    f32, i32 = jnp.float32, jnp.int32
    neg = jnp.float32(-jnp.inf)
    n_chunks = SEQ // ROUTE_TC
    erow = lax.broadcasted_iota(i32, (N_EXPERTS, ROUTE_TC), 0)
    grow = lax.broadcasted_iota(i32, (N_GROUPS, ROUTE_TC), 0)
    src_tok = lax.broadcasted_iota(i32, (ROUTE_TC, ROUTE_TC), 0)
    dst_tok = lax.broadcasted_iota(i32, (ROUTE_TC, ROUTE_TC), 1)
    before = (src_tok < dst_tok).astype(jnp.bfloat16)

    def first_index(mask, idx, size):
        return jnp.min(jnp.where(mask, idx, size), axis=0, keepdims=True)

    counts = jnp.zeros((N_EXPERTS, 1), f32)
    for c in range(n_chunks):
        lanes = slice(c * ROUTE_TC, (c + 1) * ROUTE_TC)
        gl = lt_ref[0:N_GROUPS, lanes]
        g_max = jnp.max(gl, axis=0, keepdims=True)
        g_sel = first_index(gl == g_max, grow, N_GROUPS)
        g_weight = 1.0 / jnp.sum(jnp.exp(gl - g_max), axis=0, keepdims=True)
        el = lt_ref[N_GROUPS:N_GROUPS + N_EXPERTS, lanes]
        in_group = (erow >= g_sel * EXPERTS_PER_GROUP) & (erow < (g_sel + 1) * EXPERTS_PER_GROUP)
        el = jnp.where(in_group, el, neg)
        t1 = jnp.max(el, axis=0, keepdims=True)
        e1 = first_index(el == t1, erow, N_EXPERTS)
        el2 = jnp.where(erow == e1, neg, el)
        t2 = jnp.max(el2, axis=0, keepdims=True)
        e2 = first_index(el2 == t2, erow, N_EXPERTS)
        r = jnp.exp(t2 - t1)
        gates_ref[0:1, lanes] = g_weight * (1.0 / (1.0 + r))
        gates_ref[1:2, lanes] = g_weight * (r / (1.0 + r))
        oh1 = erow == e1
        oh2 = erow == e2
        hits = oh1.astype(jnp.bfloat16) + oh2.astype(jnp.bfloat16)
        ahead = jnp.dot(hits, before, preferred_element_type=f32) + counts
        rank_ref[0:1, lanes] = jnp.sum(jnp.where(oh1, ahead, 0.0), axis=0, keepdims=True)
        rank_ref[1:2, lanes] = jnp.sum(jnp.where(oh2, ahead, 0.0), axis=0, keepdims=True)
        sel_ref[0:1, lanes] = e1
        sel_ref[1:2, lanes] = e2
        counts = counts + jnp.sum(hits.astype(f32), axis=1, keepdims=True)

    n_blk = lax.shift_right_logical(counts.astype(i32) + (ROW_BLOCK - 1), ROW_BLOCK.bit_length() - 1)
    e_src = lax.broadcasted_iota(i32, (N_EXPERTS, N_EXPERTS), 1)
    e_dst = lax.broadcasted_iota(i32, (N_EXPERTS, N_EXPERTS), 0)
    incl = (e_src <= e_dst).astype(jnp.bfloat16)
    n_blk_b = jnp.broadcast_to(n_blk.astype(f32), (N_EXPERTS, 128)).astype(jnp.bfloat16)
    blk_end = jnp.dot(incl, n_blk_b, preferred_element_type=f32)[:, 0:1].astype(i32)
    blk_start = blk_end - n_blk
    row_start = (blk_start * ROW_BLOCK).astype(f32)

    for c in range(n_chunks):
        lanes = slice(c * ROUTE_TC, (c + 1) * ROUTE_TC)
        for k in range(TOP_K):
            oh = erow == sel_ref[k:k + 1, lanes]
            start = jnp.sum(jnp.where(oh, row_start, 0.0), axis=0, keepdims=True)
            pos_ref[k:k + 1, lanes] = (start + rank_ref[k:k + 1, lanes]).astype(i32)

    b_lane = lax.broadcasted_iota(i32, (N_EXPERTS, BLK_LANES), 1)
    e_sub = lax.broadcasted_iota(i32, (N_EXPERTS, BLK_LANES), 0)
    used = n_blk > 0
    block_expert = jnp.minimum(jnp.sum((blk_end <= b_lane).astype(i32), axis=0, keepdims=True), N_EXPERTS - 1)
    first = jnp.sum((used & (blk_start == b_lane)).astype(i32), axis=0, keepdims=True)
    run_index = jnp.sum((used & (e_sub < block_expert)).astype(i32), axis=0, keepdims=True)
    nxt = jnp.min(jnp.where(used & (e_sub > block_expert), e_sub, N_EXPERTS), axis=0, keepdims=True)
    nxt = jnp.where(nxt >= N_EXPERTS, -1, nxt)
    n_used = jnp.broadcast_to(blk_end[N_EXPERTS - 1:N_EXPERTS, :], (1, BLK_LANES))
    blk_ref[...] = jnp.concatenate(
        [block_expert, first, run_index & 1, nxt, n_used, jnp.zeros((3, BLK_LANES), i32)], axis=0)


def _route_call(logits_t):
    return pl.pallas_call(
        _route_kernel,
        out_shape=[jax.ShapeDtypeStruct((TOP_K, SEQ), jnp.float32),
                   jax.ShapeDtypeStruct((TOP_K, SEQ), jnp.int32),
                   jax.ShapeDtypeStruct((8, BLK_LANES), jnp.int32)],
        scratch_shapes=[pltpu.VMEM((TOP_K, SEQ), jnp.int32),
                        pltpu.VMEM((TOP_K, SEQ), jnp.float32)],
        compiler_params=pltpu.CompilerParams(vmem_limit_bytes=VMEM_LIMIT_BYTES),
        name="route",
    )(logits_t)


def kernel(x, w_in, b_in, conv_w, conv_b, conv_norm_g, conv_norm_b, gmlp_norm_g, gmlp_norm_b,
           w_spatial, b_spatial, w_out, b_out, ln1_g, ln1_b, w_router_group, b_router_group,
           w_router_expert, b_router_expert, w_expert_gate, w_expert_up, w_expert_down,
           ln2_g, ln2_b):
    assert x.shape == (1, SEQ, D_MODEL) and w_in.shape[0] == 1
    x2d = x.reshape(SEQ, D_MODEL)
    ya, yg = _mixer_call(
        x2d, w_in[0].astype(jnp.bfloat16), b_in, conv_w[0], conv_b, conv_norm_g, conv_norm_b,
        gmlp_norm_g, gmlp_norm_b, w_spatial[0], b_spatial[0][:, :, None])

    pad = ROUTER_COLS - N_GROUPS - N_EXPERTS
    w_router = jnp.concatenate(
        [w_router_group[0],
         jnp.transpose(w_router_expert[0], (1, 0, 2)).reshape(D_MODEL, N_EXPERTS),
         jnp.zeros((D_MODEL, pad), jnp.float32)], axis=1)
    b_router = jnp.concatenate(
        [b_router_group[0], b_router_expert[0].reshape(-1), jnp.zeros((pad,), jnp.float32)])[None, :]
    x1p, logits_t = _outproj_call(ya, yg, x2d, w_out[0].astype(jnp.bfloat16), b_out, ln1_g, ln1_b,
                                    w_router, b_router)

    gates_t, pos, blk = _route_call(logits_t)
    ys = _expert_call(blk, pos, x1p, w_expert_gate[0], w_expert_up[0], w_expert_down[0])
    out = _combine_call(pos, ys, x1p, gates_t.T, ln2_g, ln2_b)
    return out.reshape(1, SEQ, D_MODEL)
```

```python
import jax
import jax.numpy as jnp
from jax import lax
from jax.experimental import pallas as pl
from jax.experimental.pallas import tpu as pltpu

D_MODEL = 4096
SEQ = 8192
D_CONV = D_MODEL // 2
D_GMLP = D_MODEL // 2
CONV_WIDTH = 31
GROUP_DIM = 128
GMLP_HEADS = 16
CHUNK = 128
N_GROUPS = 8
EXPERTS_PER_GROUP = 8
N_EXPERTS = N_GROUPS * EXPERTS_PER_GROUP
TOP_K = 2
D_EXPERT = D_MODEL // 8
ROW_BLOCK = 128
LN_EPS = 1e-5
ALPHA = 2.0 ** 0.25

N_ROWS = SEQ * TOP_K
N_BLOCKS = N_ROWS // ROW_BLOCK + N_EXPERTS
ROUTER_COLS = 128
SLAB = D_MODEL // 2 // 128

VMEM_LIMIT_BYTES = 60 * 1024 * 1024

MIX_BM = 1024
MIX_SUB = 256
MIX_BN = 256
HALO = 32
CONV_ROWS = 128
SHIFT_ROWS = CONV_ROWS + HALO - 8

OUT_BM = 256
ROUTE_TC = 512
BLK_LANES = 256
BLK_EXPERT, BLK_FIRST, BLK_SLOT, BLK_NEXT, BLK_USED = range(5)
COMB_BM = 128
DISP_BM = 128


def _group_norm(v, g, b):
    outs = []
    for s in range(v.shape[1] // GROUP_DIM):
        blk = v[:, s * GROUP_DIM:(s + 1) * GROUP_DIM]
        mu = jnp.mean(blk, axis=-1, keepdims=True)
        d = blk - mu
        var = jnp.mean(d * d, axis=-1, keepdims=True)
        outs.append(d * lax.rsqrt(var + LN_EPS))
    return jnp.concatenate(outs, axis=-1) * g + b


def _mixer_kernel(x_ref, wa_ref, wg_ref, wu_ref, wv_ref, ba_ref, bg_ref, bu_ref, bv_ref,
                  cw_ref, cb_ref, cng_ref, cnb_ref, gng_ref, gnb_ref, wsp_ref, bsp_ref,
                  ya_ref, yg_ref, xs_ref, r_ref, sh_ref):
    i = pl.program_id(1)
    f32 = jnp.float32
    bf16 = jnp.bfloat16
    w_refs = (wa_ref, wg_ref, wu_ref, wv_ref)
    off = HALO - (CONV_WIDTH - 1)

    @pl.when(i == 0)
    def _():
        xs_ref[0:HALO, :] = jnp.zeros((HALO, MIX_BN), f32)

    row = lax.broadcasted_iota(jnp.int32, (CHUNK, CHUNK), 0)
    col = lax.broadcasted_iota(jnp.int32, (CHUNK, CHUNK), 1)
    w_sp = [jnp.where(row >= col, wsp_ref[h], 0.0).astype(bf16) for h in range(MIX_BN // GROUP_DIM)]

    def project(k):
        xk = x_ref[k * MIX_SUB:(k + 1) * MIX_SUB, :].astype(bf16)
        for q in range(4):
            r_ref[k % 2, q] = jnp.dot(xk, w_refs[q][...], preferred_element_type=f32)

    def finish_conv(k, c):
        t0 = k * MIX_SUB + c * CONV_ROWS
        rows = slice(c * CONV_ROWS, (c + 1) * CONV_ROWS)
        a = r_ref[k % 2, 0, rows, :] + ba_ref[...]
        gate = r_ref[k % 2, 1, rows, :] + bg_ref[...]
        xs_ref[HALO + t0:HALO + t0 + CONV_ROWS, :] = a * jax.nn.sigmoid(gate)
        sh = sh_ref.at[c % 2]
        for m in range(1, 8):
            sh[m - 1] = xs_ref[t0 + m:t0 + m + SHIFT_ROWS, :]
        acc = jnp.zeros((CONV_ROWS, MIX_BN), f32) + cb_ref[...]
        for k2 in range(CONV_WIDTH):
            m, j8 = (off + k2) % 8, 8 * ((off + k2) // 8)
            if m == 0:
                src = xs_ref[t0 + j8:t0 + j8 + CONV_ROWS, :]
            else:
                src = sh[m - 1, j8:j8 + CONV_ROWS, :]
            acc = acc + cw_ref[k2:k2 + 1, :] * src
        y = jax.nn.silu(_group_norm(acc, cng_ref[...], cnb_ref[...]))
        ya_ref[t0:t0 + CONV_ROWS, :] = y.astype(ya_ref.dtype)

    def finish_gmlp(k):
        base = k * MIX_SUB
        n_chunks = MIX_SUB // CHUNK
        us, vs = [], []
        for c in range(n_chunks):
            rows = slice(c * CHUNK, (c + 1) * CHUNK)
            us.append(jax.nn.gelu(r_ref[k % 2, 2, rows, :] + bu_ref[...]))
            v = jax.nn.gelu(r_ref[k % 2, 3, rows, :] + bv_ref[...])
            vs.append(_group_norm(v, gng_ref[...], gnb_ref[...]).astype(bf16))
        for h in range(MIX_BN // GROUP_DIM):
            lanes = slice(h * GROUP_DIM, (h + 1) * GROUP_DIM)
            v_h = jnp.concatenate([v[:, lanes] for v in vs], axis=1)
            s_h = jnp.dot(w_sp[h], v_h, preferred_element_type=f32) + bsp_ref[h]
            for c in range(n_chunks):
                g = us[c][:, lanes] * s_h[:, c * GROUP_DIM:(c + 1) * GROUP_DIM]
                yg_ref[base + c * CHUNK:base + (c + 1) * CHUNK, lanes] = g.astype(yg_ref.dtype)

    def finish(k):
        for c in range(MIX_SUB // CONV_ROWS):
            finish_conv(k, c)
        finish_gmlp(k)

    n_sub = MIX_BM // MIX_SUB
    for k in range(n_sub + 1):
        if k < n_sub:
            project(k)
        if k > 0:
            finish(k - 1)
    xs_ref[0:HALO, :] = xs_ref[MIX_BM:MIX_BM + HALO, :]


def _mixer_call(x2d, w_in_b, b_in, conv_w, conv_b, cng, cnb, gng, gnb, w_spatial, b_spatial_col):
    nq = D_CONV // MIX_BN
    grid = (nq, SEQ // MIX_BM)

    def wspec(q):
        return pl.BlockSpec((D_MODEL, MIX_BN), lambda j, i, q=q: (0, q * nq + j))

    def bspec(q):
        return pl.BlockSpec((1, MIX_BN), lambda j, i, q=q: (0, q * nq + j))

    vec = pl.BlockSpec((1, MIX_BN), lambda j, i: (0, j))
    heads_per_step = MIX_BN // GROUP_DIM
    in_specs = [
        pl.BlockSpec((MIX_BM, D_MODEL), lambda j, i: (i, 0)),
        wspec(0), wspec(1), wspec(2), wspec(3),
        bspec(0), bspec(1), bspec(2), bspec(3),
        pl.BlockSpec((CONV_WIDTH, MIX_BN), lambda j, i: (0, j)),
        vec, vec, vec, vec, vec,
        pl.BlockSpec((heads_per_step, CHUNK, CHUNK), lambda j, i: (j, 0, 0)),
        pl.BlockSpec((heads_per_step, CHUNK, 1), lambda j, i: (j, 0, 0)),
    ]
    out_spec = pl.BlockSpec((MIX_BM, MIX_BN), lambda j, i: (i, j))
    return pl.pallas_call(
        _mixer_kernel,
        grid=grid,
        in_specs=in_specs,
        out_specs=[out_spec, out_spec],
        out_shape=[jax.ShapeDtypeStruct((SEQ, D_CONV), jnp.bfloat16),
                   jax.ShapeDtypeStruct((SEQ, D_GMLP), jnp.bfloat16)],
        scratch_shapes=[pltpu.VMEM((HALO + MIX_BM, MIX_BN), jnp.float32),
                        pltpu.VMEM((2, 4, MIX_SUB, MIX_BN), jnp.float32),
                        pltpu.VMEM((2, 7, SHIFT_ROWS, MIX_BN), jnp.float32)],
        compiler_params=pltpu.CompilerParams(
            dimension_semantics=("arbitrary", "arbitrary"),
            vmem_limit_bytes=VMEM_LIMIT_BYTES),
        name="mixer",
    )(x2d, w_in_b, w_in_b, w_in_b, w_in_b, b_in, b_in, b_in, b_in,
      conv_w, conv_b, cng, cnb, gng, gnb, w_spatial, b_spatial_col)


def _store_packed_rows(dst_ref, x):
    f32, u32 = jnp.float32, jnp.uint32
    half = D_MODEL // 2
    hi = lax.bitcast_convert_type(x[:, :half].astype(jnp.bfloat16).astype(f32), u32)
    lo = lax.bitcast_convert_type(x[:, half:].astype(jnp.bfloat16).astype(f32), u32)
    words = hi | (lo >> 16)
    for j in range(SLAB):
        dst_ref[pl.ds(j, x.shape[0], stride=SLAB), :] = words[:, j * 128:(j + 1) * 128]


def _load_packed_rows(src_ref, rows):
    f32, u32 = jnp.float32, jnp.uint32
    his, los = [], []
    for j in range(SLAB):
        w = src_ref[pl.ds(j, rows, stride=SLAB), :]
        his.append(lax.bitcast_convert_type(w & u32(0xFFFF0000), f32))
        los.append(lax.bitcast_convert_type(w << 16, f32))
    return jnp.concatenate(his + los, axis=1)


def _outproj_kernel(ya_ref, yg_ref, x_ref, w_ref, bo_ref, g_ref, b_ref, wr_ref, br_ref,
                    x1p_ref, logit_ref):
    f32 = jnp.float32
    z = jnp.dot(ya_ref[...], w_ref[0:D_CONV, :], preferred_element_type=f32)
    z = z + jnp.dot(yg_ref[...], w_ref[D_CONV:, :], preferred_element_type=f32)
    z = z + bo_ref[...] + ALPHA * x_ref[...]
    mu = jnp.mean(z, axis=-1, keepdims=True)
    d = z - mu
    var = jnp.mean(d * d, axis=-1, keepdims=True)
    x1 = d * lax.rsqrt(var + LN_EPS) * g_ref[...] + b_ref[...]
    _store_packed_rows(x1p_ref, x1)
    wr = wr_ref[...]
    wr_hi = wr.astype(jnp.bfloat16)
    wr_lo = (wr - wr_hi.astype(f32)).astype(jnp.bfloat16)
    x_hi = x1.astype(jnp.bfloat16)
    x_lo = (x1 - x_hi.astype(f32)).astype(jnp.bfloat16)
    logits = jnp.dot(x_hi, wr_hi, preferred_element_type=f32)
    logits = logits + (jnp.dot(x_lo, wr_hi, preferred_element_type=f32)
                       + jnp.dot(x_hi, wr_lo, preferred_element_type=f32))
    logit_ref[...] = (logits + br_ref[...]).T


def _outproj_call(ya, yg, x2d, w_out_b, b_out, ln_g, ln_b, w_router, b_router):
    grid = (SEQ // OUT_BM,)
    row_half = pl.BlockSpec((OUT_BM, D_CONV), lambda i: (i, 0))
    row_full = pl.BlockSpec((OUT_BM, D_MODEL), lambda i: (i, 0))
    vec = pl.BlockSpec((1, D_MODEL), lambda i: (0, 0))
    in_specs = [
        row_half, row_half, row_full,
        pl.BlockSpec((D_MODEL, D_MODEL), lambda i: (0, 0), pipeline_mode=pl.Buffered(1)),
        vec, vec, vec,
        pl.BlockSpec((D_MODEL, ROUTER_COLS), lambda i: (0, 0), pipeline_mode=pl.Buffered(1)),
        pl.BlockSpec((1, ROUTER_COLS), lambda i: (0, 0)),
    ]
    return pl.pallas_call(
        _outproj_kernel,
        grid=grid,
        in_specs=in_specs,
        out_specs=[pl.BlockSpec((OUT_BM * SLAB, 128), lambda i: (i, 0)),
                   pl.BlockSpec((ROUTER_COLS, OUT_BM), lambda i: (0, i))],
        out_shape=[jax.ShapeDtypeStruct((SEQ * SLAB, 128), jnp.uint32),
                   jax.ShapeDtypeStruct((ROUTER_COLS, SEQ), jnp.float32)],
        compiler_params=pltpu.CompilerParams(
            dimension_semantics=("arbitrary",),
            vmem_limit_bytes=VMEM_LIMIT_BYTES),
        name="outproj",
    )(ya, yg, x2d, w_out_b, b_out, ln_g, ln_b, w_router, b_router)


def _dispatch_kernel(pos_ref, pad_ref, x1p_hbm, xs_hbm, buf, zbuf, lsem, ssem, zsem, tsem):
    i = pl.program_id(0)
    n_steps = pl.num_programs(0)
    rows = DISP_BM * SLAB
    n_slots = 3

    def load(tile, slot):
        src = pl.multiple_of(tile * rows, rows)
        return pltpu.make_async_copy(x1p_hbm.at[pl.ds(src, rows)], buf.at[slot], lsem.at[slot])

    def scatter_wait(slot):
        for _ in range(TOP_K):
            pltpu.make_async_copy(buf.at[slot], xs_hbm.at[pl.ds(0, rows)], ssem.at[slot]).wait()

    def pad_copy(r):
        dst = pl.multiple_of(r * SLAB, SLAB)
        return pltpu.make_async_copy(zbuf.at[pl.ds(0, SLAB)], xs_hbm.at[pl.ds(dst, SLAB)], zsem)

    def tail_copy(blk):
        dst = pl.multiple_of(blk * (ROW_BLOCK * SLAB), ROW_BLOCK * SLAB)
        return pltpu.make_async_copy(zbuf, xs_hbm.at[pl.ds(dst, ROW_BLOCK * SLAB)], tsem)

    def for_each_tail_block(fn):
        used_blocks = lax.shift_right_logical(pad_ref[1, N_EXPERTS - 1], ROW_BLOCK.bit_length() - 1)

        def per_block(blk, carry):
            fn(blk)
            return carry
        lax.fori_loop(used_blocks, N_BLOCKS, per_block, 0)

    def for_each_pad_row(fn):
        def per_expert(e, carry):
            def per_row(r, c):
                fn(r)
                return c
            lax.fori_loop(pad_ref[0, e], pad_ref[1, e], per_row, 0)
            return carry
        lax.fori_loop(0, N_EXPERTS, per_expert, 0)

    @pl.when(i == 0)
    def _():
        load(0, 0).start()
        zbuf[...] = jnp.zeros(zbuf.shape, zbuf.dtype)
        for_each_pad_row(lambda r: pad_copy(r).start())
        for_each_tail_block(lambda blk: tail_copy(blk).start())

    slot = lax.rem(i, n_slots)

    @pl.when(i + 1 < n_steps)
    def _():
        load(i + 1, lax.rem(i + 1, n_slots)).start()

    load(i, slot).wait()

    def issue(r, carry):
        src = pl.multiple_of(r * SLAB, SLAB)
        for k in range(TOP_K):
            dst = pl.multiple_of(pos_ref[k, i * DISP_BM + r] * SLAB, SLAB)
            pltpu.make_async_copy(buf.at[slot, pl.ds(src, SLAB)], xs_hbm.at[pl.ds(dst, SLAB)],
                                  ssem.at[slot]).start()
        return carry
    lax.fori_loop(0, DISP_BM, issue, 0, unroll=4)

    @pl.when(i >= 1)
    def _():
        scatter_wait(lax.rem(i + n_slots - 1, n_slots))

    @pl.when(i == n_steps - 1)
    def _():
        scatter_wait(slot)
        for_each_pad_row(lambda r: pad_copy(r).wait())
        for_each_tail_block(lambda blk: tail_copy(blk).wait())


def _dispatch_call(pos, pad, x1p):
    any_spec = pl.BlockSpec(memory_space=pl.ANY)
    grid_spec = pltpu.PrefetchScalarGridSpec(
        num_scalar_prefetch=2,
        grid=(SEQ // DISP_BM,),
        in_specs=[any_spec],
        out_specs=any_spec,
        scratch_shapes=[pltpu.VMEM((3, DISP_BM * SLAB, 128), jnp.uint32),
                        pltpu.VMEM((ROW_BLOCK * SLAB, 128), jnp.uint32),
                        pltpu.SemaphoreType.DMA((3,)),
                        pltpu.SemaphoreType.DMA((3,)),
                        pltpu.SemaphoreType.DMA(()),
                        pltpu.SemaphoreType.DMA(())],
    )
    return pl.pallas_call(
        _dispatch_kernel,
        grid_spec=grid_spec,
        out_shape=jax.ShapeDtypeStruct((N_BLOCKS * ROW_BLOCK * SLAB, 128), jnp.uint32),
        compiler_params=pltpu.CompilerParams(
            dimension_semantics=("arbitrary",),
            vmem_limit_bytes=VMEM_LIMIT_BYTES),
        name="dispatch",
    )(pos, pad, x1p)


def _expert_kernel(blk_ref, x_ref, wg_hbm, wu_hbm, wd_hbm, out_ref, wgb, wub, wdb, wsem):
    b = pl.program_id(0)
    n_used = blk_ref[BLK_USED, 0]

    def weight_copies(e, s):
        return (pltpu.make_async_copy(wg_hbm.at[e], wgb.at[s], wsem.at[s, 0]),
                pltpu.make_async_copy(wu_hbm.at[e], wub.at[s], wsem.at[s, 1]),
                pltpu.make_async_copy(wd_hbm.at[e], wdb.at[s], wsem.at[s, 2]))

    @pl.when(b == 0)
    def _():
        for c in weight_copies(blk_ref[BLK_EXPERT, 0], 0):
            c.start()

    @pl.when(b < n_used)
    def _():
        ws = blk_ref[BLK_SLOT, b]

        @pl.when(blk_ref[BLK_FIRST, b] == 1)
        def _():
            @pl.when(blk_ref[BLK_NEXT, b] >= 0)
            def _():
                for c in weight_copies(blk_ref[BLK_NEXT, b], 1 - ws):
                    c.start()
            for c in weight_copies(blk_ref[BLK_EXPERT, b], ws):
                c.wait()

        xb = _load_packed_rows(x_ref, ROW_BLOCK).astype(jnp.bfloat16)
        g = jnp.dot(xb, wgb[ws], preferred_element_type=jnp.float32)
        u = jnp.dot(xb, wub[ws], preferred_element_type=jnp.float32)
        h = jax.nn.silu(g) * u
        _store_packed_rows(out_ref, jnp.dot(h, wdb[ws], preferred_element_type=jnp.float32))

    @pl.when(b >= n_used)
    def _():
        out_ref[...] = jnp.zeros(out_ref.shape, out_ref.dtype)


def _expert_call(blk, xs, w_gate, w_up, w_down):
    any_spec = pl.BlockSpec(memory_space=pl.ANY)
    block = (ROW_BLOCK * SLAB, 128)
    grid_spec = pltpu.PrefetchScalarGridSpec(
        num_scalar_prefetch=1,
        grid=(N_BLOCKS,),
        in_specs=[pl.BlockSpec(block, lambda b, blk: (jnp.minimum(b, blk[BLK_USED, 0] - 1), 0)),
                  any_spec, any_spec, any_spec],
        out_specs=pl.BlockSpec(block, lambda b, blk: (b, 0)),
        scratch_shapes=[pltpu.VMEM((2, D_MODEL, D_EXPERT), jnp.float32),
                        pltpu.VMEM((2, D_MODEL, D_EXPERT), jnp.float32),
                        pltpu.VMEM((2, D_EXPERT, D_MODEL), jnp.float32),
                        pltpu.SemaphoreType.DMA((2, 3))],
    )
    return pl.pallas_call(
        _expert_kernel,
        grid_spec=grid_spec,
        out_shape=jax.ShapeDtypeStruct((N_BLOCKS * ROW_BLOCK * SLAB, 128), jnp.uint32),
        compiler_params=pltpu.CompilerParams(
            dimension_semantics=("arbitrary",),
            vmem_limit_bytes=VMEM_LIMIT_BYTES),
        name="experts",
    )(blk, xs, w_gate, w_up, w_down)


def _combine_kernel(pos_ref, ys_hbm, x1p_ref, gate_ref, g_ref, b_ref, out_ref, ybuf, sem):
    i = pl.program_id(0)
    n_steps = pl.num_programs(0)
    rows = TOP_K * COMB_BM

    def gather_start(tile, s):
        def issue(r, carry):
            for k in range(TOP_K):
                src = pl.multiple_of(pos_ref[k, tile * COMB_BM + r] * SLAB, SLAB)
                dst = pl.multiple_of((k * COMB_BM + r) * SLAB, SLAB)
                pltpu.make_async_copy(ys_hbm.at[pl.ds(src, SLAB)], ybuf.at[s, pl.ds(dst, SLAB)],
                                      sem.at[s]).start(priority=k % 2)
            return carry
        lax.fori_loop(0, COMB_BM, issue, 0, unroll=4)

    @pl.when(i == 0)
    def _():
        gather_start(0, 0)

    s = lax.rem(i, 2)

    @pl.when(i + 1 < n_steps)
    def _():
        gather_start(i + 1, 1 - s)

    pltpu.make_async_copy(ys_hbm.at[pl.ds(0, rows * SLAB)], ybuf.at[s], sem.at[s]).wait()
    gate = gate_ref[...]
    y0 = _load_packed_rows(ybuf.at[s, pl.ds(0, COMB_BM * SLAB)], COMB_BM)
    y1 = _load_packed_rows(ybuf.at[s, pl.ds(COMB_BM * SLAB, COMB_BM * SLAB)], COMB_BM)
    ffn = y0 * gate[:, 0:1] + y1 * gate[:, 1:2]
    z = ALPHA * _load_packed_rows(x1p_ref, COMB_BM) + ffn
    mu = jnp.mean(z, axis=-1, keepdims=True)
    d = z - mu
    var = jnp.mean(d * d, axis=-1, keepdims=True)
    out_ref[...] = d * lax.rsqrt(var + LN_EPS) * g_ref[...] + b_ref[...]


def _combine_call(pos, ys, x1p, gates, ln_g, ln_b):
    row_full = pl.BlockSpec((COMB_BM, D_MODEL), lambda i, pos: (i, 0))
    vec = pl.BlockSpec((1, D_MODEL), lambda i, pos: (0, 0))
    grid_spec = pltpu.PrefetchScalarGridSpec(
        num_scalar_prefetch=1,
        grid=(SEQ // COMB_BM,),
        in_specs=[
            pl.BlockSpec(memory_space=pl.ANY),
            pl.BlockSpec((COMB_BM * SLAB, 128), lambda i, pos: (i, 0)),
            pl.BlockSpec((COMB_BM, TOP_K), lambda i, pos: (i, 0)),
            vec, vec,
        ],
        out_specs=row_full,
        scratch_shapes=[pltpu.VMEM((2, TOP_K * COMB_BM * SLAB, 128), jnp.uint32),
                        pltpu.SemaphoreType.DMA((2,))],
    )
    return pl.pallas_call(
        _combine_kernel,
        grid_spec=grid_spec,
        out_shape=jax.ShapeDtypeStruct((SEQ, D_MODEL), jnp.float32),
        compiler_params=pltpu.CompilerParams(
            dimension_semantics=("arbitrary",),
            vmem_limit_bytes=VMEM_LIMIT_BYTES),
        name="combine",
    )(pos, ys, x1p, gates, ln_g, ln_b)


def _route_kernel(lt_ref, gates_ref, pos_ref, blk_ref, pad_ref, sel_ref, rank_ref):
    f32, i32 = jnp.float32, jnp.int32
    neg = jnp.float32(-jnp.inf)
    n_chunks = SEQ // ROUTE_TC
    erow = lax.broadcasted_iota(i32, (N_EXPERTS, ROUTE_TC), 0)
    grow = lax.broadcasted_iota(i32, (N_GROUPS, ROUTE_TC), 0)
    src_tok = lax.broadcasted_iota(i32, (ROUTE_TC, ROUTE_TC), 0)
    dst_tok = lax.broadcasted_iota(i32, (ROUTE_TC, ROUTE_TC), 1)
    before = (src_tok < dst_tok).astype(jnp.bfloat16)

    def first_index(mask, idx, size):
        return jnp.min(jnp.where(mask, idx, size), axis=0, keepdims=True)

    counts = jnp.zeros((N_EXPERTS, 1), f32)
    for c in range(n_chunks):
        lanes = slice(c * ROUTE_TC, (c + 1) * ROUTE_TC)
        gl = lt_ref[0:N_GROUPS, lanes]
        g_max = jnp.max(gl, axis=0, keepdims=True)
        g_sel = first_index(gl == g_max, grow, N_GROUPS)
        g_weight = 1.0 / jnp.sum(jnp.exp(gl - g_max), axis=0, keepdims=True)
        el = lt_ref[N_GROUPS:N_GROUPS + N_EXPERTS, lanes]
        in_group = (erow >= g_sel * EXPERTS_PER_GROUP) & (erow < (g_sel + 1) * EXPERTS_PER_GROUP)
        el = jnp.where(in_group, el, neg)
        t1 = jnp.max(el, axis=0, keepdims=True)
        e1 = first_index(el == t1, erow, N_EXPERTS)
        el2 = jnp.where(erow == e1, neg, el)
        t2 = jnp.max(el2, axis=0, keepdims=True)
        e2 = first_index(el2 == t2, erow, N_EXPERTS)
        r = jnp.exp(t2 - t1)
        gates_ref[0:1, lanes] = g_weight * (1.0 / (1.0 + r))
        gates_ref[1:2, lanes] = g_weight * (r / (1.0 + r))
        oh1 = erow == e1
        oh2 = erow == e2
        hits = oh1.astype(jnp.bfloat16) + oh2.astype(jnp.bfloat16)
        ahead = jnp.dot(hits, before, preferred_element_type=f32) + counts
        rank_ref[0:1, lanes] = jnp.sum(jnp.where(oh1, ahead, 0.0), axis=0, keepdims=True)
        rank_ref[1:2, lanes] = jnp.sum(jnp.where(oh2, ahead, 0.0), axis=0, keepdims=True)
        sel_ref[0:1, lanes] = e1
        sel_ref[1:2, lanes] = e2
        counts = counts + jnp.sum(hits.astype(f32), axis=1, keepdims=True)

    n_blk = lax.shift_right_logical(counts.astype(i32) + (ROW_BLOCK - 1), ROW_BLOCK.bit_length() - 1)
    e_src = lax.broadcasted_iota(i32, (N_EXPERTS, N_EXPERTS), 1)
    e_dst = lax.broadcasted_iota(i32, (N_EXPERTS, N_EXPERTS), 0)
    incl = (e_src <= e_dst).astype(jnp.bfloat16)
    n_blk_b = jnp.broadcast_to(n_blk.astype(f32), (N_EXPERTS, 128)).astype(jnp.bfloat16)
    blk_end = jnp.dot(incl, n_blk_b, preferred_element_type=f32)[:, 0:1].astype(i32)
    blk_start = blk_end - n_blk
    row_start = (blk_start * ROW_BLOCK).astype(f32)
    e_sub128 = lax.broadcasted_iota(i32, (N_EXPERTS, 128), 0)
    e_lane128 = lax.broadcasted_iota(i32, (N_EXPERTS, 128), 1)
    diag = e_sub128 == e_lane128
    pad_lo = jnp.sum(jnp.where(diag, blk_start * ROW_BLOCK + counts.astype(i32), 0), axis=0, keepdims=True)
    pad_hi = jnp.sum(jnp.where(diag, blk_end * ROW_BLOCK, 0), axis=0, keepdims=True)
    pad_ref[...] = jnp.concatenate([pad_lo, pad_hi, jnp.zeros((6, 128), i32)], axis=0)

    for c in range(n_chunks):
        lanes = slice(c * ROUTE_TC, (c + 1) * ROUTE_TC)
        for k in range(TOP_K):
            oh = erow == sel_ref[k:k + 1, lanes]
            start = jnp.sum(jnp.where(oh, row_start, 0.0), axis=0, keepdims=True)
            pos_ref[k:k + 1, lanes] = (start + rank_ref[k:k + 1, lanes]).astype(i32)

    b_lane = lax.broadcasted_iota(i32, (N_EXPERTS, BLK_LANES), 1)
    e_sub = lax.broadcasted_iota(i32, (N_EXPERTS, BLK_LANES), 0)
    used = n_blk > 0
    block_expert = jnp.minimum(jnp.sum((blk_end <= b_lane).astype(i32), axis=0, keepdims=True), N_EXPERTS - 1)
    first = jnp.sum((used & (blk_start == b_lane)).astype(i32), axis=0, keepdims=True)
    run_index = jnp.sum((used & (e_sub < block_expert)).astype(i32), axis=0, keepdims=True)
    nxt = jnp.min(jnp.where(used & (e_sub > block_expert), e_sub, N_EXPERTS), axis=0, keepdims=True)
    nxt = jnp.where(nxt >= N_EXPERTS, -1, nxt)
    n_used = jnp.broadcast_to(blk_end[N_EXPERTS - 1:N_EXPERTS, :], (1, BLK_LANES))
    blk_ref[...] = jnp.concatenate(
        [block_expert, first, run_index & 1, nxt, n_used, jnp.zeros((3, BLK_LANES), i32)], axis=0)


def _route_call(logits_t):
    return pl.pallas_call(
        _route_kernel,
        out_shape=[jax.ShapeDtypeStruct((TOP_K, SEQ), jnp.float32),
                   jax.ShapeDtypeStruct((TOP_K, SEQ), jnp.int32),
                   jax.ShapeDtypeStruct((8, BLK_LANES), jnp.int32),
                   jax.ShapeDtypeStruct((8, 128), jnp.int32)],
        scratch_shapes=[pltpu.VMEM((TOP_K, SEQ), jnp.int32),
                        pltpu.VMEM((TOP_K, SEQ), jnp.float32)],
        compiler_params=pltpu.CompilerParams(vmem_limit_bytes=VMEM_LIMIT_BYTES),
        name="route",
    )(logits_t)


def kernel(x, w_in, b_in, conv_w, conv_b, conv_norm_g, conv_norm_b, gmlp_norm_g, gmlp_norm_b,
           w_spatial, b_spatial, w_out, b_out, ln1_g, ln1_b, w_router_group, b_router_group,
           w_router_expert, b_router_expert, w_expert_gate, w_expert_up, w_expert_down,
           ln2_g, ln2_b):
    assert x.shape == (1, SEQ, D_MODEL) and w_in.shape[0] == 1
    x2d = x.reshape(SEQ, D_MODEL)
    ya, yg = _mixer_call(
        x2d, w_in[0].astype(jnp.bfloat16), b_in, conv_w[0], conv_b, conv_norm_g, conv_norm_b,
        gmlp_norm_g, gmlp_norm_b, w_spatial[0], b_spatial[0][:, :, None])

    pad = ROUTER_COLS - N_GROUPS - N_EXPERTS
    w_router = jnp.concatenate(
        [w_router_group[0],
         jnp.transpose(w_router_expert[0], (1, 0, 2)).reshape(D_MODEL, N_EXPERTS),
         jnp.zeros((D_MODEL, pad), jnp.float32)], axis=1)
    b_router = jnp.concatenate(
        [b_router_group[0], b_router_expert[0].reshape(-1), jnp.zeros((pad,), jnp.float32)])[None, :]
    x1p, logits_t = _outproj_call(ya, yg, x2d, w_out[0].astype(jnp.bfloat16), b_out, ln1_g, ln1_b,
                                    w_router, b_router)

    gates_t, pos, blk, pad = _route_call(logits_t)
    xs = _dispatch_call(pos, pad, x1p)
    ys = _expert_call(blk, xs, w_expert_gate[0], w_expert_up[0], w_expert_down[0])
    out = _combine_call(pos, ys, x1p, gates_t.T, ln2_g, ln2_b)
    return out.reshape(1, SEQ, D_MODEL)
```

```python
import jax
import jax.numpy as jnp
from jax import lax
from jax.experimental import pallas as pl
from jax.experimental.pallas import tpu as pltpu

D_MODEL = 4096
SEQ = 8192
D_CONV = D_MODEL // 2
D_GMLP = D_MODEL // 2
CONV_WIDTH = 31
GROUP_DIM = 128
GMLP_HEADS = 16
CHUNK = 128
N_GROUPS = 8
EXPERTS_PER_GROUP = 8
N_EXPERTS = N_GROUPS * EXPERTS_PER_GROUP
TOP_K = 2
D_EXPERT = D_MODEL // 8
ROW_BLOCK = 128
LN_EPS = 1e-5
ALPHA = 2.0 ** 0.25

N_ROWS = SEQ * TOP_K
N_BLOCKS = N_ROWS // ROW_BLOCK + N_EXPERTS
ROUTER_COLS = 128
SLAB = D_MODEL // 2 // 128

VMEM_LIMIT_BYTES = 60 * 1024 * 1024

MIX_BM = 1024
MIX_SUB = 256
MIX_BN = 256
HALO = 32
CONV_ROWS = 128
SHIFT_ROWS = CONV_ROWS + HALO - 8

OUT_BM = 256
ROUTE_TC = 512
BLK_LANES = 256
BLK_EXPERT, BLK_FIRST, BLK_SLOT, BLK_NEXT, BLK_USED = range(5)
COMB_BM = 128
DISP_BM = 128


def _group_norm(v, g, b):
    outs = []
    for s in range(v.shape[1] // GROUP_DIM):
        blk = v[:, s * GROUP_DIM:(s + 1) * GROUP_DIM]
        mu = jnp.mean(blk, axis=-1, keepdims=True)
        d = blk - mu
        var = jnp.mean(d * d, axis=-1, keepdims=True)
        outs.append(d * lax.rsqrt(var + LN_EPS))
    return jnp.concatenate(outs, axis=-1) * g + b


def _mixer_kernel(x_ref, wa_ref, wg_ref, wu_ref, wv_ref, ba_ref, bg_ref, bu_ref, bv_ref,
                  cw_ref, cb_ref, cng_ref, cnb_ref, gng_ref, gnb_ref, wsp_ref, bsp_ref,
                  ya_ref, yg_ref, xs_ref, r_ref, sh_ref):
    i = pl.program_id(1)
    f32 = jnp.float32
    bf16 = jnp.bfloat16
    w_refs = (wa_ref, wg_ref, wu_ref, wv_ref)
    off = HALO - (CONV_WIDTH - 1)

    @pl.when(i == 0)
    def _():
        xs_ref[0:HALO, :] = jnp.zeros((HALO, MIX_BN), f32)

    row = lax.broadcasted_iota(jnp.int32, (CHUNK, CHUNK), 0)
    col = lax.broadcasted_iota(jnp.int32, (CHUNK, CHUNK), 1)
    w_sp = [jnp.where(row >= col, wsp_ref[h], 0.0).astype(bf16) for h in range(MIX_BN // GROUP_DIM)]

    def project(k):
        xk = x_ref[k * MIX_SUB:(k + 1) * MIX_SUB, :].astype(bf16)
        for q in range(4):
            r_ref[k % 2, q] = jnp.dot(xk, w_refs[q][...], preferred_element_type=f32)

    def finish_conv(k, c):
        t0 = k * MIX_SUB + c * CONV_ROWS
        rows = slice(c * CONV_ROWS, (c + 1) * CONV_ROWS)
        a = r_ref[k % 2, 0, rows, :] + ba_ref[...]
        gate = r_ref[k % 2, 1, rows, :] + bg_ref[...]
        xs_ref[HALO + t0:HALO + t0 + CONV_ROWS, :] = a * jax.nn.sigmoid(gate)
        sh = sh_ref.at[c % 2]
        for m in range(1, 8):
            sh[m - 1] = xs_ref[t0 + m:t0 + m + SHIFT_ROWS, :]
        acc = jnp.zeros((CONV_ROWS, MIX_BN), f32) + cb_ref[...]
        for k2 in range(CONV_WIDTH):
            m, j8 = (off + k2) % 8, 8 * ((off + k2) // 8)
            if m == 0:
                src = xs_ref[t0 + j8:t0 + j8 + CONV_ROWS, :]
            else:
                src = sh[m - 1, j8:j8 + CONV_ROWS, :]
            acc = acc + cw_ref[k2:k2 + 1, :] * src
        y = jax.nn.silu(_group_norm(acc, cng_ref[...], cnb_ref[...]))
        ya_ref[t0:t0 + CONV_ROWS, :] = y.astype(ya_ref.dtype)

    def finish_gmlp(k):
        base = k * MIX_SUB
        n_chunks = MIX_SUB // CHUNK
        us, vs = [], []
        for c in range(n_chunks):
            rows = slice(c * CHUNK, (c + 1) * CHUNK)
            us.append(jax.nn.gelu(r_ref[k % 2, 2, rows, :] + bu_ref[...]))
            v = jax.nn.gelu(r_ref[k % 2, 3, rows, :] + bv_ref[...])
            vs.append(_group_norm(v, gng_ref[...], gnb_ref[...]).astype(bf16))
        for h in range(MIX_BN // GROUP_DIM):
            lanes = slice(h * GROUP_DIM, (h + 1) * GROUP_DIM)
            v_h = jnp.concatenate([v[:, lanes] for v in vs], axis=1)
            s_h = jnp.dot(w_sp[h], v_h, preferred_element_type=f32) + bsp_ref[h]
            for c in range(n_chunks):
                g = us[c][:, lanes] * s_h[:, c * GROUP_DIM:(c + 1) * GROUP_DIM]
                yg_ref[base + c * CHUNK:base + (c + 1) * CHUNK, lanes] = g.astype(yg_ref.dtype)

    def finish(k):
        for c in range(MIX_SUB // CONV_ROWS):
            finish_conv(k, c)
        finish_gmlp(k)

    n_sub = MIX_BM // MIX_SUB
    for k in range(n_sub + 1):
        if k < n_sub:
            project(k)
        if k > 0:
            finish(k - 1)
    xs_ref[0:HALO, :] = xs_ref[MIX_BM:MIX_BM + HALO, :]


def _mixer_call(x2d, w_in_b, b_in, conv_w, conv_b, cng, cnb, gng, gnb, w_spatial, b_spatial_col):
    nq = D_CONV // MIX_BN
    grid = (nq, SEQ // MIX_BM)

    def wspec(q):
        return pl.BlockSpec((D_MODEL, MIX_BN), lambda j, i, q=q: (0, q * nq + j))

    def bspec(q):
        return pl.BlockSpec((1, MIX_BN), lambda j, i, q=q: (0, q * nq + j))

    vec = pl.BlockSpec((1, MIX_BN), lambda j, i: (0, j))
    heads_per_step = MIX_BN // GROUP_DIM
    in_specs = [
        pl.BlockSpec((MIX_BM, D_MODEL), lambda j, i: (i, 0)),
        wspec(0), wspec(1), wspec(2), wspec(3),
        bspec(0), bspec(1), bspec(2), bspec(3),
        pl.BlockSpec((CONV_WIDTH, MIX_BN), lambda j, i: (0, j)),
        vec, vec, vec, vec, vec,
        pl.BlockSpec((heads_per_step, CHUNK, CHUNK), lambda j, i: (j, 0, 0)),
        pl.BlockSpec((heads_per_step, CHUNK, 1), lambda j, i: (j, 0, 0)),
    ]
    out_spec = pl.BlockSpec((MIX_BM, MIX_BN), lambda j, i: (i, j))
    return pl.pallas_call(
        _mixer_kernel,
        grid=grid,
        in_specs=in_specs,
        out_specs=[out_spec, out_spec],
        out_shape=[jax.ShapeDtypeStruct((SEQ, D_CONV), jnp.bfloat16),
                   jax.ShapeDtypeStruct((SEQ, D_GMLP), jnp.bfloat16)],
        scratch_shapes=[pltpu.VMEM((HALO + MIX_BM, MIX_BN), jnp.float32),
                        pltpu.VMEM((2, 4, MIX_SUB, MIX_BN), jnp.float32),
                        pltpu.VMEM((2, 7, SHIFT_ROWS, MIX_BN), jnp.float32)],
        compiler_params=pltpu.CompilerParams(
            dimension_semantics=("arbitrary", "arbitrary"),
            vmem_limit_bytes=VMEM_LIMIT_BYTES),
        name="mixer",
    )(x2d, w_in_b, w_in_b, w_in_b, w_in_b, b_in, b_in, b_in, b_in,
      conv_w, conv_b, cng, cnb, gng, gnb, w_spatial, b_spatial_col)


def _store_packed_rows(dst_ref, x):
    f32, u32 = jnp.float32, jnp.uint32
    half = D_MODEL // 2
    hi = lax.bitcast_convert_type(x[:, :half].astype(jnp.bfloat16).astype(f32), u32)
    lo = lax.bitcast_convert_type(x[:, half:].astype(jnp.bfloat16).astype(f32), u32)
    words = hi | (lo >> 16)
    for j in range(SLAB):
        dst_ref[pl.ds(j, x.shape[0], stride=SLAB), :] = words[:, j * 128:(j + 1) * 128]


def _load_packed_rows(src_ref, rows):
    f32, u32 = jnp.float32, jnp.uint32
    his, los = [], []
    for j in range(SLAB):
        w = src_ref[pl.ds(j, rows, stride=SLAB), :]
        his.append(lax.bitcast_convert_type(w & u32(0xFFFF0000), f32))
        los.append(lax.bitcast_convert_type(w << 16, f32))
    return jnp.concatenate(his + los, axis=1)


def _outproj_kernel(ya_ref, yg_ref, x_ref, w_ref, bo_ref, g_ref, b_ref, wr_ref, br_ref,
                    x1p_ref, logit_ref):
    f32 = jnp.float32
    z = jnp.dot(ya_ref[...], w_ref[0:D_CONV, :], preferred_element_type=f32)
    z = z + jnp.dot(yg_ref[...], w_ref[D_CONV:, :], preferred_element_type=f32)
    z = z + bo_ref[...] + ALPHA * x_ref[...]
    mu = jnp.mean(z, axis=-1, keepdims=True)
    d = z - mu
    var = jnp.mean(d * d, axis=-1, keepdims=True)
    x1 = d * lax.rsqrt(var + LN_EPS) * g_ref[...] + b_ref[...]
    _store_packed_rows(x1p_ref, x1)
    wr = wr_ref[...]
    wr_hi = wr.astype(jnp.bfloat16)
    wr_lo = (wr - wr_hi.astype(f32)).astype(jnp.bfloat16)
    x_hi = x1.astype(jnp.bfloat16)
    x_lo = (x1 - x_hi.astype(f32)).astype(jnp.bfloat16)
    logits = jnp.dot(x_hi, wr_hi, preferred_element_type=f32)
    logits = logits + (jnp.dot(x_lo, wr_hi, preferred_element_type=f32)
                       + jnp.dot(x_hi, wr_lo, preferred_element_type=f32))
    logit_ref[...] = (logits + br_ref[...]).T


def _outproj_call(ya, yg, x2d, w_out_b, b_out, ln_g, ln_b, w_router, b_router):
    grid = (SEQ // OUT_BM,)
    row_half = pl.BlockSpec((OUT_BM, D_CONV), lambda i: (i, 0))
    row_full = pl.BlockSpec((OUT_BM, D_MODEL), lambda i: (i, 0))
    vec = pl.BlockSpec((1, D_MODEL), lambda i: (0, 0))
    in_specs = [
        row_half, row_half, row_full,
        pl.BlockSpec((D_MODEL, D_MODEL), lambda i: (0, 0), pipeline_mode=pl.Buffered(1)),
        vec, vec, vec,
        pl.BlockSpec((D_MODEL, ROUTER_COLS), lambda i: (0, 0), pipeline_mode=pl.Buffered(1)),
        pl.BlockSpec((1, ROUTER_COLS), lambda i: (0, 0)),
    ]
    return pl.pallas_call(
        _outproj_kernel,
        grid=grid,
        in_specs=in_specs,
        out_specs=[pl.BlockSpec((OUT_BM * SLAB, 128), lambda i: (i, 0)),
                   pl.BlockSpec((ROUTER_COLS, OUT_BM), lambda i: (0, i))],
        out_shape=[jax.ShapeDtypeStruct((SEQ * SLAB, 128), jnp.uint32),
                   jax.ShapeDtypeStruct((ROUTER_COLS, SEQ), jnp.float32)],
        compiler_params=pltpu.CompilerParams(
            dimension_semantics=("arbitrary",),
            vmem_limit_bytes=VMEM_LIMIT_BYTES),
        name="outproj",
    )(ya, yg, x2d, w_out_b, b_out, ln_g, ln_b, w_router, b_router)


def _dispatch_kernel(pos_ref, pad_ref, x1p_hbm, xs_hbm, buf, zbuf, lsem, ssem, zsem, tsem):
    i = pl.program_id(0)
    n_steps = pl.num_programs(0)
    rows = DISP_BM * SLAB
    n_slots = 3

    def load(tile, slot):
        src = pl.multiple_of(tile * rows, rows)
        return pltpu.make_async_copy(x1p_hbm.at[pl.ds(src, rows)], buf.at[slot], lsem.at[slot])

    def scatter_wait(slot):
        for _ in range(TOP_K):
            pltpu.make_async_copy(buf.at[slot], xs_hbm.at[pl.ds(0, rows)], ssem.at[slot]).wait()

    def pad_copy(r, n_rows):
        dst = pl.multiple_of(r * SLAB, SLAB)
        return pltpu.make_async_copy(zbuf.at[pl.ds(0, n_rows * SLAB)],
                                     xs_hbm.at[pl.ds(dst, n_rows * SLAB)], zsem)

    def tail_copy(blk):
        dst = pl.multiple_of(blk * (ROW_BLOCK * SLAB), ROW_BLOCK * SLAB)
        return pltpu.make_async_copy(zbuf, xs_hbm.at[pl.ds(dst, ROW_BLOCK * SLAB)], tsem)

    def for_each_tail_block(fn):
        used_blocks = lax.shift_right_logical(pad_ref[1, N_EXPERTS - 1], ROW_BLOCK.bit_length() - 1)

        def per_block(blk, carry):
            fn(blk)
            return carry
        lax.fori_loop(used_blocks, N_BLOCKS, per_block, 0)

    def for_each_pad_piece(fn):
        def per_expert(e, carry):
            lo = pad_ref[0, e]
            n_pad = pad_ref[1, e] - lo
            for bit in range(ROW_BLOCK.bit_length() - 1):
                size = 1 << bit

                @pl.when((n_pad & size) != 0)
                def _():
                    fn(lo + (n_pad & (size - 1)), size)
            return carry
        lax.fori_loop(0, N_EXPERTS, per_expert, 0)

    @pl.when(i == 0)
    def _():
        load(0, 0).start()
        zbuf[...] = jnp.zeros(zbuf.shape, zbuf.dtype)
        for_each_pad_piece(lambda r, n_rows: pad_copy(r, n_rows).start())
        for_each_tail_block(lambda blk: tail_copy(blk).start())

    slot = lax.rem(i, n_slots)

    @pl.when(i + 1 < n_steps)
    def _():
        load(i + 1, lax.rem(i + 1, n_slots)).start()

    load(i, slot).wait()

    def issue(r, carry):
        src = pl.multiple_of(r * SLAB, SLAB)
        for k in range(TOP_K):
            dst = pl.multiple_of(pos_ref[k, i * DISP_BM + r] * SLAB, SLAB)
            pltpu.make_async_copy(buf.at[slot, pl.ds(src, SLAB)], xs_hbm.at[pl.ds(dst, SLAB)],
                                  ssem.at[slot]).start(priority=k % 2)
        return carry
    lax.fori_loop(0, DISP_BM, issue, 0, unroll=4)

    @pl.when(i >= 1)
    def _():
        scatter_wait(lax.rem(i + n_slots - 1, n_slots))

    @pl.when(i == n_steps - 1)
    def _():
        scatter_wait(slot)
        for_each_pad_piece(lambda r, n_rows: pad_copy(r, n_rows).wait())
        for_each_tail_block(lambda blk: tail_copy(blk).wait())


def _dispatch_call(pos, pad, x1p):
    any_spec = pl.BlockSpec(memory_space=pl.ANY)
    grid_spec = pltpu.PrefetchScalarGridSpec(
        num_scalar_prefetch=2,
        grid=(SEQ // DISP_BM,),
        in_specs=[any_spec],
        out_specs=any_spec,
        scratch_shapes=[pltpu.VMEM((3, DISP_BM * SLAB, 128), jnp.uint32),
                        pltpu.VMEM((ROW_BLOCK * SLAB, 128), jnp.uint32),
                        pltpu.SemaphoreType.DMA((3,)),
                        pltpu.SemaphoreType.DMA((3,)),
                        pltpu.SemaphoreType.DMA(()),
                        pltpu.SemaphoreType.DMA(())],
    )
    return pl.pallas_call(
        _dispatch_kernel,
        grid_spec=grid_spec,
        out_shape=jax.ShapeDtypeStruct((N_BLOCKS * ROW_BLOCK * SLAB, 128), jnp.uint32),
        compiler_params=pltpu.CompilerParams(
            dimension_semantics=("arbitrary",),
            vmem_limit_bytes=VMEM_LIMIT_BYTES),
        name="dispatch",
    )(pos, pad, x1p)


def _expert_kernel(blk_ref, x_ref, wg_hbm, wu_hbm, wd_hbm, out_ref, wgb, wub, wdb, wsem):
    b = pl.program_id(0)
    n_used = blk_ref[BLK_USED, 0]

    def weight_copies(e, s):
        return (pltpu.make_async_copy(wg_hbm.at[e], wgb.at[s], wsem.at[s, 0]),
                pltpu.make_async_copy(wu_hbm.at[e], wub.at[s], wsem.at[s, 1]),
                pltpu.make_async_copy(wd_hbm.at[e], wdb.at[s], wsem.at[s, 2]))

    @pl.when(b == 0)
    def _():
        for c in weight_copies(blk_ref[BLK_EXPERT, 0], 0):
            c.start(priority=1)

    @pl.when(b < n_used)
    def _():
        ws = blk_ref[BLK_SLOT, b]

        @pl.when(blk_ref[BLK_FIRST, b] == 1)
        def _():
            @pl.when(blk_ref[BLK_NEXT, b] >= 0)
            def _():
                for c in weight_copies(blk_ref[BLK_NEXT, b], 1 - ws):
                    c.start(priority=1)
            for c in weight_copies(blk_ref[BLK_EXPERT, b], ws):
                c.wait()

        xb = _load_packed_rows(x_ref, ROW_BLOCK).astype(jnp.bfloat16)
        g = jnp.dot(xb, wgb[ws], preferred_element_type=jnp.float32)
        u = jnp.dot(xb, wub[ws], preferred_element_type=jnp.float32)
        h = jax.nn.silu(g) * u
        _store_packed_rows(out_ref, jnp.dot(h, wdb[ws], preferred_element_type=jnp.float32))

    @pl.when(b >= n_used)
    def _():
        out_ref[...] = jnp.zeros(out_ref.shape, out_ref.dtype)


def _expert_call(blk, xs, w_gate, w_up, w_down):
    any_spec = pl.BlockSpec(memory_space=pl.ANY)
    block = (ROW_BLOCK * SLAB, 128)
    grid_spec = pltpu.PrefetchScalarGridSpec(
        num_scalar_prefetch=1,
        grid=(N_BLOCKS,),
        in_specs=[pl.BlockSpec(block, lambda b, blk: (jnp.minimum(b, blk[BLK_USED, 0] - 1), 0)),
                  any_spec, any_spec, any_spec],
        out_specs=pl.BlockSpec(block, lambda b, blk: (b, 0)),
        scratch_shapes=[pltpu.VMEM((2, D_MODEL, D_EXPERT), jnp.float32),
                        pltpu.VMEM((2, D_MODEL, D_EXPERT), jnp.float32),
                        pltpu.VMEM((2, D_EXPERT, D_MODEL), jnp.float32),
                        pltpu.SemaphoreType.DMA((2, 3))],
    )
    return pl.pallas_call(
        _expert_kernel,
        grid_spec=grid_spec,
        out_shape=jax.ShapeDtypeStruct((N_BLOCKS * ROW_BLOCK * SLAB, 128), jnp.uint32),
        compiler_params=pltpu.CompilerParams(
            dimension_semantics=("arbitrary",),
            vmem_limit_bytes=VMEM_LIMIT_BYTES),
        name="experts",
    )(blk, xs, w_gate, w_up, w_down)


def _combine_kernel(pos_ref, ys_hbm, x1p_ref, gate_ref, g_ref, b_ref, out_ref, ybuf, sem):
    i = pl.program_id(0)
    n_steps = pl.num_programs(0)
    rows = TOP_K * COMB_BM

    def gather_start(tile, s):
        def issue(r, carry):
            for k in range(TOP_K):
                src = pl.multiple_of(pos_ref[k, tile * COMB_BM + r] * SLAB, SLAB)
                dst = pl.multiple_of((k * COMB_BM + r) * SLAB, SLAB)
                pltpu.make_async_copy(ys_hbm.at[pl.ds(src, SLAB)], ybuf.at[s, pl.ds(dst, SLAB)],
                                      sem.at[s]).start(priority=k % 2)
            return carry
        lax.fori_loop(0, COMB_BM, issue, 0, unroll=4)

    @pl.when(i == 0)
    def _():
        gather_start(0, 0)

    s = lax.rem(i, 2)

    @pl.when(i + 1 < n_steps)
    def _():
        gather_start(i + 1, 1 - s)

    pltpu.make_async_copy(ys_hbm.at[pl.ds(0, rows * SLAB)], ybuf.at[s], sem.at[s]).wait()
    gate = gate_ref[...]
    y0 = _load_packed_rows(ybuf.at[s, pl.ds(0, COMB_BM * SLAB)], COMB_BM)
    y1 = _load_packed_rows(ybuf.at[s, pl.ds(COMB_BM * SLAB, COMB_BM * SLAB)], COMB_BM)
    ffn = y0 * gate[:, 0:1] + y1 * gate[:, 1:2]
    z = ALPHA * _load_packed_rows(x1p_ref, COMB_BM) + ffn
    mu = jnp.mean(z, axis=-1, keepdims=True)
    d = z - mu
    var = jnp.mean(d * d, axis=-1, keepdims=True)
    out_ref[...] = d * lax.rsqrt(var + LN_EPS) * g_ref[...] + b_ref[...]


def _combine_call(pos, ys, x1p, gates, ln_g, ln_b):
    row_full = pl.BlockSpec((COMB_BM, D_MODEL), lambda i, pos: (i, 0))
    vec = pl.BlockSpec((1, D_MODEL), lambda i, pos: (0, 0))
    grid_spec = pltpu.PrefetchScalarGridSpec(
        num_scalar_prefetch=1,
        grid=(SEQ // COMB_BM,),
        in_specs=[
            pl.BlockSpec(memory_space=pl.ANY),
            pl.BlockSpec((COMB_BM * SLAB, 128), lambda i, pos: (i, 0)),
            pl.BlockSpec((COMB_BM, TOP_K), lambda i, pos: (i, 0)),
            vec, vec,
        ],
        out_specs=row_full,
        scratch_shapes=[pltpu.VMEM((2, TOP_K * COMB_BM * SLAB, 128), jnp.uint32),
                        pltpu.SemaphoreType.DMA((2,))],
    )
    return pl.pallas_call(
        _combine_kernel,
        grid_spec=grid_spec,
        out_shape=jax.ShapeDtypeStruct((SEQ, D_MODEL), jnp.float32),
        compiler_params=pltpu.CompilerParams(
            dimension_semantics=("arbitrary",),
            vmem_limit_bytes=VMEM_LIMIT_BYTES),
        name="combine",
    )(pos, ys, x1p, gates, ln_g, ln_b)


def _route_kernel(lt_ref, gates_ref, pos_ref, blk_ref, pad_ref, sel_ref, rank_ref):
    f32, i32 = jnp.float32, jnp.int32
    neg = jnp.float32(-jnp.inf)
    n_chunks = SEQ // ROUTE_TC
    erow = lax.broadcasted_iota(i32, (N_EXPERTS, ROUTE_TC), 0)
    grow = lax.broadcasted_iota(i32, (N_GROUPS, ROUTE_TC), 0)
    src_tok = lax.broadcasted_iota(i32, (ROUTE_TC, ROUTE_TC), 0)
    dst_tok = lax.broadcasted_iota(i32, (ROUTE_TC, ROUTE_TC), 1)
    before = (src_tok < dst_tok).astype(jnp.bfloat16)

    def first_index(mask, idx, size):
        return jnp.min(jnp.where(mask, idx, size), axis=0, keepdims=True)

    counts = jnp.zeros((N_EXPERTS, 1), f32)
    for c in range(n_chunks):
        lanes = slice(c * ROUTE_TC, (c + 1) * ROUTE_TC)
        gl = lt_ref[0:N_GROUPS, lanes]
        g_max = jnp.max(gl, axis=0, keepdims=True)
        g_sel = first_index(gl == g_max, grow, N_GROUPS)
        g_weight = 1.0 / jnp.sum(jnp.exp(gl - g_max), axis=0, keepdims=True)
        el = lt_ref[N_GROUPS:N_GROUPS + N_EXPERTS, lanes]
        in_group = (erow >= g_sel * EXPERTS_PER_GROUP) & (erow < (g_sel + 1) * EXPERTS_PER_GROUP)
        el = jnp.where(in_group, el, neg)
        t1 = jnp.max(el, axis=0, keepdims=True)
        e1 = first_index(el == t1, erow, N_EXPERTS)
        el2 = jnp.where(erow == e1, neg, el)
        t2 = jnp.max(el2, axis=0, keepdims=True)
        e2 = first_index(el2 == t2, erow, N_EXPERTS)
        r = jnp.exp(t2 - t1)
        gates_ref[0:1, lanes] = g_weight * (1.0 / (1.0 + r))
        gates_ref[1:2, lanes] = g_weight * (r / (1.0 + r))
        oh1 = erow == e1
        oh2 = erow == e2
        hits = oh1.astype(jnp.bfloat16) + oh2.astype(jnp.bfloat16)
        ahead = jnp.dot(hits, before, preferred_element_type=f32) + counts
        rank_ref[0:1, lanes] = jnp.sum(jnp.where(oh1, ahead, 0.0), axis=0, keepdims=True)
        rank_ref[1:2, lanes] = jnp.sum(jnp.where(oh2, ahead, 0.0), axis=0, keepdims=True)
        sel_ref[0:1, lanes] = e1
        sel_ref[1:2, lanes] = e2
        counts = counts + jnp.sum(hits.astype(f32), axis=1, keepdims=True)

    n_blk = lax.shift_right_logical(counts.astype(i32) + (ROW_BLOCK - 1), ROW_BLOCK.bit_length() - 1)
    e_src = lax.broadcasted_iota(i32, (N_EXPERTS, N_EXPERTS), 1)
    e_dst = lax.broadcasted_iota(i32, (N_EXPERTS, N_EXPERTS), 0)
    incl = (e_src <= e_dst).astype(jnp.bfloat16)
    n_blk_b = jnp.broadcast_to(n_blk.astype(f32), (N_EXPERTS, 128)).astype(jnp.bfloat16)
    blk_end = jnp.dot(incl, n_blk_b, preferred_element_type=f32)[:, 0:1].astype(i32)
    blk_start = blk_end - n_blk
    row_start = (blk_start * ROW_BLOCK).astype(f32)
    e_sub128 = lax.broadcasted_iota(i32, (N_EXPERTS, 128), 0)
    e_lane128 = lax.broadcasted_iota(i32, (N_EXPERTS, 128), 1)
    diag = e_sub128 == e_lane128
    pad_lo = jnp.sum(jnp.where(diag, blk_start * ROW_BLOCK + counts.astype(i32), 0), axis=0, keepdims=True)
    pad_hi = jnp.sum(jnp.where(diag, blk_end * ROW_BLOCK, 0), axis=0, keepdims=True)
    pad_ref[...] = jnp.concatenate([pad_lo, pad_hi, jnp.zeros((6, 128), i32)], axis=0)

    for c in range(n_chunks):
        lanes = slice(c * ROUTE_TC, (c + 1) * ROUTE_TC)
        for k in range(TOP_K):
            oh = erow == sel_ref[k:k + 1, lanes]
            start = jnp.sum(jnp.where(oh, row_start, 0.0), axis=0, keepdims=True)
            pos_ref[k:k + 1, lanes] = (start + rank_ref[k:k + 1, lanes]).astype(i32)

    b_lane = lax.broadcasted_iota(i32, (N_EXPERTS, BLK_LANES), 1)
    e_sub = lax.broadcasted_iota(i32, (N_EXPERTS, BLK_LANES), 0)
    used = n_blk > 0
    block_expert = jnp.minimum(jnp.sum((blk_end <= b_lane).astype(i32), axis=0, keepdims=True), N_EXPERTS - 1)
    first = jnp.sum((used & (blk_start == b_lane)).astype(i32), axis=0, keepdims=True)
    run_index = jnp.sum((used & (e_sub < block_expert)).astype(i32), axis=0, keepdims=True)
    nxt = jnp.min(jnp.where(used & (e_sub > block_expert), e_sub, N_EXPERTS), axis=0, keepdims=True)
    nxt = jnp.where(nxt >= N_EXPERTS, -1, nxt)
    n_used = jnp.broadcast_to(blk_end[N_EXPERTS - 1:N_EXPERTS, :], (1, BLK_LANES))
    blk_ref[...] = jnp.concatenate(
        [block_expert, first, run_index & 1, nxt, n_used, jnp.zeros((3, BLK_LANES), i32)], axis=0)


def _route_call(logits_t):
    return pl.pallas_call(
        _route_kernel,
        out_shape=[jax.ShapeDtypeStruct((TOP_K, SEQ), jnp.float32),
                   jax.ShapeDtypeStruct((TOP_K, SEQ), jnp.int32),
                   jax.ShapeDtypeStruct((8, BLK_LANES), jnp.int32),
                   jax.ShapeDtypeStruct((8, 128), jnp.int32)],
        scratch_shapes=[pltpu.VMEM((TOP_K, SEQ), jnp.int32),
                        pltpu.VMEM((TOP_K, SEQ), jnp.float32)],
        compiler_params=pltpu.CompilerParams(vmem_limit_bytes=VMEM_LIMIT_BYTES),
        name="route",
    )(logits_t)


def kernel(x, w_in, b_in, conv_w, conv_b, conv_norm_g, conv_norm_b, gmlp_norm_g, gmlp_norm_b,
           w_spatial, b_spatial, w_out, b_out, ln1_g, ln1_b, w_router_group, b_router_group,
           w_router_expert, b_router_expert, w_expert_gate, w_expert_up, w_expert_down,
           ln2_g, ln2_b):
    assert x.shape == (1, SEQ, D_MODEL) and w_in.shape[0] == 1
    x2d = x.reshape(SEQ, D_MODEL)
    ya, yg = _mixer_call(
        x2d, w_in[0].astype(jnp.bfloat16), b_in, conv_w[0], conv_b, conv_norm_g, conv_norm_b,
        gmlp_norm_g, gmlp_norm_b, w_spatial[0], b_spatial[0][:, :, None])

    pad = ROUTER_COLS - N_GROUPS - N_EXPERTS
    w_router = jnp.concatenate(
        [w_router_group[0],
         jnp.transpose(w_router_expert[0], (1, 0, 2)).reshape(D_MODEL, N_EXPERTS),
         jnp.zeros((D_MODEL, pad), jnp.float32)], axis=1)
    b_router = jnp.concatenate(
        [b_router_group[0], b_router_expert[0].reshape(-1), jnp.zeros((pad,), jnp.float32)])[None, :]
    x1p, logits_t = _outproj_call(ya, yg, x2d, w_out[0].astype(jnp.bfloat16), b_out, ln1_g, ln1_b,
                                    w_router, b_router)

    gates_t, pos, blk, pad = _route_call(logits_t)
    xs = _dispatch_call(pos, pad, x1p)
    ys = _expert_call(blk, xs, w_expert_gate[0], w_expert_up[0], w_expert_down[0])
    out = _combine_call(pos, ys, x1p, gates_t.T, ln2_g, ln2_b)
    return out.reshape(1, SEQ, D_MODEL)
```

```python
import jax
import jax.numpy as jnp
from jax import lax
from jax.experimental import pallas as pl
from jax.experimental.pallas import tpu as pltpu

D_MODEL = 4096
SEQ = 8192
D_CONV = D_MODEL // 2
D_GMLP = D_MODEL // 2
CONV_WIDTH = 31
GROUP_DIM = 128
GMLP_HEADS = 16
CHUNK = 128
N_GROUPS = 8
EXPERTS_PER_GROUP = 8
N_EXPERTS = N_GROUPS * EXPERTS_PER_GROUP
TOP_K = 2
D_EXPERT = D_MODEL // 8
ROW_BLOCK = 128
LN_EPS = 1e-5
ALPHA = 2.0 ** 0.25

N_ROWS = SEQ * TOP_K
N_BLOCKS = N_ROWS // ROW_BLOCK + N_EXPERTS
ROUTER_COLS = 128
SLAB = D_MODEL // 2 // 128

VMEM_LIMIT_BYTES = 60 * 1024 * 1024

MIX_BM = 1024
MIX_SUB = 256
MIX_BN = 256
HALO = 32
CONV_ROWS = 128
SHIFT_ROWS = CONV_ROWS + HALO - 8

OUT_BM = 256
OUT_SUB = 128
ROUTE_TC = 512
BLK_LANES = 256
BLK_EXPERT, BLK_FIRST, BLK_SLOT, BLK_NEXT, BLK_USED = range(5)
COMB_BM = 256
DISP_BM = 256


def _group_norm(v, g, b):
    outs = []
    for s in range(v.shape[1] // GROUP_DIM):
        blk = v[:, s * GROUP_DIM:(s + 1) * GROUP_DIM]
        mu = jnp.mean(blk, axis=-1, keepdims=True)
        d = blk - mu
        var = jnp.mean(d * d, axis=-1, keepdims=True)
        outs.append(d * lax.rsqrt(var + LN_EPS))
    return jnp.concatenate(outs, axis=-1) * g + b


def _mixer_kernel(x_ref, wa_ref, wg_ref, wu_ref, wv_ref, ba_ref, bg_ref, bu_ref, bv_ref,
                  cw_ref, cb_ref, cng_ref, cnb_ref, gng_ref, gnb_ref, wsp_ref, bsp_ref,
                  ya_ref, yg_ref, xs_ref, r_ref, sh_ref):
    i = pl.program_id(1)
    f32 = jnp.float32
    bf16 = jnp.bfloat16
    w_refs = (wa_ref, wg_ref, wu_ref, wv_ref)
    off = HALO - (CONV_WIDTH - 1)

    @pl.when(i == 0)
    def _():
        xs_ref[0:HALO, :] = jnp.zeros((HALO, MIX_BN), f32)

    row = lax.broadcasted_iota(jnp.int32, (CHUNK, CHUNK), 0)
    col = lax.broadcasted_iota(jnp.int32, (CHUNK, CHUNK), 1)
    w_sp = [jnp.where(row >= col, wsp_ref[h], 0.0).astype(bf16) for h in range(MIX_BN // GROUP_DIM)]

    def project(k):
        xk = x_ref[k * MIX_SUB:(k + 1) * MIX_SUB, :].astype(bf16)
        for q in range(4):
            r_ref[k % 2, q] = jnp.dot(xk, w_refs[q][...], preferred_element_type=f32)

    def finish_conv(k, c):
        t0 = k * MIX_SUB + c * CONV_ROWS
        rows = slice(c * CONV_ROWS, (c + 1) * CONV_ROWS)
        a = r_ref[k % 2, 0, rows, :] + ba_ref[...]
        gate = r_ref[k % 2, 1, rows, :] + bg_ref[...]
        xs_ref[HALO + t0:HALO + t0 + CONV_ROWS, :] = a * jax.nn.sigmoid(gate)
        sh = sh_ref.at[c % 2]
        for m in range(1, 8):
            sh[m - 1] = xs_ref[t0 + m:t0 + m + SHIFT_ROWS, :]
        acc = jnp.zeros((CONV_ROWS, MIX_BN), f32) + cb_ref[...]
        for k2 in range(CONV_WIDTH):
            m, j8 = (off + k2) % 8, 8 * ((off + k2) // 8)
            if m == 0:
                src = xs_ref[t0 + j8:t0 + j8 + CONV_ROWS, :]
            else:
                src = sh[m - 1, j8:j8 + CONV_ROWS, :]
            acc = acc + cw_ref[k2:k2 + 1, :] * src
        y = jax.nn.silu(_group_norm(acc, cng_ref[...], cnb_ref[...]))
        ya_ref[t0:t0 + CONV_ROWS, :] = y.astype(ya_ref.dtype)

    def finish_gmlp(k):
        base = k * MIX_SUB
        n_chunks = MIX_SUB // CHUNK
        us, vs = [], []
        for c in range(n_chunks):
            rows = slice(c * CHUNK, (c + 1) * CHUNK)
            us.append(jax.nn.gelu(r_ref[k % 2, 2, rows, :] + bu_ref[...]))
            v = jax.nn.gelu(r_ref[k % 2, 3, rows, :] + bv_ref[...])
            vs.append(_group_norm(v, gng_ref[...], gnb_ref[...]).astype(bf16))
        for h in range(MIX_BN // GROUP_DIM):
            lanes = slice(h * GROUP_DIM, (h + 1) * GROUP_DIM)
            v_h = jnp.concatenate([v[:, lanes] for v in vs], axis=1)
            s_h = jnp.dot(w_sp[h], v_h, preferred_element_type=f32) + bsp_ref[h]
            for c in range(n_chunks):
                g = us[c][:, lanes] * s_h[:, c * GROUP_DIM:(c + 1) * GROUP_DIM]
                yg_ref[base + c * CHUNK:base + (c + 1) * CHUNK, lanes] = g.astype(yg_ref.dtype)

    def finish(k):
        for c in range(MIX_SUB // CONV_ROWS):
            finish_conv(k, c)
        finish_gmlp(k)

    n_sub = MIX_BM // MIX_SUB
    for k in range(n_sub + 1):
        if k < n_sub:
            project(k)
        if k > 0:
            finish(k - 1)
    xs_ref[0:HALO, :] = xs_ref[MIX_BM:MIX_BM + HALO, :]


def _mixer_call(x2d, w_in_b, b_in, conv_w, conv_b, cng, cnb, gng, gnb, w_spatial, b_spatial_col):
    nq = D_CONV // MIX_BN
    grid = (nq, SEQ // MIX_BM)

    def wspec(q):
        return pl.BlockSpec((D_MODEL, MIX_BN), lambda j, i, q=q: (0, q * nq + j))

    def bspec(q):
        return pl.BlockSpec((1, MIX_BN), lambda j, i, q=q: (0, q * nq + j))

    vec = pl.BlockSpec((1, MIX_BN), lambda j, i: (0, j))
    heads_per_step = MIX_BN // GROUP_DIM
    in_specs = [
        pl.BlockSpec((MIX_BM, D_MODEL), lambda j, i: (i, 0)),
        wspec(0), wspec(1), wspec(2), wspec(3),
        bspec(0), bspec(1), bspec(2), bspec(3),
        pl.BlockSpec((CONV_WIDTH, MIX_BN), lambda j, i: (0, j)),
        vec, vec, vec, vec, vec,
        pl.BlockSpec((heads_per_step, CHUNK, CHUNK), lambda j, i: (j, 0, 0)),
        pl.BlockSpec((heads_per_step, CHUNK, 1), lambda j, i: (j, 0, 0)),
    ]
    out_spec = pl.BlockSpec((MIX_BM, MIX_BN), lambda j, i: (i, j))
    return pl.pallas_call(
        _mixer_kernel,
        grid=grid,
        in_specs=in_specs,
        out_specs=[out_spec, out_spec],
        out_shape=[jax.ShapeDtypeStruct((SEQ, D_CONV), jnp.bfloat16),
                   jax.ShapeDtypeStruct((SEQ, D_GMLP), jnp.bfloat16)],
        scratch_shapes=[pltpu.VMEM((HALO + MIX_BM, MIX_BN), jnp.float32),
                        pltpu.VMEM((2, 4, MIX_SUB, MIX_BN), jnp.float32),
                        pltpu.VMEM((2, 7, SHIFT_ROWS, MIX_BN), jnp.float32)],
        compiler_params=pltpu.CompilerParams(
            dimension_semantics=("arbitrary", "arbitrary"),
            vmem_limit_bytes=VMEM_LIMIT_BYTES),
        name="mixer",
    )(x2d, w_in_b, w_in_b, w_in_b, w_in_b, b_in, b_in, b_in, b_in,
      conv_w, conv_b, cng, cnb, gng, gnb, w_spatial, b_spatial_col)


def _store_packed_rows(dst_ref, x):
    f32, u32 = jnp.float32, jnp.uint32
    half = D_MODEL // 2
    hi = lax.bitcast_convert_type(x[:, :half].astype(jnp.bfloat16).astype(f32), u32)
    lo = lax.bitcast_convert_type(x[:, half:].astype(jnp.bfloat16).astype(f32), u32)
    words = hi | (lo >> 16)
    for j in range(SLAB):
        dst_ref[pl.ds(j, x.shape[0], stride=SLAB), :] = words[:, j * 128:(j + 1) * 128]


def _load_packed_rows(src_ref, rows):
    f32, u32 = jnp.float32, jnp.uint32
    his, los = [], []
    for j in range(SLAB):
        w = src_ref[pl.ds(j, rows, stride=SLAB), :]
        his.append(lax.bitcast_convert_type(w & u32(0xFFFF0000), f32))
        los.append(lax.bitcast_convert_type(w << 16, f32))
    return jnp.concatenate(his + los, axis=1)


def _outproj_kernel(ya_ref, yg_ref, x_ref, w_ref, bo_ref, g_ref, b_ref, wr_ref, br_ref,
                    x1p_ref, logit_ref, z_ref):
    f32 = jnp.float32
    wr = wr_ref[...]
    wr_hi = wr.astype(jnp.bfloat16)
    wr_lo = (wr - wr_hi.astype(f32)).astype(jnp.bfloat16)
    n_sub = OUT_BM // OUT_SUB

    def project(k):
        rows = slice(k * OUT_SUB, (k + 1) * OUT_SUB)
        z = jnp.dot(ya_ref[rows, :], w_ref[0:D_CONV, :], preferred_element_type=f32)
        z_ref[rows, :] = z + jnp.dot(yg_ref[rows, :], w_ref[D_CONV:, :], preferred_element_type=f32)

    def finish(k):
        rows = slice(k * OUT_SUB, (k + 1) * OUT_SUB)
        z = z_ref[rows, :] + bo_ref[...] + ALPHA * x_ref[rows, :]
        mu = jnp.mean(z, axis=-1, keepdims=True)
        d = z - mu
        var = jnp.mean(d * d, axis=-1, keepdims=True)
        x1 = d * lax.rsqrt(var + LN_EPS) * g_ref[...] + b_ref[...]
        _store_packed_rows(x1p_ref.at[pl.ds(k * OUT_SUB * SLAB, OUT_SUB * SLAB)], x1)
        x_hi = x1.astype(jnp.bfloat16)
        x_lo = (x1 - x_hi.astype(f32)).astype(jnp.bfloat16)
        logits = jnp.dot(x_hi, wr_hi, preferred_element_type=f32)
        logits = logits + (jnp.dot(x_lo, wr_hi, preferred_element_type=f32)
                           + jnp.dot(x_hi, wr_lo, preferred_element_type=f32))
        logit_ref[:, rows] = (logits + br_ref[...]).T

    project(0)
    for k in range(n_sub):
        if k + 1 < n_sub:
            project(k + 1)
        finish(k)


def _outproj_call(ya, yg, x2d, w_out_b, b_out, ln_g, ln_b, w_router, b_router):
    grid = (SEQ // OUT_BM,)
    row_half = pl.BlockSpec((OUT_BM, D_CONV), lambda i: (i, 0))
    row_full = pl.BlockSpec((OUT_BM, D_MODEL), lambda i: (i, 0))
    vec = pl.BlockSpec((1, D_MODEL), lambda i: (0, 0))
    in_specs = [
        row_half, row_half, row_full,
        pl.BlockSpec((D_MODEL, D_MODEL), lambda i: (0, 0), pipeline_mode=pl.Buffered(1)),
        vec, vec, vec,
        pl.BlockSpec((D_MODEL, ROUTER_COLS), lambda i: (0, 0), pipeline_mode=pl.Buffered(1)),
        pl.BlockSpec((1, ROUTER_COLS), lambda i: (0, 0)),
    ]
    return pl.pallas_call(
        _outproj_kernel,
        grid=grid,
        in_specs=in_specs,
        out_specs=[pl.BlockSpec((OUT_BM * SLAB, 128), lambda i: (i, 0)),
                   pl.BlockSpec((ROUTER_COLS, OUT_BM), lambda i: (0, i))],
        out_shape=[jax.ShapeDtypeStruct((SEQ * SLAB, 128), jnp.uint32),
                   jax.ShapeDtypeStruct((ROUTER_COLS, SEQ), jnp.float32)],
        scratch_shapes=[pltpu.VMEM((OUT_BM, D_MODEL), jnp.float32)],
        compiler_params=pltpu.CompilerParams(
            dimension_semantics=("arbitrary",),
            vmem_limit_bytes=VMEM_LIMIT_BYTES),
        name="outproj",
    )(ya, yg, x2d, w_out_b, b_out, ln_g, ln_b, w_router, b_router)


def _dispatch_kernel(pos_ref, pad_ref, x1p_hbm, xs_hbm, buf, zbuf, lsem, ssem, zsem, tsem):
    i = pl.program_id(0)
    n_steps = pl.num_programs(0)
    rows = DISP_BM * SLAB
    n_slots = 3

    def load(tile, slot):
        src = pl.multiple_of(tile * rows, rows)
        return pltpu.make_async_copy(x1p_hbm.at[pl.ds(src, rows)], buf.at[slot], lsem.at[slot])

    def scatter_wait(slot):
        for _ in range(TOP_K):
            pltpu.make_async_copy(buf.at[slot], xs_hbm.at[pl.ds(0, rows)], ssem.at[slot]).wait()

    def pad_copy(r, n_rows):
        dst = pl.multiple_of(r * SLAB, SLAB)
        return pltpu.make_async_copy(zbuf.at[pl.ds(0, n_rows * SLAB)],
                                     xs_hbm.at[pl.ds(dst, n_rows * SLAB)], zsem)

    def tail_copy(blk):
        dst = pl.multiple_of(blk * (ROW_BLOCK * SLAB), ROW_BLOCK * SLAB)
        return pltpu.make_async_copy(zbuf, xs_hbm.at[pl.ds(dst, ROW_BLOCK * SLAB)], tsem)

    def for_each_tail_block(fn):
        used_blocks = lax.shift_right_logical(pad_ref[1, N_EXPERTS - 1], ROW_BLOCK.bit_length() - 1)

        def per_block(blk, carry):
            fn(blk)
            return carry
        lax.fori_loop(used_blocks, N_BLOCKS, per_block, 0)

    def for_each_pad_piece(fn):
        def per_expert(e, carry):
            lo = pad_ref[0, e]
            n_pad = pad_ref[1, e] - lo
            for bit in range(ROW_BLOCK.bit_length() - 1):
                size = 1 << bit

                @pl.when((n_pad & size) != 0)
                def _():
                    fn(lo + (n_pad & (size - 1)), size)
            return carry
        lax.fori_loop(0, N_EXPERTS, per_expert, 0)

    @pl.when(i == 0)
    def _():
        load(0, 0).start()
        zbuf[...] = jnp.zeros(zbuf.shape, zbuf.dtype)
        for_each_pad_piece(lambda r, n_rows: pad_copy(r, n_rows).start())
        for_each_tail_block(lambda blk: tail_copy(blk).start())

    slot = lax.rem(i, n_slots)

    @pl.when(i + 1 < n_steps)
    def _():
        load(i + 1, lax.rem(i + 1, n_slots)).start()

    load(i, slot).wait()

    def issue(r, carry):
        src = pl.multiple_of(r * SLAB, SLAB)
        for k in range(TOP_K):
            dst = pl.multiple_of(pos_ref[k, i * DISP_BM + r] * SLAB, SLAB)
            pltpu.make_async_copy(buf.at[slot, pl.ds(src, SLAB)], xs_hbm.at[pl.ds(dst, SLAB)],
                                  ssem.at[slot]).start(priority=k % 2)
        return carry
    lax.fori_loop(0, DISP_BM, issue, 0, unroll=4)

    @pl.when(i >= 1)
    def _():
        scatter_wait(lax.rem(i + n_slots - 1, n_slots))

    @pl.when(i == n_steps - 1)
    def _():
        scatter_wait(slot)
        for_each_pad_piece(lambda r, n_rows: pad_copy(r, n_rows).wait())
        for_each_tail_block(lambda blk: tail_copy(blk).wait())


def _dispatch_call(pos, pad, x1p):
    any_spec = pl.BlockSpec(memory_space=pl.ANY)
    grid_spec = pltpu.PrefetchScalarGridSpec(
        num_scalar_prefetch=2,
        grid=(SEQ // DISP_BM,),
        in_specs=[any_spec],
        out_specs=any_spec,
        scratch_shapes=[pltpu.VMEM((3, DISP_BM * SLAB, 128), jnp.uint32),
                        pltpu.VMEM((ROW_BLOCK * SLAB, 128), jnp.uint32),
                        pltpu.SemaphoreType.DMA((3,)),
                        pltpu.SemaphoreType.DMA((3,)),
                        pltpu.SemaphoreType.DMA(()),
                        pltpu.SemaphoreType.DMA(())],
    )
    return pl.pallas_call(
        _dispatch_kernel,
        grid_spec=grid_spec,
        out_shape=jax.ShapeDtypeStruct((N_BLOCKS * ROW_BLOCK * SLAB, 128), jnp.uint32),
        compiler_params=pltpu.CompilerParams(
            dimension_semantics=("arbitrary",),
            vmem_limit_bytes=VMEM_LIMIT_BYTES),
        name="dispatch",
    )(pos, pad, x1p)


def _expert_kernel(blk_ref, x_ref, wg_hbm, wu_hbm, wd_hbm, out_ref, wgb, wub, wdb, wsem):
    b = pl.program_id(0)
    n_used = blk_ref[BLK_USED, 0]

    def weight_copies(e, s):
        return (pltpu.make_async_copy(wg_hbm.at[e], wgb.at[s], wsem.at[s, 0]),
                pltpu.make_async_copy(wu_hbm.at[e], wub.at[s], wsem.at[s, 1]),
                pltpu.make_async_copy(wd_hbm.at[e], wdb.at[s], wsem.at[s, 2]))

    @pl.when(b == 0)
    def _():
        for c in weight_copies(blk_ref[BLK_EXPERT, 0], 0):
            c.start(priority=1)

    @pl.when(b < n_used)
    def _():
        ws = blk_ref[BLK_SLOT, b]

        @pl.when(blk_ref[BLK_FIRST, b] == 1)
        def _():
            @pl.when(blk_ref[BLK_NEXT, b] >= 0)
            def _():
                for c in weight_copies(blk_ref[BLK_NEXT, b], 1 - ws):
                    c.start(priority=1)
            for c in weight_copies(blk_ref[BLK_EXPERT, b], ws):
                c.wait()

        xb = _load_packed_rows(x_ref, ROW_BLOCK).astype(jnp.bfloat16)
        g = jnp.dot(xb, wgb[ws], preferred_element_type=jnp.float32)
        u = jnp.dot(xb, wub[ws], preferred_element_type=jnp.float32)
        h = jax.nn.silu(g) * u
        _store_packed_rows(out_ref, jnp.dot(h, wdb[ws], preferred_element_type=jnp.float32))

    @pl.when(b >= n_used)
    def _():
        out_ref[...] = jnp.zeros(out_ref.shape, out_ref.dtype)


def _expert_call(blk, xs, w_gate, w_up, w_down):
    any_spec = pl.BlockSpec(memory_space=pl.ANY)
    block = (ROW_BLOCK * SLAB, 128)
    grid_spec = pltpu.PrefetchScalarGridSpec(
        num_scalar_prefetch=1,
        grid=(N_BLOCKS,),
        in_specs=[pl.BlockSpec(block, lambda b, blk: (jnp.minimum(b, blk[BLK_USED, 0] - 1), 0)),
                  any_spec, any_spec, any_spec],
        out_specs=pl.BlockSpec(block, lambda b, blk: (b, 0)),
        scratch_shapes=[pltpu.VMEM((2, D_MODEL, D_EXPERT), jnp.float32),
                        pltpu.VMEM((2, D_MODEL, D_EXPERT), jnp.float32),
                        pltpu.VMEM((2, D_EXPERT, D_MODEL), jnp.float32),
                        pltpu.SemaphoreType.DMA((2, 3))],
    )
    return pl.pallas_call(
        _expert_kernel,
        grid_spec=grid_spec,
        out_shape=jax.ShapeDtypeStruct((N_BLOCKS * ROW_BLOCK * SLAB, 128), jnp.uint32),
        compiler_params=pltpu.CompilerParams(
            dimension_semantics=("arbitrary",),
            vmem_limit_bytes=VMEM_LIMIT_BYTES),
        name="experts",
    )(blk, xs, w_gate, w_up, w_down)


def _combine_kernel(pos_ref, ys_hbm, x1p_ref, gate_ref, g_ref, b_ref, out_ref, ybuf, sem):
    i = pl.program_id(0)
    n_steps = pl.num_programs(0)
    rows = TOP_K * COMB_BM

    def gather_start(tile, s):
        def issue(r, carry):
            for k in range(TOP_K):
                src = pl.multiple_of(pos_ref[k, tile * COMB_BM + r] * SLAB, SLAB)
                dst = pl.multiple_of((k * COMB_BM + r) * SLAB, SLAB)
                pltpu.make_async_copy(ys_hbm.at[pl.ds(src, SLAB)], ybuf.at[s, pl.ds(dst, SLAB)],
                                      sem.at[s]).start(priority=k % 2)
            return carry
        lax.fori_loop(0, COMB_BM, issue, 0, unroll=4)

    @pl.when(i == 0)
    def _():
        gather_start(0, 0)

    s = lax.rem(i, 2)

    @pl.when(i + 1 < n_steps)
    def _():
        gather_start(i + 1, 1 - s)

    pltpu.make_async_copy(ys_hbm.at[pl.ds(0, rows * SLAB)], ybuf.at[s], sem.at[s]).wait()
    gate = gate_ref[...]
    y0 = _load_packed_rows(ybuf.at[s, pl.ds(0, COMB_BM * SLAB)], COMB_BM)
    y1 = _load_packed_rows(ybuf.at[s, pl.ds(COMB_BM * SLAB, COMB_BM * SLAB)], COMB_BM)
    ffn = y0 * gate[:, 0:1] + y1 * gate[:, 1:2]
    z = ALPHA * _load_packed_rows(x1p_ref, COMB_BM) + ffn
    mu = jnp.mean(z, axis=-1, keepdims=True)
    d = z - mu
    var = jnp.mean(d * d, axis=-1, keepdims=True)
    out_ref[...] = d * lax.rsqrt(var + LN_EPS) * g_ref[...] + b_ref[...]


def _combine_call(pos, ys, x1p, gates, ln_g, ln_b):
    row_full = pl.BlockSpec((COMB_BM, D_MODEL), lambda i, pos: (i, 0))
    vec = pl.BlockSpec((1, D_MODEL), lambda i, pos: (0, 0))
    grid_spec = pltpu.PrefetchScalarGridSpec(
        num_scalar_prefetch=1,
        grid=(SEQ // COMB_BM,),
        in_specs=[
            pl.BlockSpec(memory_space=pl.ANY),
            pl.BlockSpec((COMB_BM * SLAB, 128), lambda i, pos: (i, 0)),
            pl.BlockSpec((COMB_BM, TOP_K), lambda i, pos: (i, 0)),
            vec, vec,
        ],
        out_specs=row_full,
        scratch_shapes=[pltpu.VMEM((2, TOP_K * COMB_BM * SLAB, 128), jnp.uint32),
                        pltpu.SemaphoreType.DMA((2,))],
    )
    return pl.pallas_call(
        _combine_kernel,
        grid_spec=grid_spec,
        out_shape=jax.ShapeDtypeStruct((SEQ, D_MODEL), jnp.float32),
        compiler_params=pltpu.CompilerParams(
            dimension_semantics=("arbitrary",),
            vmem_limit_bytes=VMEM_LIMIT_BYTES),
        name="combine",
    )(pos, ys, x1p, gates, ln_g, ln_b)


def _route_kernel(lt_ref, gates_ref, pos_ref, blk_ref, pad_ref, sel_ref, rank_ref):
    f32, i32 = jnp.float32, jnp.int32
    neg = jnp.float32(-jnp.inf)
    n_chunks = SEQ // ROUTE_TC
    erow = lax.broadcasted_iota(i32, (N_EXPERTS, ROUTE_TC), 0)
    grow = lax.broadcasted_iota(i32, (N_GROUPS, ROUTE_TC), 0)
    src_tok = lax.broadcasted_iota(i32, (ROUTE_TC, ROUTE_TC), 0)
    dst_tok = lax.broadcasted_iota(i32, (ROUTE_TC, ROUTE_TC), 1)
    before = (src_tok < dst_tok).astype(jnp.bfloat16)

    def first_index(mask, idx, size):
        return jnp.min(jnp.where(mask, idx, size), axis=0, keepdims=True)

    counts = jnp.zeros((N_EXPERTS, 1), f32)
    for c in range(n_chunks):
        lanes = slice(c * ROUTE_TC, (c + 1) * ROUTE_TC)
        gl = lt_ref[0:N_GROUPS, lanes]
        g_max = jnp.max(gl, axis=0, keepdims=True)
        g_sel = first_index(gl == g_max, grow, N_GROUPS)
        g_weight = 1.0 / jnp.sum(jnp.exp(gl - g_max), axis=0, keepdims=True)
        el = lt_ref[N_GROUPS:N_GROUPS + N_EXPERTS, lanes]
        in_group = (erow >= g_sel * EXPERTS_PER_GROUP) & (erow < (g_sel + 1) * EXPERTS_PER_GROUP)
        el = jnp.where(in_group, el, neg)
        t1 = jnp.max(el, axis=0, keepdims=True)
        e1 = first_index(el == t1, erow, N_EXPERTS)
        el2 = jnp.where(erow == e1, neg, el)
        t2 = jnp.max(el2, axis=0, keepdims=True)
        e2 = first_index(el2 == t2, erow, N_EXPERTS)
        r = jnp.exp(t2 - t1)
        gates_ref[0:1, lanes] = g_weight * (1.0 / (1.0 + r))
        gates_ref[1:2, lanes] = g_weight * (r / (1.0 + r))
        oh1 = erow == e1
        oh2 = erow == e2
        hits = oh1.astype(jnp.bfloat16) + oh2.astype(jnp.bfloat16)
        ahead = jnp.dot(hits, before, preferred_element_type=f32) + counts
        rank_ref[0:1, lanes] = jnp.sum(jnp.where(oh1, ahead, 0.0), axis=0, keepdims=True)
        rank_ref[1:2, lanes] = jnp.sum(jnp.where(oh2, ahead, 0.0), axis=0, keepdims=True)
        sel_ref[0:1, lanes] = e1
        sel_ref[1:2, lanes] = e2
        counts = counts + jnp.sum(hits.astype(f32), axis=1, keepdims=True)

    n_blk = lax.shift_right_logical(counts.astype(i32) + (ROW_BLOCK - 1), ROW_BLOCK.bit_length() - 1)
    e_src = lax.broadcasted_iota(i32, (N_EXPERTS, N_EXPERTS), 1)
    e_dst = lax.broadcasted_iota(i32, (N_EXPERTS, N_EXPERTS), 0)
    incl = (e_src <= e_dst).astype(jnp.bfloat16)
    n_blk_b = jnp.broadcast_to(n_blk.astype(f32), (N_EXPERTS, 128)).astype(jnp.bfloat16)
    blk_end = jnp.dot(incl, n_blk_b, preferred_element_type=f32)[:, 0:1].astype(i32)
    blk_start = blk_end - n_blk
    row_start = (blk_start * ROW_BLOCK).astype(f32)
    e_sub128 = lax.broadcasted_iota(i32, (N_EXPERTS, 128), 0)
    e_lane128 = lax.broadcasted_iota(i32, (N_EXPERTS, 128), 1)
    diag = e_sub128 == e_lane128
    pad_lo = jnp.sum(jnp.where(diag, blk_start * ROW_BLOCK + counts.astype(i32), 0), axis=0, keepdims=True)
    pad_hi = jnp.sum(jnp.where(diag, blk_end * ROW_BLOCK, 0), axis=0, keepdims=True)
    pad_ref[...] = jnp.concatenate([pad_lo, pad_hi, jnp.zeros((6, 128), i32)], axis=0)

    for c in range(n_chunks):
        lanes = slice(c * ROUTE_TC, (c + 1) * ROUTE_TC)
        for k in range(TOP_K):
            oh = erow == sel_ref[k:k + 1, lanes]
            start = jnp.sum(jnp.where(oh, row_start, 0.0), axis=0, keepdims=True)
            pos_ref[k:k + 1, lanes] = (start + rank_ref[k:k + 1, lanes]).astype(i32)

    b_lane = lax.broadcasted_iota(i32, (N_EXPERTS, BLK_LANES), 1)
    e_sub = lax.broadcasted_iota(i32, (N_EXPERTS, BLK_LANES), 0)
    used = n_blk > 0
    block_expert = jnp.minimum(jnp.sum((blk_end <= b_lane).astype(i32), axis=0, keepdims=True), N_EXPERTS - 1)
    first = jnp.sum((used & (blk_start == b_lane)).astype(i32), axis=0, keepdims=True)
    run_index = jnp.sum((used & (e_sub < block_expert)).astype(i32), axis=0, keepdims=True)
    nxt = jnp.min(jnp.where(used & (e_sub > block_expert), e_sub, N_EXPERTS), axis=0, keepdims=True)
    nxt = jnp.where(nxt >= N_EXPERTS, -1, nxt)
    n_used = jnp.broadcast_to(blk_end[N_EXPERTS - 1:N_EXPERTS, :], (1, BLK_LANES))
    blk_ref[...] = jnp.concatenate(
        [block_expert, first, run_index & 1, nxt, n_used, jnp.zeros((3, BLK_LANES), i32)], axis=0)


def _route_call(logits_t):
    return pl.pallas_call(
        _route_kernel,
        out_shape=[jax.ShapeDtypeStruct((TOP_K, SEQ), jnp.float32),
                   jax.ShapeDtypeStruct((TOP_K, SEQ), jnp.int32),
                   jax.ShapeDtypeStruct((8, BLK_LANES), jnp.int32),
                   jax.ShapeDtypeStruct((8, 128), jnp.int32)],
        scratch_shapes=[pltpu.VMEM((TOP_K, SEQ), jnp.int32),
                        pltpu.VMEM((TOP_K, SEQ), jnp.float32)],
        compiler_params=pltpu.CompilerParams(vmem_limit_bytes=VMEM_LIMIT_BYTES),
        name="route",
    )(logits_t)


def kernel(x, w_in, b_in, conv_w, conv_b, conv_norm_g, conv_norm_b, gmlp_norm_g, gmlp_norm_b,
           w_spatial, b_spatial, w_out, b_out, ln1_g, ln1_b, w_router_group, b_router_group,
           w_router_expert, b_router_expert, w_expert_gate, w_expert_up, w_expert_down,
           ln2_g, ln2_b):
    assert x.shape == (1, SEQ, D_MODEL) and w_in.shape[0] == 1
    x2d = x.reshape(SEQ, D_MODEL)
    ya, yg = _mixer_call(
        x2d, w_in[0].astype(jnp.bfloat16), b_in, conv_w[0], conv_b, conv_norm_g, conv_norm_b,
        gmlp_norm_g, gmlp_norm_b, w_spatial[0], b_spatial[0][:, :, None])

    pad = ROUTER_COLS - N_GROUPS - N_EXPERTS
    w_router = jnp.concatenate(
        [w_router_group[0],
         jnp.transpose(w_router_expert[0], (1, 0, 2)).reshape(D_MODEL, N_EXPERTS),
         jnp.zeros((D_MODEL, pad), jnp.float32)], axis=1)
    b_router = jnp.concatenate(
        [b_router_group[0], b_router_expert[0].reshape(-1), jnp.zeros((pad,), jnp.float32)])[None, :]
    x1p, logits_t = _outproj_call(ya, yg, x2d, w_out[0].astype(jnp.bfloat16), b_out, ln1_g, ln1_b,
                                    w_router, b_router)

    gates_t, pos, blk, pad = _route_call(logits_t)
    xs = _dispatch_call(pos, pad, x1p)
    ys = _expert_call(blk, xs, w_expert_gate[0], w_expert_up[0], w_expert_down[0])
    out = _combine_call(pos, ys, x1p, gates_t.T, ln2_g, ln2_b)
    return out.reshape(1, SEQ, D_MODEL)
```

```python
import jax
import jax.numpy as jnp
from jax import lax
from jax.experimental import pallas as pl
from jax.experimental.pallas import tpu as pltpu

D_MODEL = 4096
SEQ = 8192
D_CONV = D_MODEL // 2
D_GMLP = D_MODEL // 2
CONV_WIDTH = 31
GROUP_DIM = 128
GMLP_HEADS = 16
CHUNK = 128
N_GROUPS = 8
EXPERTS_PER_GROUP = 8
N_EXPERTS = N_GROUPS * EXPERTS_PER_GROUP
TOP_K = 2
D_EXPERT = D_MODEL // 8
ROW_BLOCK = 128
LN_EPS = 1e-5
ALPHA = 2.0 ** 0.25

N_ROWS = SEQ * TOP_K
N_BLOCKS = N_ROWS // ROW_BLOCK + N_EXPERTS
ROUTER_COLS = 128
SLAB = D_MODEL // 2 // 128

VMEM_LIMIT_BYTES = 60 * 1024 * 1024

MIX_BM = 1024
MIX_SUB = 256
MIX_BN = 256
HALO = 32
CONV_ROWS = 128
SHIFT_ROWS = CONV_ROWS + HALO - 8

OUT_BM = 256
OUT_SUB = 128
ROUTE_TC = 512
BLK_LANES = 256
BLK_EXPERT, BLK_FIRST, BLK_SLOT, BLK_NEXT, BLK_USED = range(5)
COMB_BM = 256
DISP_BM = 512


def _group_norm(v, g, b):
    outs = []
    for s in range(v.shape[1] // GROUP_DIM):
        blk = v[:, s * GROUP_DIM:(s + 1) * GROUP_DIM]
        mu = jnp.mean(blk, axis=-1, keepdims=True)
        d = blk - mu
        var = jnp.mean(d * d, axis=-1, keepdims=True)
        outs.append(d * lax.rsqrt(var + LN_EPS))
    return jnp.concatenate(outs, axis=-1) * g + b


def _mixer_kernel(x_ref, wa_ref, wg_ref, wu_ref, wv_ref, ba_ref, bg_ref, bu_ref, bv_ref,
                  cw_ref, cb_ref, cng_ref, cnb_ref, gng_ref, gnb_ref, wsp_ref, bsp_ref,
                  ya_ref, yg_ref, xs_ref, r_ref, sh_ref):
    i = pl.program_id(1)
    f32 = jnp.float32
    bf16 = jnp.bfloat16
    w_refs = (wa_ref, wg_ref, wu_ref, wv_ref)
    off = HALO - (CONV_WIDTH - 1)

    @pl.when(i == 0)
    def _():
        xs_ref[0:HALO, :] = jnp.zeros((HALO, MIX_BN), f32)

    row = lax.broadcasted_iota(jnp.int32, (CHUNK, CHUNK), 0)
    col = lax.broadcasted_iota(jnp.int32, (CHUNK, CHUNK), 1)
    w_sp = [jnp.where(row >= col, wsp_ref[h], 0.0).astype(bf16) for h in range(MIX_BN // GROUP_DIM)]

    def project(k):
        xk = x_ref[k * MIX_SUB:(k + 1) * MIX_SUB, :].astype(bf16)
        for q in range(4):
            r_ref[k % 2, q] = jnp.dot(xk, w_refs[q][...], preferred_element_type=f32)

    def finish_conv(k, c):
        t0 = k * MIX_SUB + c * CONV_ROWS
        rows = slice(c * CONV_ROWS, (c + 1) * CONV_ROWS)
        a = r_ref[k % 2, 0, rows, :] + ba_ref[...]
        gate = r_ref[k % 2, 1, rows, :] + bg_ref[...]
        xs_ref[HALO + t0:HALO + t0 + CONV_ROWS, :] = a * jax.nn.sigmoid(gate)
        sh = sh_ref.at[c % 2]
        for m in range(1, 8):
            sh[m - 1] = xs_ref[t0 + m:t0 + m + SHIFT_ROWS, :]
        acc = jnp.zeros((CONV_ROWS, MIX_BN), f32) + cb_ref[...]
        for k2 in range(CONV_WIDTH):
            m, j8 = (off + k2) % 8, 8 * ((off + k2) // 8)
            if m == 0:
                src = xs_ref[t0 + j8:t0 + j8 + CONV_ROWS, :]
            else:
                src = sh[m - 1, j8:j8 + CONV_ROWS, :]
            acc = acc + cw_ref[k2:k2 + 1, :] * src
        y = jax.nn.silu(_group_norm(acc, cng_ref[...], cnb_ref[...]))
        ya_ref[t0:t0 + CONV_ROWS, :] = y.astype(ya_ref.dtype)

    def finish_gmlp(k):
        base = k * MIX_SUB
        n_chunks = MIX_SUB // CHUNK
        us, vs = [], []
        for c in range(n_chunks):
            rows = slice(c * CHUNK, (c + 1) * CHUNK)
            us.append(jax.nn.gelu(r_ref[k % 2, 2, rows, :] + bu_ref[...]))
            v = jax.nn.gelu(r_ref[k % 2, 3, rows, :] + bv_ref[...])
            vs.append(_group_norm(v, gng_ref[...], gnb_ref[...]).astype(bf16))
        for h in range(MIX_BN // GROUP_DIM):
            lanes = slice(h * GROUP_DIM, (h + 1) * GROUP_DIM)
            v_h = jnp.concatenate([v[:, lanes] for v in vs], axis=1)
            s_h = jnp.dot(w_sp[h], v_h, preferred_element_type=f32) + bsp_ref[h]
            for c in range(n_chunks):
                g = us[c][:, lanes] * s_h[:, c * GROUP_DIM:(c + 1) * GROUP_DIM]
                yg_ref[base + c * CHUNK:base + (c + 1) * CHUNK, lanes] = g.astype(yg_ref.dtype)

    def finish(k):
        for c in range(MIX_SUB // CONV_ROWS):
            finish_conv(k, c)
        finish_gmlp(k)

    n_sub = MIX_BM // MIX_SUB
    for k in range(n_sub + 1):
        if k < n_sub:
            project(k)
        if k > 0:
            finish(k - 1)
    xs_ref[0:HALO, :] = xs_ref[MIX_BM:MIX_BM + HALO, :]


def _mixer_call(x2d, w_in_b, b_in, conv_w, conv_b, cng, cnb, gng, gnb, w_spatial, b_spatial_col):
    nq = D_CONV // MIX_BN
    grid = (nq, SEQ // MIX_BM)

    def wspec(q):
        return pl.BlockSpec((D_MODEL, MIX_BN), lambda j, i, q=q: (0, q * nq + j))

    def bspec(q):
        return pl.BlockSpec((1, MIX_BN), lambda j, i, q=q: (0, q * nq + j))

    vec = pl.BlockSpec((1, MIX_BN), lambda j, i: (0, j))
    heads_per_step = MIX_BN // GROUP_DIM
    in_specs = [
        pl.BlockSpec((MIX_BM, D_MODEL), lambda j, i: (i, 0)),
        wspec(0), wspec(1), wspec(2), wspec(3),
        bspec(0), bspec(1), bspec(2), bspec(3),
        pl.BlockSpec((CONV_WIDTH, MIX_BN), lambda j, i: (0, j)),
        vec, vec, vec, vec, vec,
        pl.BlockSpec((heads_per_step, CHUNK, CHUNK), lambda j, i: (j, 0, 0)),
        pl.BlockSpec((heads_per_step, CHUNK, 1), lambda j, i: (j, 0, 0)),
    ]
    out_spec = pl.BlockSpec((MIX_BM, MIX_BN), lambda j, i: (i, j))
    return pl.pallas_call(
        _mixer_kernel,
        grid=grid,
        in_specs=in_specs,
        out_specs=[out_spec, out_spec],
        out_shape=[jax.ShapeDtypeStruct((SEQ, D_CONV), jnp.bfloat16),
                   jax.ShapeDtypeStruct((SEQ, D_GMLP), jnp.bfloat16)],
        scratch_shapes=[pltpu.VMEM((HALO + MIX_BM, MIX_BN), jnp.float32),
                        pltpu.VMEM((2, 4, MIX_SUB, MIX_BN), jnp.float32),
                        pltpu.VMEM((2, 7, SHIFT_ROWS, MIX_BN), jnp.float32)],
        compiler_params=pltpu.CompilerParams(
            dimension_semantics=("arbitrary", "arbitrary"),
            vmem_limit_bytes=VMEM_LIMIT_BYTES),
        name="mixer",
    )(x2d, w_in_b, w_in_b, w_in_b, w_in_b, b_in, b_in, b_in, b_in,
      conv_w, conv_b, cng, cnb, gng, gnb, w_spatial, b_spatial_col)


def _store_packed_rows(dst_ref, x):
    _store_packed_rounded_rows(dst_ref, x.astype(jnp.bfloat16).astype(jnp.float32))


def _store_packed_rounded_rows(dst_ref, x):
    half = D_MODEL // 2
    bits = lax.bitcast_convert_type(x, jnp.uint32)
    words = bits[:, :half] | (bits[:, half:] >> 16)
    for j in range(SLAB):
        dst_ref[pl.ds(j, x.shape[0], stride=SLAB), :] = words[:, j * 128:(j + 1) * 128]


def _load_packed_rows(src_ref, rows):
    f32, u32 = jnp.float32, jnp.uint32
    his, los = [], []
    for j in range(SLAB):
        w = src_ref[pl.ds(j, rows, stride=SLAB), :]
        his.append(lax.bitcast_convert_type(w & u32(0xFFFF0000), f32))
        los.append(lax.bitcast_convert_type(w << 16, f32))
    return jnp.concatenate(his + los, axis=1)


def _outproj_kernel(ya_ref, yg_ref, x_ref, w_ref, bo_ref, g_ref, b_ref, wr_ref, br_ref,
                    x1p_ref, logit_ref, z_ref):
    f32 = jnp.float32
    wr = wr_ref[...]
    wr_hi = wr.astype(jnp.bfloat16)
    wr_lo = (wr - wr_hi.astype(f32)).astype(jnp.bfloat16)
    wr_split = jnp.concatenate([wr_hi, wr_lo], axis=1)
    n_sub = OUT_BM // OUT_SUB

    def project(k):
        rows = slice(k * OUT_SUB, (k + 1) * OUT_SUB)
        z = jnp.dot(ya_ref[rows, :], w_ref[0:D_CONV, :], preferred_element_type=f32)
        z_ref[rows, :] = z + jnp.dot(yg_ref[rows, :], w_ref[D_CONV:, :], preferred_element_type=f32)

    def finish(k):
        rows = slice(k * OUT_SUB, (k + 1) * OUT_SUB)
        z = z_ref[rows, :] + bo_ref[...] + ALPHA * x_ref[rows, :]
        mu = jnp.mean(z, axis=-1, keepdims=True)
        d = z - mu
        var = jnp.mean(d * d, axis=-1, keepdims=True)
        x1 = d * lax.rsqrt(var + LN_EPS) * g_ref[...] + b_ref[...]
        x_hi = x1.astype(jnp.bfloat16)
        x1_rounded = x_hi.astype(f32)
        _store_packed_rounded_rows(x1p_ref.at[pl.ds(k * OUT_SUB * SLAB, OUT_SUB * SLAB)], x1_rounded)
        x_lo = (x1 - x1_rounded).astype(jnp.bfloat16)
        parts = jnp.dot(jnp.concatenate([x_hi, x_lo], axis=0), wr_split, preferred_element_type=f32)
        logits = ((parts[:OUT_SUB, :ROUTER_COLS] + parts[:OUT_SUB, ROUTER_COLS:])
                  + (parts[OUT_SUB:, :ROUTER_COLS] + parts[OUT_SUB:, ROUTER_COLS:]))
        logit_ref[:, rows] = (logits + br_ref[...]).T

    project(0)
    for k in range(n_sub):
        if k + 1 < n_sub:
            project(k + 1)
        finish(k)


def _outproj_call(ya, yg, x2d, w_out_b, b_out, ln_g, ln_b, w_router, b_router):
    grid = (SEQ // OUT_BM,)
    row_half = pl.BlockSpec((OUT_BM, D_CONV), lambda i: (i, 0))
    row_full = pl.BlockSpec((OUT_BM, D_MODEL), lambda i: (i, 0))
    vec = pl.BlockSpec((1, D_MODEL), lambda i: (0, 0))
    in_specs = [
        row_half, row_half, row_full,
        pl.BlockSpec((D_MODEL, D_MODEL), lambda i: (0, 0), pipeline_mode=pl.Buffered(1)),
        vec, vec, vec,
        pl.BlockSpec((D_MODEL, ROUTER_COLS), lambda i: (0, 0), pipeline_mode=pl.Buffered(1)),
        pl.BlockSpec((1, ROUTER_COLS), lambda i: (0, 0)),
    ]
    return pl.pallas_call(
        _outproj_kernel,
        grid=grid,
        in_specs=in_specs,
        out_specs=[pl.BlockSpec((OUT_BM * SLAB, 128), lambda i: (i, 0)),
                   pl.BlockSpec((ROUTER_COLS, OUT_BM), lambda i: (0, i))],
        out_shape=[jax.ShapeDtypeStruct((SEQ * SLAB, 128), jnp.uint32),
                   jax.ShapeDtypeStruct((ROUTER_COLS, SEQ), jnp.float32)],
        scratch_shapes=[pltpu.VMEM((OUT_BM, D_MODEL), jnp.float32)],
        compiler_params=pltpu.CompilerParams(
            dimension_semantics=("arbitrary",),
            vmem_limit_bytes=VMEM_LIMIT_BYTES),
        name="outproj",
    )(ya, yg, x2d, w_out_b, b_out, ln_g, ln_b, w_router, b_router)


def _dispatch_kernel(pos_ref, pad_ref, x1p_hbm, xs_hbm, buf, zbuf, lsem, ssem, zsem, tsem):
    i = pl.program_id(0)
    n_steps = pl.num_programs(0)
    rows = DISP_BM * SLAB
    n_slots = 3

    def load(tile, slot):
        src = pl.multiple_of(tile * rows, rows)
        return pltpu.make_async_copy(x1p_hbm.at[pl.ds(src, rows)], buf.at[slot], lsem.at[slot])

    def scatter_wait(slot):
        for _ in range(TOP_K):
            pltpu.make_async_copy(buf.at[slot], xs_hbm.at[pl.ds(0, rows)], ssem.at[slot]).wait()

    def pad_copy(r, n_rows):
        dst = pl.multiple_of(r * SLAB, SLAB)
        return pltpu.make_async_copy(zbuf.at[pl.ds(0, n_rows * SLAB)],
                                     xs_hbm.at[pl.ds(dst, n_rows * SLAB)], zsem)

    def tail_copy(blk):
        dst = pl.multiple_of(blk * (ROW_BLOCK * SLAB), ROW_BLOCK * SLAB)
        return pltpu.make_async_copy(zbuf, xs_hbm.at[pl.ds(dst, ROW_BLOCK * SLAB)], tsem)

    def for_each_tail_block(fn):
        used_blocks = lax.shift_right_logical(pad_ref[1, N_EXPERTS - 1], ROW_BLOCK.bit_length() - 1)

        def per_block(blk, carry):
            fn(blk)
            return carry
        lax.fori_loop(used_blocks, N_BLOCKS, per_block, 0)

    def for_each_pad_piece(fn):
        def per_expert(e, carry):
            lo = pad_ref[0, e]
            n_pad = pad_ref[1, e] - lo
            for bit in range(ROW_BLOCK.bit_length() - 1):
                size = 1 << bit

                @pl.when((n_pad & size) != 0)
                def _():
                    fn(lo + (n_pad & (size - 1)), size)
            return carry
        lax.fori_loop(0, N_EXPERTS, per_expert, 0)

    @pl.when(i == 0)
    def _():
        load(0, 0).start()
        zbuf[...] = jnp.zeros(zbuf.shape, zbuf.dtype)
        for_each_pad_piece(lambda r, n_rows: pad_copy(r, n_rows).start())
        for_each_tail_block(lambda blk: tail_copy(blk).start())

    slot = lax.rem(i, n_slots)

    @pl.when(i + 1 < n_steps)
    def _():
        load(i + 1, lax.rem(i + 1, n_slots)).start()

    load(i, slot).wait()

    def issue(r, carry):
        src = pl.multiple_of(r * SLAB, SLAB)
        for k in range(TOP_K):
            dst = pl.multiple_of(pos_ref[k, i * DISP_BM + r] * SLAB, SLAB)
            pltpu.make_async_copy(buf.at[slot, pl.ds(src, SLAB)], xs_hbm.at[pl.ds(dst, SLAB)],
                                  ssem.at[slot]).start(priority=k % 2)
        return carry
    lax.fori_loop(0, DISP_BM, issue, 0, unroll=4)

    @pl.when(i >= 1)
    def _():
        scatter_wait(lax.rem(i + n_slots - 1, n_slots))

    @pl.when(i == n_steps - 1)
    def _():
        scatter_wait(slot)
        for_each_pad_piece(lambda r, n_rows: pad_copy(r, n_rows).wait())
        for_each_tail_block(lambda blk: tail_copy(blk).wait())


def _dispatch_call(pos, pad, x1p):
    any_spec = pl.BlockSpec(memory_space=pl.ANY)
    grid_spec = pltpu.PrefetchScalarGridSpec(
        num_scalar_prefetch=2,
        grid=(SEQ // DISP_BM,),
        in_specs=[any_spec],
        out_specs=any_spec,
        scratch_shapes=[pltpu.VMEM((3, DISP_BM * SLAB, 128), jnp.uint32),
                        pltpu.VMEM((ROW_BLOCK * SLAB, 128), jnp.uint32),
                        pltpu.SemaphoreType.DMA((3,)),
                        pltpu.SemaphoreType.DMA((3,)),
                        pltpu.SemaphoreType.DMA(()),
                        pltpu.SemaphoreType.DMA(())],
    )
    return pl.pallas_call(
        _dispatch_kernel,
        grid_spec=grid_spec,
        out_shape=jax.ShapeDtypeStruct((N_BLOCKS * ROW_BLOCK * SLAB, 128), jnp.uint32),
        compiler_params=pltpu.CompilerParams(
            dimension_semantics=("arbitrary",),
            vmem_limit_bytes=VMEM_LIMIT_BYTES),
        name="dispatch",
    )(pos, pad, x1p)


def _expert_kernel(blk_ref, x_ref, wg_hbm, wu_hbm, wd_hbm, out_ref, wgb, wub, wdb, wsem):
    b = pl.program_id(0)
    n_used = blk_ref[BLK_USED, 0]

    def weight_copies(e, s):
        return (pltpu.make_async_copy(wg_hbm.at[e], wgb.at[s], wsem.at[s, 0]),
                pltpu.make_async_copy(wu_hbm.at[e], wub.at[s], wsem.at[s, 1]),
                pltpu.make_async_copy(wd_hbm.at[e], wdb.at[s], wsem.at[s, 2]))

    @pl.when(b == 0)
    def _():
        for c in weight_copies(blk_ref[BLK_EXPERT, 0], 0):
            c.start(priority=1)

    @pl.when(b < n_used)
    def _():
        ws = blk_ref[BLK_SLOT, b]

        @pl.when(blk_ref[BLK_FIRST, b] == 1)
        def _():
            @pl.when(blk_ref[BLK_NEXT, b] >= 0)
            def _():
                for c in weight_copies(blk_ref[BLK_NEXT, b], 1 - ws):
                    c.start(priority=1)
            for c in weight_copies(blk_ref[BLK_EXPERT, b], ws):
                c.wait()

        xb = _load_packed_rows(x_ref, ROW_BLOCK).astype(jnp.bfloat16)
        g = jnp.dot(xb, wgb[ws], preferred_element_type=jnp.float32)
        u = jnp.dot(xb, wub[ws], preferred_element_type=jnp.float32)
        h = jax.nn.silu(g) * u
        _store_packed_rows(out_ref, jnp.dot(h, wdb[ws], preferred_element_type=jnp.float32))

    @pl.when(b >= n_used)
    def _():
        out_ref[...] = jnp.zeros(out_ref.shape, out_ref.dtype)


def _expert_call(blk, xs, w_gate, w_up, w_down):
    any_spec = pl.BlockSpec(memory_space=pl.ANY)
    block = (ROW_BLOCK * SLAB, 128)
    grid_spec = pltpu.PrefetchScalarGridSpec(
        num_scalar_prefetch=1,
        grid=(N_BLOCKS,),
        in_specs=[pl.BlockSpec(block, lambda b, blk: (jnp.minimum(b, blk[BLK_USED, 0] - 1), 0)),
                  any_spec, any_spec, any_spec],
        out_specs=pl.BlockSpec(block, lambda b, blk: (b, 0)),
        scratch_shapes=[pltpu.VMEM((2, D_MODEL, D_EXPERT), jnp.float32),
                        pltpu.VMEM((2, D_MODEL, D_EXPERT), jnp.float32),
                        pltpu.VMEM((2, D_EXPERT, D_MODEL), jnp.float32),
                        pltpu.SemaphoreType.DMA((2, 3))],
    )
    return pl.pallas_call(
        _expert_kernel,
        grid_spec=grid_spec,
        out_shape=jax.ShapeDtypeStruct((N_BLOCKS * ROW_BLOCK * SLAB, 128), jnp.uint32),
        compiler_params=pltpu.CompilerParams(
            dimension_semantics=("arbitrary",),
            vmem_limit_bytes=VMEM_LIMIT_BYTES),
        name="experts",
    )(blk, xs, w_gate, w_up, w_down)


def _combine_kernel(pos_ref, ys_hbm, x1p_ref, gate_ref, g_ref, b_ref, out_ref, ybuf, sem):
    i = pl.program_id(0)
    n_steps = pl.num_programs(0)
    rows = TOP_K * COMB_BM

    def gather_start(tile, s):
        def issue(r, carry):
            for k in range(TOP_K):
                src = pl.multiple_of(pos_ref[k, tile * COMB_BM + r] * SLAB, SLAB)
                dst = pl.multiple_of((k * COMB_BM + r) * SLAB, SLAB)
                pltpu.make_async_copy(ys_hbm.at[pl.ds(src, SLAB)], ybuf.at[s, pl.ds(dst, SLAB)],
                                      sem.at[s]).start(priority=k % 2)
            return carry
        lax.fori_loop(0, COMB_BM, issue, 0, unroll=4)

    @pl.when(i == 0)
    def _():
        gather_start(0, 0)

    s = lax.rem(i, 2)

    @pl.when(i + 1 < n_steps)
    def _():
        gather_start(i + 1, 1 - s)

    pltpu.make_async_copy(ys_hbm.at[pl.ds(0, rows * SLAB)], ybuf.at[s], sem.at[s]).wait()
    gate = gate_ref[...]
    y0 = _load_packed_rows(ybuf.at[s, pl.ds(0, COMB_BM * SLAB)], COMB_BM)
    y1 = _load_packed_rows(ybuf.at[s, pl.ds(COMB_BM * SLAB, COMB_BM * SLAB)], COMB_BM)
    ffn = y0 * gate[:, 0:1] + y1 * gate[:, 1:2]
    z = ALPHA * _load_packed_rows(x1p_ref, COMB_BM) + ffn
    mu = jnp.mean(z, axis=-1, keepdims=True)
    d = z - mu
    var = jnp.mean(d * d, axis=-1, keepdims=True)
    out_ref[...] = d * lax.rsqrt(var + LN_EPS) * g_ref[...] + b_ref[...]


def _combine_call(pos, ys, x1p, gates, ln_g, ln_b):
    row_full = pl.BlockSpec((COMB_BM, D_MODEL), lambda i, pos: (i, 0))
    vec = pl.BlockSpec((1, D_MODEL), lambda i, pos: (0, 0))
    grid_spec = pltpu.PrefetchScalarGridSpec(
        num_scalar_prefetch=1,
        grid=(SEQ // COMB_BM,),
        in_specs=[
            pl.BlockSpec(memory_space=pl.ANY),
            pl.BlockSpec((COMB_BM * SLAB, 128), lambda i, pos: (i, 0)),
            pl.BlockSpec((COMB_BM, TOP_K), lambda i, pos: (i, 0)),
            vec, vec,
        ],
        out_specs=row_full,
        scratch_shapes=[pltpu.VMEM((2, TOP_K * COMB_BM * SLAB, 128), jnp.uint32),
                        pltpu.SemaphoreType.DMA((2,))],
    )
    return pl.pallas_call(
        _combine_kernel,
        grid_spec=grid_spec,
        out_shape=jax.ShapeDtypeStruct((SEQ, D_MODEL), jnp.float32),
        compiler_params=pltpu.CompilerParams(
            dimension_semantics=("arbitrary",),
            vmem_limit_bytes=VMEM_LIMIT_BYTES),
        name="combine",
    )(pos, ys, x1p, gates, ln_g, ln_b)


def _route_kernel(lt_ref, gates_ref, pos_ref, blk_ref, pad_ref, sel_ref, rank_ref):
    f32, i32 = jnp.float32, jnp.int32
    neg = jnp.float32(-jnp.inf)
    n_chunks = SEQ // ROUTE_TC
    erow = lax.broadcasted_iota(i32, (N_EXPERTS, ROUTE_TC), 0)
    grow = lax.broadcasted_iota(i32, (N_GROUPS, ROUTE_TC), 0)
    src_tok = lax.broadcasted_iota(i32, (ROUTE_TC, ROUTE_TC), 0)
    dst_tok = lax.broadcasted_iota(i32, (ROUTE_TC, ROUTE_TC), 1)
    before = (src_tok < dst_tok).astype(jnp.bfloat16)

    def first_index(mask, idx, size):
        return jnp.min(jnp.where(mask, idx, size), axis=0, keepdims=True)

    counts = jnp.zeros((N_EXPERTS, 1), f32)
    for c in range(n_chunks):
        lanes = slice(c * ROUTE_TC, (c + 1) * ROUTE_TC)
        gl = lt_ref[0:N_GROUPS, lanes]
        g_max = jnp.max(gl, axis=0, keepdims=True)
        g_sel = first_index(gl == g_max, grow, N_GROUPS)
        g_weight = 1.0 / jnp.sum(jnp.exp(gl - g_max), axis=0, keepdims=True)
        el = lt_ref[N_GROUPS:N_GROUPS + N_EXPERTS, lanes]
        in_group = (erow >= g_sel * EXPERTS_PER_GROUP) & (erow < (g_sel + 1) * EXPERTS_PER_GROUP)
        el = jnp.where(in_group, el, neg)
        t1 = jnp.max(el, axis=0, keepdims=True)
        e1 = first_index(el == t1, erow, N_EXPERTS)
        el2 = jnp.where(erow == e1, neg, el)
        t2 = jnp.max(el2, axis=0, keepdims=True)
        e2 = first_index(el2 == t2, erow, N_EXPERTS)
        r = jnp.exp(t2 - t1)
        gates_ref[0:1, lanes] = g_weight * (1.0 / (1.0 + r))
        gates_ref[1:2, lanes] = g_weight * (r / (1.0 + r))
        oh1 = erow == e1
        oh2 = erow == e2
        hits = oh1.astype(jnp.bfloat16) + oh2.astype(jnp.bfloat16)
        ahead = jnp.dot(hits, before, preferred_element_type=f32) + counts
        rank_ref[0:1, lanes] = jnp.sum(jnp.where(oh1, ahead, 0.0), axis=0, keepdims=True)
        rank_ref[1:2, lanes] = jnp.sum(jnp.where(oh2, ahead, 0.0), axis=0, keepdims=True)
        sel_ref[0:1, lanes] = e1
        sel_ref[1:2, lanes] = e2
        counts = counts + jnp.sum(hits.astype(f32), axis=1, keepdims=True)

    n_blk = lax.shift_right_logical(counts.astype(i32) + (ROW_BLOCK - 1), ROW_BLOCK.bit_length() - 1)
    e_src = lax.broadcasted_iota(i32, (N_EXPERTS, N_EXPERTS), 1)
    e_dst = lax.broadcasted_iota(i32, (N_EXPERTS, N_EXPERTS), 0)
    incl = (e_src <= e_dst).astype(jnp.bfloat16)
    n_blk_b = jnp.broadcast_to(n_blk.astype(f32), (N_EXPERTS, 128)).astype(jnp.bfloat16)
    blk_end = jnp.dot(incl, n_blk_b, preferred_element_type=f32)[:, 0:1].astype(i32)
    blk_start = blk_end - n_blk
    row_start = (blk_start * ROW_BLOCK).astype(f32)
    e_sub128 = lax.broadcasted_iota(i32, (N_EXPERTS, 128), 0)
    e_lane128 = lax.broadcasted_iota(i32, (N_EXPERTS, 128), 1)
    diag = e_sub128 == e_lane128
    pad_lo = jnp.sum(jnp.where(diag, blk_start * ROW_BLOCK + counts.astype(i32), 0), axis=0, keepdims=True)
    pad_hi = jnp.sum(jnp.where(diag, blk_end * ROW_BLOCK, 0), axis=0, keepdims=True)
    pad_ref[...] = jnp.concatenate([pad_lo, pad_hi, jnp.zeros((6, 128), i32)], axis=0)

    for c in range(n_chunks):
        lanes = slice(c * ROUTE_TC, (c + 1) * ROUTE_TC)
        for k in range(TOP_K):
            oh = erow == sel_ref[k:k + 1, lanes]
            start = jnp.sum(jnp.where(oh, row_start, 0.0), axis=0, keepdims=True)
            pos_ref[k:k + 1, lanes] = (start + rank_ref[k:k + 1, lanes]).astype(i32)

    b_lane = lax.broadcasted_iota(i32, (N_EXPERTS, BLK_LANES), 1)
    e_sub = lax.broadcasted_iota(i32, (N_EXPERTS, BLK_LANES), 0)
    used = n_blk > 0
    block_expert = jnp.minimum(jnp.sum((blk_end <= b_lane).astype(i32), axis=0, keepdims=True), N_EXPERTS - 1)
    first = jnp.sum((used & (blk_start == b_lane)).astype(i32), axis=0, keepdims=True)
    run_index = jnp.sum((used & (e_sub < block_expert)).astype(i32), axis=0, keepdims=True)
    nxt = jnp.min(jnp.where(used & (e_sub > block_expert), e_sub, N_EXPERTS), axis=0, keepdims=True)
    nxt = jnp.where(nxt >= N_EXPERTS, -1, nxt)
    n_used = jnp.broadcast_to(blk_end[N_EXPERTS - 1:N_EXPERTS, :], (1, BLK_LANES))
    blk_ref[...] = jnp.concatenate(
        [block_expert, first, run_index & 1, nxt, n_used, jnp.zeros((3, BLK_LANES), i32)], axis=0)


def _route_call(logits_t):
    return pl.pallas_call(
        _route_kernel,
        out_shape=[jax.ShapeDtypeStruct((TOP_K, SEQ), jnp.float32),
                   jax.ShapeDtypeStruct((TOP_K, SEQ), jnp.int32),
                   jax.ShapeDtypeStruct((8, BLK_LANES), jnp.int32),
                   jax.ShapeDtypeStruct((8, 128), jnp.int32)],
        scratch_shapes=[pltpu.VMEM((TOP_K, SEQ), jnp.int32),
                        pltpu.VMEM((TOP_K, SEQ), jnp.float32)],
        compiler_params=pltpu.CompilerParams(vmem_limit_bytes=VMEM_LIMIT_BYTES),
        name="route",
    )(logits_t)


def kernel(x, w_in, b_in, conv_w, conv_b, conv_norm_g, conv_norm_b, gmlp_norm_g, gmlp_norm_b,
           w_spatial, b_spatial, w_out, b_out, ln1_g, ln1_b, w_router_group, b_router_group,
           w_router_expert, b_router_expert, w_expert_gate, w_expert_up, w_expert_down,
           ln2_g, ln2_b):
    assert x.shape == (1, SEQ, D_MODEL) and w_in.shape[0] == 1
    x2d = x.reshape(SEQ, D_MODEL)
    ya, yg = _mixer_call(
        x2d, w_in[0].astype(jnp.bfloat16), b_in, conv_w[0], conv_b, conv_norm_g, conv_norm_b,
        gmlp_norm_g, gmlp_norm_b, w_spatial[0], b_spatial[0][:, :, None])

    pad = ROUTER_COLS - N_GROUPS - N_EXPERTS
    w_router = jnp.concatenate(
        [w_router_group[0],
         jnp.transpose(w_router_expert[0], (1, 0, 2)).reshape(D_MODEL, N_EXPERTS),
         jnp.zeros((D_MODEL, pad), jnp.float32)], axis=1)
    b_router = jnp.concatenate(
        [b_router_group[0], b_router_expert[0].reshape(-1), jnp.zeros((pad,), jnp.float32)])[None, :]
    x1p, logits_t = _outproj_call(ya, yg, x2d, w_out[0].astype(jnp.bfloat16), b_out, ln1_g, ln1_b,
                                    w_router, b_router)

    gates_t, pos, blk, pad = _route_call(logits_t)
    xs = _dispatch_call(pos, pad, x1p)
    ys = _expert_call(blk, xs, w_expert_gate[0], w_expert_up[0], w_expert_down[0])
    out = _combine_call(pos, ys, x1p, gates_t.T, ln2_g, ln2_b)
    return out.reshape(1, SEQ, D_MODEL)
```

```python
import jax
import jax.numpy as jnp
from jax import lax
from jax.experimental import pallas as pl
from jax.experimental.pallas import tpu as pltpu

D_MODEL = 4096
SEQ = 8192
D_CONV = D_MODEL // 2
D_GMLP = D_MODEL // 2
CONV_WIDTH = 31
GROUP_DIM = 128
CHUNK = 128
N_GROUPS = 8
EXPERTS_PER_GROUP = 8
N_EXPERTS = N_GROUPS * EXPERTS_PER_GROUP
TOP_K = 2
D_EXPERT = D_MODEL // 8
ROW_BLOCK = 128
LN_EPS = 1e-5
ALPHA = 2.0 ** 0.25

N_ROWS = SEQ * TOP_K
N_BLOCKS = N_ROWS // ROW_BLOCK + N_EXPERTS
ROUTER_COLS = 128
SLAB = D_MODEL // 2 // 128

VMEM_LIMIT_BYTES = 60 * 1024 * 1024

MIX_BM = 1024
MIX_SUB = 256
MIX_BN = 256
HALO = 32
CONV_ROWS = 128
SHIFT_ROWS = CONV_ROWS + HALO - 8

OUT_BM = 256
OUT_SUB = 128
ROUTE_TC = 512
BLK_LANES = 256
BLK_EXPERT, BLK_FIRST, BLK_SLOT, BLK_NEXT, BLK_USED = range(5)
COMB_BM = 256
DISP_BM = 512


def _group_norm(v, g, b):
    outs = []
    for s in range(v.shape[1] // GROUP_DIM):
        blk = v[:, s * GROUP_DIM:(s + 1) * GROUP_DIM]
        mu = jnp.mean(blk, axis=-1, keepdims=True)
        d = blk - mu
        var = jnp.mean(d * d, axis=-1, keepdims=True)
        outs.append(d * lax.rsqrt(var + LN_EPS))
    return jnp.concatenate(outs, axis=-1) * g + b


def _mixer_kernel(x_ref, wa_ref, wg_ref, wu_ref, wv_ref, ba_ref, bg_ref, bu_ref, bv_ref,
                  cw_ref, cb_ref, cng_ref, cnb_ref, gng_ref, gnb_ref, wsp_ref, bsp_ref,
                  ya_ref, yg_ref, xs_ref, r_ref, sh_ref):
    i = pl.program_id(1)
    f32 = jnp.float32
    bf16 = jnp.bfloat16
    w_refs = (wa_ref, wg_ref, wu_ref, wv_ref)
    off = HALO - (CONV_WIDTH - 1)

    @pl.when(i == 0)
    def _():
        xs_ref[0:HALO, :] = jnp.zeros((HALO, MIX_BN), f32)

    row = lax.broadcasted_iota(jnp.int32, (CHUNK, CHUNK), 0)
    col = lax.broadcasted_iota(jnp.int32, (CHUNK, CHUNK), 1)
    w_sp = [jnp.where(row >= col, wsp_ref[h], 0.0).astype(bf16) for h in range(MIX_BN // GROUP_DIM)]

    def project(k):
        xk = x_ref[k * MIX_SUB:(k + 1) * MIX_SUB, :].astype(bf16)
        for q in range(4):
            r_ref[k % 2, q] = jnp.dot(xk, w_refs[q][...], preferred_element_type=f32)

    def finish_conv(k, c):
        t0 = k * MIX_SUB + c * CONV_ROWS
        rows = slice(c * CONV_ROWS, (c + 1) * CONV_ROWS)
        a = r_ref[k % 2, 0, rows, :] + ba_ref[...]
        gate = r_ref[k % 2, 1, rows, :] + bg_ref[...]
        xs_ref[HALO + t0:HALO + t0 + CONV_ROWS, :] = a * jax.nn.sigmoid(gate)
        sh = sh_ref.at[c % 2]
        for m in range(1, 8):
            sh[m - 1] = xs_ref[t0 + m:t0 + m + SHIFT_ROWS, :]
        acc = jnp.zeros((CONV_ROWS, MIX_BN), f32) + cb_ref[...]
        for k2 in range(CONV_WIDTH):
            m, j8 = (off + k2) % 8, 8 * ((off + k2) // 8)
            if m == 0:
                src = xs_ref[t0 + j8:t0 + j8 + CONV_ROWS, :]
            else:
                src = sh[m - 1, j8:j8 + CONV_ROWS, :]
            acc = acc + cw_ref[k2:k2 + 1, :] * src
        y = jax.nn.silu(_group_norm(acc, cng_ref[...], cnb_ref[...]))
        ya_ref[t0:t0 + CONV_ROWS, :] = y.astype(ya_ref.dtype)

    def finish_gmlp(k):
        base = k * MIX_SUB
        n_chunks = MIX_SUB // CHUNK
        us, vs = [], []
        for c in range(n_chunks):
            rows = slice(c * CHUNK, (c + 1) * CHUNK)
            us.append(jax.nn.gelu(r_ref[k % 2, 2, rows, :] + bu_ref[...]))
            v = jax.nn.gelu(r_ref[k % 2, 3, rows, :] + bv_ref[...])
            vs.append(_group_norm(v, gng_ref[...], gnb_ref[...]).astype(bf16))
        for h in range(MIX_BN // GROUP_DIM):
            lanes = slice(h * GROUP_DIM, (h + 1) * GROUP_DIM)
            v_h = jnp.concatenate([v[:, lanes] for v in vs], axis=1)
            s_h = jnp.dot(w_sp[h], v_h, preferred_element_type=f32) + bsp_ref[h]
            for c in range(n_chunks):
                g = us[c][:, lanes] * s_h[:, c * GROUP_DIM:(c + 1) * GROUP_DIM]
                yg_ref[base + c * CHUNK:base + (c + 1) * CHUNK, lanes] = g.astype(yg_ref.dtype)

    def finish(k):
        for c in range(MIX_SUB // CONV_ROWS):
            finish_conv(k, c)
        finish_gmlp(k)

    n_sub = MIX_BM // MIX_SUB
    for k in range(n_sub + 1):
        if k < n_sub:
            project(k)
        if k > 0:
            finish(k - 1)
    xs_ref[0:HALO, :] = xs_ref[MIX_BM:MIX_BM + HALO, :]


def _mixer_call(x2d, w_in_b, b_in, conv_w, conv_b, cng, cnb, gng, gnb, w_spatial, b_spatial_col):
    nq = D_CONV // MIX_BN
    grid = (nq, SEQ // MIX_BM)

    def wspec(q):
        return pl.BlockSpec((D_MODEL, MIX_BN), lambda j, i, q=q: (0, q * nq + j))

    def bspec(q):
        return pl.BlockSpec((1, MIX_BN), lambda j, i, q=q: (0, q * nq + j))

    vec = pl.BlockSpec((1, MIX_BN), lambda j, i: (0, j))
    heads_per_step = MIX_BN // GROUP_DIM
    in_specs = [
        pl.BlockSpec((MIX_BM, D_MODEL), lambda j, i: (i, 0)),
        wspec(0), wspec(1), wspec(2), wspec(3),
        bspec(0), bspec(1), bspec(2), bspec(3),
        pl.BlockSpec((CONV_WIDTH, MIX_BN), lambda j, i: (0, j)),
        vec, vec, vec, vec, vec,
        pl.BlockSpec((heads_per_step, CHUNK, CHUNK), lambda j, i: (j, 0, 0)),
        pl.BlockSpec((heads_per_step, CHUNK, 1), lambda j, i: (j, 0, 0)),
    ]
    out_spec = pl.BlockSpec((MIX_BM, MIX_BN), lambda j, i: (i, j))
    return pl.pallas_call(
        _mixer_kernel,
        grid=grid,
        in_specs=in_specs,
        out_specs=[out_spec, out_spec],
        out_shape=[jax.ShapeDtypeStruct((SEQ, D_CONV), jnp.bfloat16),
                   jax.ShapeDtypeStruct((SEQ, D_GMLP), jnp.bfloat16)],
        scratch_shapes=[pltpu.VMEM((HALO + MIX_BM, MIX_BN), jnp.float32),
                        pltpu.VMEM((2, 4, MIX_SUB, MIX_BN), jnp.float32),
                        pltpu.VMEM((2, 7, SHIFT_ROWS, MIX_BN), jnp.float32)],
        compiler_params=pltpu.CompilerParams(
            dimension_semantics=("arbitrary", "arbitrary"),
            vmem_limit_bytes=VMEM_LIMIT_BYTES),
        name="mixer",
    )(x2d, w_in_b, w_in_b, w_in_b, w_in_b, b_in, b_in, b_in, b_in,
      conv_w, conv_b, cng, cnb, gng, gnb, w_spatial, b_spatial_col)


def _store_packed_rows(dst_ref, x):
    _store_packed_rounded_rows(dst_ref, x.astype(jnp.bfloat16).astype(jnp.float32))


def _store_packed_rounded_rows(dst_ref, x):
    half = D_MODEL // 2
    bits = lax.bitcast_convert_type(x, jnp.uint32)
    words = bits[:, :half] | (bits[:, half:] >> 16)
    for j in range(SLAB):
        dst_ref[pl.ds(j, x.shape[0], stride=SLAB), :] = words[:, j * 128:(j + 1) * 128]


def _load_packed_rows(src_ref, rows):
    f32, u32 = jnp.float32, jnp.uint32
    his, los = [], []
    for j in range(SLAB):
        w = src_ref[pl.ds(j, rows, stride=SLAB), :]
        his.append(lax.bitcast_convert_type(w & u32(0xFFFF0000), f32))
        los.append(lax.bitcast_convert_type(w << 16, f32))
    return jnp.concatenate(his + los, axis=1)


def _outproj_kernel(ya_ref, yg_ref, x_ref, w_ref, bo_ref, g_ref, b_ref, wr_ref, br_ref,
                    x1p_ref, logit_ref, z_ref):
    f32 = jnp.float32
    wr = wr_ref[...]
    wr_hi = wr.astype(jnp.bfloat16)
    wr_lo = (wr - wr_hi.astype(f32)).astype(jnp.bfloat16)
    wr_split = jnp.concatenate([wr_hi, wr_lo], axis=1)
    n_sub = OUT_BM // OUT_SUB

    def project(k):
        rows = slice(k * OUT_SUB, (k + 1) * OUT_SUB)
        z = jnp.dot(ya_ref[rows, :], w_ref[0:D_CONV, :], preferred_element_type=f32)
        z_ref[rows, :] = z + jnp.dot(yg_ref[rows, :], w_ref[D_CONV:, :], preferred_element_type=f32)

    def finish(k):
        rows = slice(k * OUT_SUB, (k + 1) * OUT_SUB)
        z = z_ref[rows, :] + bo_ref[...] + ALPHA * x_ref[rows, :]
        mu = jnp.mean(z, axis=-1, keepdims=True)
        d = z - mu
        var = jnp.mean(d * d, axis=-1, keepdims=True)
        x1 = d * lax.rsqrt(var + LN_EPS) * g_ref[...] + b_ref[...]
        x_hi = x1.astype(jnp.bfloat16)
        x1_rounded = x_hi.astype(f32)
        _store_packed_rounded_rows(x1p_ref.at[pl.ds(k * OUT_SUB * SLAB, OUT_SUB * SLAB)], x1_rounded)
        x_lo = (x1 - x1_rounded).astype(jnp.bfloat16)
        parts = jnp.dot(jnp.concatenate([x_hi, x_lo], axis=0), wr_split, preferred_element_type=f32)
        logits = ((parts[:OUT_SUB, :ROUTER_COLS] + parts[:OUT_SUB, ROUTER_COLS:])
                  + (parts[OUT_SUB:, :ROUTER_COLS] + parts[OUT_SUB:, ROUTER_COLS:]))
        logit_ref[:, rows] = (logits + br_ref[...]).T

    project(0)
    for k in range(n_sub):
        if k + 1 < n_sub:
            project(k + 1)
        finish(k)


def _outproj_call(ya, yg, x2d, w_out_b, b_out, ln_g, ln_b, w_router, b_router):
    grid = (SEQ // OUT_BM,)
    row_half = pl.BlockSpec((OUT_BM, D_CONV), lambda i: (i, 0))
    row_full = pl.BlockSpec((OUT_BM, D_MODEL), lambda i: (i, 0))
    vec = pl.BlockSpec((1, D_MODEL), lambda i: (0, 0))
    in_specs = [
        row_half, row_half, row_full,
        pl.BlockSpec((D_MODEL, D_MODEL), lambda i: (0, 0), pipeline_mode=pl.Buffered(1)),
        vec, vec, vec,
        pl.BlockSpec((D_MODEL, ROUTER_COLS), lambda i: (0, 0), pipeline_mode=pl.Buffered(1)),
        pl.BlockSpec((1, ROUTER_COLS), lambda i: (0, 0)),
    ]
    return pl.pallas_call(
        _outproj_kernel,
        grid=grid,
        in_specs=in_specs,
        out_specs=[pl.BlockSpec((OUT_BM * SLAB, 128), lambda i: (i, 0)),
                   pl.BlockSpec((ROUTER_COLS, OUT_BM), lambda i: (0, i))],
        out_shape=[jax.ShapeDtypeStruct((SEQ * SLAB, 128), jnp.uint32),
                   jax.ShapeDtypeStruct((ROUTER_COLS, SEQ), jnp.float32)],
        scratch_shapes=[pltpu.VMEM((OUT_BM, D_MODEL), jnp.float32)],
        compiler_params=pltpu.CompilerParams(
            dimension_semantics=("arbitrary",),
            vmem_limit_bytes=VMEM_LIMIT_BYTES),
        name="outproj",
    )(ya, yg, x2d, w_out_b, b_out, ln_g, ln_b, w_router, b_router)


def _dispatch_kernel(pos_ref, pad_ref, x1p_hbm, xs_hbm, buf, zbuf, lsem, ssem, zsem, tsem):
    i = pl.program_id(0)
    n_steps = pl.num_programs(0)
    rows = DISP_BM * SLAB
    n_slots = 3

    def load(tile, slot):
        src = pl.multiple_of(tile * rows, rows)
        return pltpu.make_async_copy(x1p_hbm.at[pl.ds(src, rows)], buf.at[slot], lsem.at[slot])

    def scatter_wait(slot):
        for _ in range(TOP_K):
            pltpu.make_async_copy(buf.at[slot], xs_hbm.at[pl.ds(0, rows)], ssem.at[slot]).wait()

    def pad_copy(r, n_rows):
        dst = pl.multiple_of(r * SLAB, SLAB)
        return pltpu.make_async_copy(zbuf.at[pl.ds(0, n_rows * SLAB)],
                                     xs_hbm.at[pl.ds(dst, n_rows * SLAB)], zsem)

    def tail_copy(blk):
        dst = pl.multiple_of(blk * (ROW_BLOCK * SLAB), ROW_BLOCK * SLAB)
        return pltpu.make_async_copy(zbuf, xs_hbm.at[pl.ds(dst, ROW_BLOCK * SLAB)], tsem)

    def for_each_tail_block(fn):
        used_blocks = lax.shift_right_logical(pad_ref[1, N_EXPERTS - 1], ROW_BLOCK.bit_length() - 1)

        def per_block(blk, carry):
            fn(blk)
            return carry
        lax.fori_loop(used_blocks, N_BLOCKS, per_block, 0)

    def for_each_pad_piece(fn):
        def per_expert(e, carry):
            lo = pad_ref[0, e]
            n_pad = pad_ref[1, e] - lo
            for bit in range(ROW_BLOCK.bit_length() - 1):
                size = 1 << bit

                @pl.when((n_pad & size) != 0)
                def _():
                    fn(lo + (n_pad & (size - 1)), size)
            return carry
        lax.fori_loop(0, N_EXPERTS, per_expert, 0)

    @pl.when(i == 0)
    def _():
        load(0, 0).start()
        zbuf[...] = jnp.zeros(zbuf.shape, zbuf.dtype)
        for_each_pad_piece(lambda r, n_rows: pad_copy(r, n_rows).start())
        for_each_tail_block(lambda blk: tail_copy(blk).start())

    slot = lax.rem(i, n_slots)

    @pl.when(i + 1 < n_steps)
    def _():
        load(i + 1, lax.rem(i + 1, n_slots)).start()

    load(i, slot).wait()

    def issue(r, carry):
        src = pl.multiple_of(r * SLAB, SLAB)
        for k in range(TOP_K):
            dst = pl.multiple_of(pos_ref[k, i * DISP_BM + r] * SLAB, SLAB)
            pltpu.make_async_copy(buf.at[slot, pl.ds(src, SLAB)], xs_hbm.at[pl.ds(dst, SLAB)],
                                  ssem.at[slot]).start(priority=k % 2)
        return carry
    lax.fori_loop(0, DISP_BM, issue, 0, unroll=4)

    @pl.when(i >= 1)
    def _():
        scatter_wait(lax.rem(i + n_slots - 1, n_slots))

    @pl.when(i == n_steps - 1)
    def _():
        scatter_wait(slot)
        for_each_pad_piece(lambda r, n_rows: pad_copy(r, n_rows).wait())
        for_each_tail_block(lambda blk: tail_copy(blk).wait())


def _dispatch_call(pos, pad, x1p):
    any_spec = pl.BlockSpec(memory_space=pl.ANY)
    grid_spec = pltpu.PrefetchScalarGridSpec(
        num_scalar_prefetch=2,
        grid=(SEQ // DISP_BM,),
        in_specs=[any_spec],
        out_specs=any_spec,
        scratch_shapes=[pltpu.VMEM((3, DISP_BM * SLAB, 128), jnp.uint32),
                        pltpu.VMEM((ROW_BLOCK * SLAB, 128), jnp.uint32),
                        pltpu.SemaphoreType.DMA((3,)),
                        pltpu.SemaphoreType.DMA((3,)),
                        pltpu.SemaphoreType.DMA(()),
                        pltpu.SemaphoreType.DMA(())],
    )
    return pl.pallas_call(
        _dispatch_kernel,
        grid_spec=grid_spec,
        out_shape=jax.ShapeDtypeStruct((N_BLOCKS * ROW_BLOCK * SLAB, 128), jnp.uint32),
        compiler_params=pltpu.CompilerParams(
            dimension_semantics=("arbitrary",),
            vmem_limit_bytes=VMEM_LIMIT_BYTES),
        name="dispatch",
    )(pos, pad, x1p)


def _expert_kernel(blk_ref, x_ref, wg_hbm, wu_hbm, wd_hbm, out_ref, wgb, wub, wdb, wsem):
    b = pl.program_id(0)
    n_used = blk_ref[BLK_USED, 0]

    def weight_copies(e, s):
        return (pltpu.make_async_copy(wg_hbm.at[e], wgb.at[s], wsem.at[s, 0]),
                pltpu.make_async_copy(wu_hbm.at[e], wub.at[s], wsem.at[s, 1]),
                pltpu.make_async_copy(wd_hbm.at[e], wdb.at[s], wsem.at[s, 2]))

    @pl.when(b == 0)
    def _():
        for c in weight_copies(blk_ref[BLK_EXPERT, 0], 0):
            c.start(priority=1)

    @pl.when(b < n_used)
    def _():
        ws = blk_ref[BLK_SLOT, b]

        @pl.when(blk_ref[BLK_FIRST, b] == 1)
        def _():
            @pl.when(blk_ref[BLK_NEXT, b] >= 0)
            def _():
                for c in weight_copies(blk_ref[BLK_NEXT, b], 1 - ws):
                    c.start(priority=1)
            for c in weight_copies(blk_ref[BLK_EXPERT, b], ws):
                c.wait()

        xb = _load_packed_rows(x_ref, ROW_BLOCK).astype(jnp.bfloat16)
        g = jnp.dot(xb, wgb[ws], preferred_element_type=jnp.float32)
        u = jnp.dot(xb, wub[ws], preferred_element_type=jnp.float32)
        h = jax.nn.silu(g) * u
        _store_packed_rows(out_ref, jnp.dot(h, wdb[ws], preferred_element_type=jnp.float32))

    @pl.when(b >= n_used)
    def _():
        out_ref[...] = jnp.zeros(out_ref.shape, out_ref.dtype)


def _expert_call(blk, xs, w_gate, w_up, w_down):
    any_spec = pl.BlockSpec(memory_space=pl.ANY)
    block = (ROW_BLOCK * SLAB, 128)
    grid_spec = pltpu.PrefetchScalarGridSpec(
        num_scalar_prefetch=1,
        grid=(N_BLOCKS,),
        in_specs=[pl.BlockSpec(block, lambda b, blk: (jnp.minimum(b, blk[BLK_USED, 0] - 1), 0)),
                  any_spec, any_spec, any_spec],
        out_specs=pl.BlockSpec(block, lambda b, blk: (b, 0)),
        scratch_shapes=[pltpu.VMEM((2, D_MODEL, D_EXPERT), jnp.float32),
                        pltpu.VMEM((2, D_MODEL, D_EXPERT), jnp.float32),
                        pltpu.VMEM((2, D_EXPERT, D_MODEL), jnp.float32),
                        pltpu.SemaphoreType.DMA((2, 3))],
    )
    return pl.pallas_call(
        _expert_kernel,
        grid_spec=grid_spec,
        out_shape=jax.ShapeDtypeStruct((N_BLOCKS * ROW_BLOCK * SLAB, 128), jnp.uint32),
        compiler_params=pltpu.CompilerParams(
            dimension_semantics=("arbitrary",),
            vmem_limit_bytes=VMEM_LIMIT_BYTES),
        name="experts",
    )(blk, xs, w_gate, w_up, w_down)


def _combine_kernel(pos_ref, ys_hbm, x1p_ref, gate_ref, g_ref, b_ref, out_ref, ybuf, sem):
    i = pl.program_id(0)
    n_steps = pl.num_programs(0)
    rows = TOP_K * COMB_BM

    def gather_start(tile, s):
        def issue(r, carry):
            for k in range(TOP_K):
                src = pl.multiple_of(pos_ref[k, tile * COMB_BM + r] * SLAB, SLAB)
                dst = pl.multiple_of((k * COMB_BM + r) * SLAB, SLAB)
                pltpu.make_async_copy(ys_hbm.at[pl.ds(src, SLAB)], ybuf.at[s, pl.ds(dst, SLAB)],
                                      sem.at[s]).start(priority=k % 2)
            return carry
        lax.fori_loop(0, COMB_BM, issue, 0, unroll=4)

    @pl.when(i == 0)
    def _():
        gather_start(0, 0)

    s = lax.rem(i, 2)

    @pl.when(i + 1 < n_steps)
    def _():
        gather_start(i + 1, 1 - s)

    pltpu.make_async_copy(ys_hbm.at[pl.ds(0, rows * SLAB)], ybuf.at[s], sem.at[s]).wait()
    gate = gate_ref[...]
    y0 = _load_packed_rows(ybuf.at[s, pl.ds(0, COMB_BM * SLAB)], COMB_BM)
    y1 = _load_packed_rows(ybuf.at[s, pl.ds(COMB_BM * SLAB, COMB_BM * SLAB)], COMB_BM)
    ffn = y0 * gate[:, 0:1] + y1 * gate[:, 1:2]
    z = ALPHA * _load_packed_rows(x1p_ref, COMB_BM) + ffn
    mu = jnp.mean(z, axis=-1, keepdims=True)
    d = z - mu
    var = jnp.mean(d * d, axis=-1, keepdims=True)
    out_ref[...] = d * lax.rsqrt(var + LN_EPS) * g_ref[...] + b_ref[...]


def _combine_call(pos, ys, x1p, gates, ln_g, ln_b):
    row_full = pl.BlockSpec((COMB_BM, D_MODEL), lambda i, pos: (i, 0))
    vec = pl.BlockSpec((1, D_MODEL), lambda i, pos: (0, 0))
    grid_spec = pltpu.PrefetchScalarGridSpec(
        num_scalar_prefetch=1,
        grid=(SEQ // COMB_BM,),
        in_specs=[
            pl.BlockSpec(memory_space=pl.ANY),
            pl.BlockSpec((COMB_BM * SLAB, 128), lambda i, pos: (i, 0)),
            pl.BlockSpec((COMB_BM, TOP_K), lambda i, pos: (i, 0)),
            vec, vec,
        ],
        out_specs=row_full,
        scratch_shapes=[pltpu.VMEM((2, TOP_K * COMB_BM * SLAB, 128), jnp.uint32),
                        pltpu.SemaphoreType.DMA((2,))],
    )
    return pl.pallas_call(
        _combine_kernel,
        grid_spec=grid_spec,
        out_shape=jax.ShapeDtypeStruct((SEQ, D_MODEL), jnp.float32),
        compiler_params=pltpu.CompilerParams(
            dimension_semantics=("arbitrary",),
            vmem_limit_bytes=VMEM_LIMIT_BYTES),
        name="combine",
    )(pos, ys, x1p, gates, ln_g, ln_b)


def _route_kernel(lt_ref, gates_ref, pos_ref, blk_ref, pad_ref, sel_ref, rank_ref):
    f32, i32 = jnp.float32, jnp.int32
    neg = jnp.float32(-jnp.inf)
    n_chunks = SEQ // ROUTE_TC
    erow = lax.broadcasted_iota(i32, (N_EXPERTS, ROUTE_TC), 0)
    grow = lax.broadcasted_iota(i32, (N_GROUPS, ROUTE_TC), 0)
    src_tok = lax.broadcasted_iota(i32, (ROUTE_TC, ROUTE_TC), 0)
    dst_tok = lax.broadcasted_iota(i32, (ROUTE_TC, ROUTE_TC), 1)
    before = (src_tok < dst_tok).astype(jnp.bfloat16)

    def first_index(mask, idx, size):
        return jnp.min(jnp.where(mask, idx, size), axis=0, keepdims=True)

    counts = jnp.zeros((N_EXPERTS, 1), f32)
    for c in range(n_chunks):
        lanes = slice(c * ROUTE_TC, (c + 1) * ROUTE_TC)
        gl = lt_ref[0:N_GROUPS, lanes]
        g_max = jnp.max(gl, axis=0, keepdims=True)
        g_sel = first_index(gl == g_max, grow, N_GROUPS)
        g_weight = 1.0 / jnp.sum(jnp.exp(gl - g_max), axis=0, keepdims=True)
        el = lt_ref[N_GROUPS:N_GROUPS + N_EXPERTS, lanes]
        in_group = (erow >= g_sel * EXPERTS_PER_GROUP) & (erow < (g_sel + 1) * EXPERTS_PER_GROUP)
        el = jnp.where(in_group, el, neg)
        t1 = jnp.max(el, axis=0, keepdims=True)
        e1 = first_index(el == t1, erow, N_EXPERTS)
        el2 = jnp.where(erow == e1, neg, el)
        t2 = jnp.max(el2, axis=0, keepdims=True)
        e2 = first_index(el2 == t2, erow, N_EXPERTS)
        r = jnp.exp(t2 - t1)
        gates_ref[0:1, lanes] = g_weight * (1.0 / (1.0 + r))
        gates_ref[1:2, lanes] = g_weight * (r / (1.0 + r))
        oh1 = erow == e1
        oh2 = erow == e2
        hits = oh1.astype(jnp.bfloat16) + oh2.astype(jnp.bfloat16)
        ahead = jnp.dot(hits, before, preferred_element_type=f32) + counts
        rank_ref[0:1, lanes] = jnp.sum(jnp.where(oh1, ahead, 0.0), axis=0, keepdims=True)
        rank_ref[1:2, lanes] = jnp.sum(jnp.where(oh2, ahead, 0.0), axis=0, keepdims=True)
        sel_ref[0:1, lanes] = e1
        sel_ref[1:2, lanes] = e2
        counts = counts + jnp.sum(hits.astype(f32), axis=1, keepdims=True)

    n_blk = lax.shift_right_logical(counts.astype(i32) + (ROW_BLOCK - 1), ROW_BLOCK.bit_length() - 1)
    e_src = lax.broadcasted_iota(i32, (N_EXPERTS, N_EXPERTS), 1)
    e_dst = lax.broadcasted_iota(i32, (N_EXPERTS, N_EXPERTS), 0)
    incl = (e_src <= e_dst).astype(jnp.bfloat16)
    n_blk_b = jnp.broadcast_to(n_blk.astype(f32), (N_EXPERTS, 128)).astype(jnp.bfloat16)
    blk_end = jnp.dot(incl, n_blk_b, preferred_element_type=f32)[:, 0:1].astype(i32)
    blk_start = blk_end - n_blk
    row_start = (blk_start * ROW_BLOCK).astype(f32)
    e_sub128 = lax.broadcasted_iota(i32, (N_EXPERTS, 128), 0)
    e_lane128 = lax.broadcasted_iota(i32, (N_EXPERTS, 128), 1)
    diag = e_sub128 == e_lane128
    pad_lo = jnp.sum(jnp.where(diag, blk_start * ROW_BLOCK + counts.astype(i32), 0), axis=0, keepdims=True)
    pad_hi = jnp.sum(jnp.where(diag, blk_end * ROW_BLOCK, 0), axis=0, keepdims=True)
    pad_ref[...] = jnp.concatenate([pad_lo, pad_hi, jnp.zeros((6, 128), i32)], axis=0)

    for c in range(n_chunks):
        lanes = slice(c * ROUTE_TC, (c + 1) * ROUTE_TC)
        for k in range(TOP_K):
            oh = erow == sel_ref[k:k + 1, lanes]
            start = jnp.sum(jnp.where(oh, row_start, 0.0), axis=0, keepdims=True)
            pos_ref[k:k + 1, lanes] = (start + rank_ref[k:k + 1, lanes]).astype(i32)

    b_lane = lax.broadcasted_iota(i32, (N_EXPERTS, BLK_LANES), 1)
    e_sub = lax.broadcasted_iota(i32, (N_EXPERTS, BLK_LANES), 0)
    used = n_blk > 0
    block_expert = jnp.minimum(jnp.sum((blk_end <= b_lane).astype(i32), axis=0, keepdims=True), N_EXPERTS - 1)
    first = jnp.sum((used & (blk_start == b_lane)).astype(i32), axis=0, keepdims=True)
    run_index = jnp.sum((used & (e_sub < block_expert)).astype(i32), axis=0, keepdims=True)
    nxt = jnp.min(jnp.where(used & (e_sub > block_expert), e_sub, N_EXPERTS), axis=0, keepdims=True)
    nxt = jnp.where(nxt >= N_EXPERTS, -1, nxt)
    n_used = jnp.broadcast_to(blk_end[N_EXPERTS - 1:N_EXPERTS, :], (1, BLK_LANES))
    blk_ref[...] = jnp.concatenate(
        [block_expert, first, run_index & 1, nxt, n_used, jnp.zeros((3, BLK_LANES), i32)], axis=0)


def _route_call(logits_t):
    return pl.pallas_call(
        _route_kernel,
        out_shape=[jax.ShapeDtypeStruct((TOP_K, SEQ), jnp.float32),
                   jax.ShapeDtypeStruct((TOP_K, SEQ), jnp.int32),
                   jax.ShapeDtypeStruct((8, BLK_LANES), jnp.int32),
                   jax.ShapeDtypeStruct((8, 128), jnp.int32)],
        scratch_shapes=[pltpu.VMEM((TOP_K, SEQ), jnp.int32),
                        pltpu.VMEM((TOP_K, SEQ), jnp.float32)],
        compiler_params=pltpu.CompilerParams(vmem_limit_bytes=VMEM_LIMIT_BYTES),
        name="route",
    )(logits_t)


def kernel(x, w_in, b_in, conv_w, conv_b, conv_norm_g, conv_norm_b, gmlp_norm_g, gmlp_norm_b,
           w_spatial, b_spatial, w_out, b_out, ln1_g, ln1_b, w_router_group, b_router_group,
           w_router_expert, b_router_expert, w_expert_gate, w_expert_up, w_expert_down,
           ln2_g, ln2_b):
    assert x.shape == (1, SEQ, D_MODEL) and w_in.shape[0] == 1
    x2d = x.reshape(SEQ, D_MODEL)
    ya, yg = _mixer_call(
        x2d, w_in[0].astype(jnp.bfloat16), b_in, conv_w[0], conv_b, conv_norm_g, conv_norm_b,
        gmlp_norm_g, gmlp_norm_b, w_spatial[0], b_spatial[0][:, :, None])

    unused_cols = ROUTER_COLS - N_GROUPS - N_EXPERTS
    w_router = jnp.concatenate(
        [w_router_group[0],
         jnp.transpose(w_router_expert[0], (1, 0, 2)).reshape(D_MODEL, N_EXPERTS),
         jnp.zeros((D_MODEL, unused_cols), jnp.float32)], axis=1)
    b_router = jnp.concatenate(
        [b_router_group[0], b_router_expert[0].reshape(-1), jnp.zeros((unused_cols,), jnp.float32)])[None, :]
    x1p, logits_t = _outproj_call(ya, yg, x2d, w_out[0].astype(jnp.bfloat16), b_out, ln1_g, ln1_b,
                                    w_router, b_router)

    gates_t, pos, blk, pad = _route_call(logits_t)
    xs = _dispatch_call(pos, pad, x1p)
    ys = _expert_call(blk, xs, w_expert_gate[0], w_expert_up[0], w_expert_down[0])
    out = _combine_call(pos, ys, x1p, gates_t.T, ln2_g, ln2_b)
    return out.reshape(1, SEQ, D_MODEL)
```

```python
import jax
import jax.numpy as jnp
from jax import lax
from jax.experimental import pallas as pl
from jax.experimental.pallas import tpu as pltpu

D_MODEL = 4096
SEQ = 8192
D_CONV = D_MODEL // 2
D_GMLP = D_MODEL // 2
CONV_WIDTH = 31
GROUP_DIM = 128
CHUNK = 128
N_GROUPS = 8
EXPERTS_PER_GROUP = 8
N_EXPERTS = N_GROUPS * EXPERTS_PER_GROUP
TOP_K = 2
D_EXPERT = D_MODEL // 8
ROW_BLOCK = 128
LN_EPS = 1e-5
ALPHA = 2.0 ** 0.25

N_ROWS = SEQ * TOP_K
N_BLOCKS = N_ROWS // ROW_BLOCK + N_EXPERTS
ROUTER_COLS = 128
SLAB = D_MODEL // 2 // 128

VMEM_LIMIT_BYTES = 60 * 1024 * 1024

MIX_BM = 1024
MIX_SUB = 256
MIX_BN = 256
HALO = 32
CONV_ROWS = 128
SHIFT_ROWS = CONV_ROWS + HALO - 8

OUT_BM = 256
OUT_SUB = 128
ROUTE_TC = 512
BLK_LANES = 256
BLK_EXPERT, BLK_FIRST, BLK_SLOT, BLK_NEXT, BLK_USED = range(5)
COMB_BM = 256
DISP_BM = 512


def _group_norm(v, g, b):
    outs = []
    for s in range(v.shape[1] // GROUP_DIM):
        blk = v[:, s * GROUP_DIM:(s + 1) * GROUP_DIM]
        mu = jnp.mean(blk, axis=-1, keepdims=True)
        d = blk - mu
        var = jnp.mean(d * d, axis=-1, keepdims=True)
        outs.append(d * lax.rsqrt(var + LN_EPS))
    return jnp.concatenate(outs, axis=-1) * g + b


def _mixer_kernel(x_ref, wa_ref, wg_ref, wu_ref, wv_ref, ba_ref, bg_ref, bu_ref, bv_ref,
                  cw_ref, cb_ref, cng_ref, cnb_ref, gng_ref, gnb_ref, wsp_ref, bsp_ref, wo_ref,
                  ya_ref, yg_ref, wob_ref, xs_ref, r_ref, sh_ref):
    i = pl.program_id(1)
    f32 = jnp.float32
    bf16 = jnp.bfloat16
    w_refs = (wa_ref, wg_ref, wu_ref, wv_ref)
    wob_ref[...] = wo_ref[...].astype(bf16)
    off = HALO - (CONV_WIDTH - 1)

    @pl.when(i == 0)
    def _():
        xs_ref[0:HALO, :] = jnp.zeros((HALO, MIX_BN), f32)

    row = lax.broadcasted_iota(jnp.int32, (CHUNK, CHUNK), 0)
    col = lax.broadcasted_iota(jnp.int32, (CHUNK, CHUNK), 1)
    w_sp = [jnp.where(row >= col, wsp_ref[h], 0.0).astype(bf16) for h in range(MIX_BN // GROUP_DIM)]

    def project(k):
        xk = x_ref[k * MIX_SUB:(k + 1) * MIX_SUB, :].astype(bf16)
        for q in range(4):
            r_ref[k % 2, q] = jnp.dot(xk, w_refs[q][...], preferred_element_type=f32)

    def finish_conv(k, c):
        t0 = k * MIX_SUB + c * CONV_ROWS
        rows = slice(c * CONV_ROWS, (c + 1) * CONV_ROWS)
        a = r_ref[k % 2, 0, rows, :] + ba_ref[...]
        gate = r_ref[k % 2, 1, rows, :] + bg_ref[...]
        xs_ref[HALO + t0:HALO + t0 + CONV_ROWS, :] = a * jax.nn.sigmoid(gate)
        sh = sh_ref.at[c % 2]
        for m in range(1, 8):
            sh[m - 1] = xs_ref[t0 + m:t0 + m + SHIFT_ROWS, :]
        acc = jnp.zeros((CONV_ROWS, MIX_BN), f32) + cb_ref[...]
        for k2 in range(CONV_WIDTH):
            m, j8 = (off + k2) % 8, 8 * ((off + k2) // 8)
            if m == 0:
                src = xs_ref[t0 + j8:t0 + j8 + CONV_ROWS, :]
            else:
                src = sh[m - 1, j8:j8 + CONV_ROWS, :]
            acc = acc + cw_ref[k2:k2 + 1, :] * src
        y = jax.nn.silu(_group_norm(acc, cng_ref[...], cnb_ref[...]))
        ya_ref[t0:t0 + CONV_ROWS, :] = y.astype(ya_ref.dtype)

    def finish_gmlp(k):
        base = k * MIX_SUB
        n_chunks = MIX_SUB // CHUNK
        us, vs = [], []
        for c in range(n_chunks):
            rows = slice(c * CHUNK, (c + 1) * CHUNK)
            us.append(jax.nn.gelu(r_ref[k % 2, 2, rows, :] + bu_ref[...]))
            v = jax.nn.gelu(r_ref[k % 2, 3, rows, :] + bv_ref[...])
            vs.append(_group_norm(v, gng_ref[...], gnb_ref[...]).astype(bf16))
        for h in range(MIX_BN // GROUP_DIM):
            lanes = slice(h * GROUP_DIM, (h + 1) * GROUP_DIM)
            v_h = jnp.concatenate([v[:, lanes] for v in vs], axis=1)
            s_h = jnp.dot(w_sp[h], v_h, preferred_element_type=f32) + bsp_ref[h]
            for c in range(n_chunks):
                g = us[c][:, lanes] * s_h[:, c * GROUP_DIM:(c + 1) * GROUP_DIM]
                yg_ref[base + c * CHUNK:base + (c + 1) * CHUNK, lanes] = g.astype(yg_ref.dtype)

    def finish(k):
        for c in range(MIX_SUB // CONV_ROWS):
            finish_conv(k, c)
        finish_gmlp(k)

    n_sub = MIX_BM // MIX_SUB
    for k in range(n_sub + 1):
        if k < n_sub:
            project(k)
        if k > 0:
            finish(k - 1)
    xs_ref[0:HALO, :] = xs_ref[MIX_BM:MIX_BM + HALO, :]


def _mixer_call(x2d, w_in_b, b_in, conv_w, conv_b, cng, cnb, gng, gnb, w_spatial, b_spatial_col, w_out):
    nq = D_CONV // MIX_BN
    ni = SEQ // MIX_BM
    grid = (nq, ni)
    wo_rows = D_MODEL // (nq * ni)
    wo_spec = pl.BlockSpec((wo_rows, D_MODEL), lambda j, i: (j * ni + i, 0))

    def wspec(q):
        return pl.BlockSpec((D_MODEL, MIX_BN), lambda j, i, q=q: (0, q * nq + j))

    def bspec(q):
        return pl.BlockSpec((1, MIX_BN), lambda j, i, q=q: (0, q * nq + j))

    vec = pl.BlockSpec((1, MIX_BN), lambda j, i: (0, j))
    heads_per_step = MIX_BN // GROUP_DIM
    in_specs = [
        pl.BlockSpec((MIX_BM, D_MODEL), lambda j, i: (i, 0)),
        wspec(0), wspec(1), wspec(2), wspec(3),
        bspec(0), bspec(1), bspec(2), bspec(3),
        pl.BlockSpec((CONV_WIDTH, MIX_BN), lambda j, i: (0, j)),
        vec, vec, vec, vec, vec,
        pl.BlockSpec((heads_per_step, CHUNK, CHUNK), lambda j, i: (j, 0, 0)),
        pl.BlockSpec((heads_per_step, CHUNK, 1), lambda j, i: (j, 0, 0)),
        wo_spec,
    ]
    out_spec = pl.BlockSpec((MIX_BM, MIX_BN), lambda j, i: (i, j))
    return pl.pallas_call(
        _mixer_kernel,
        grid=grid,
        in_specs=in_specs,
        out_specs=[out_spec, out_spec, wo_spec],
        out_shape=[jax.ShapeDtypeStruct((SEQ, D_CONV), jnp.bfloat16),
                   jax.ShapeDtypeStruct((SEQ, D_GMLP), jnp.bfloat16),
                   jax.ShapeDtypeStruct((D_MODEL, D_MODEL), jnp.bfloat16)],
        scratch_shapes=[pltpu.VMEM((HALO + MIX_BM, MIX_BN), jnp.float32),
                        pltpu.VMEM((2, 4, MIX_SUB, MIX_BN), jnp.float32),
                        pltpu.VMEM((2, 7, SHIFT_ROWS, MIX_BN), jnp.float32)],
        compiler_params=pltpu.CompilerParams(
            dimension_semantics=("arbitrary", "arbitrary"),
            vmem_limit_bytes=VMEM_LIMIT_BYTES),
        name="mixer",
    )(x2d, w_in_b, w_in_b, w_in_b, w_in_b, b_in, b_in, b_in, b_in,
      conv_w, conv_b, cng, cnb, gng, gnb, w_spatial, b_spatial_col, w_out)


def _store_packed_rows(dst_ref, x):
    _store_packed_rounded_rows(dst_ref, x.astype(jnp.bfloat16).astype(jnp.float32))


def _store_packed_rounded_rows(dst_ref, x):
    half = D_MODEL // 2
    bits = lax.bitcast_convert_type(x, jnp.uint32)
    words = bits[:, :half] | (bits[:, half:] >> 16)
    for j in range(SLAB):
        dst_ref[pl.ds(j, x.shape[0], stride=SLAB), :] = words[:, j * 128:(j + 1) * 128]


def _load_packed_rows(src_ref, rows):
    f32, u32 = jnp.float32, jnp.uint32
    his, los = [], []
    for j in range(SLAB):
        w = src_ref[pl.ds(j, rows, stride=SLAB), :]
        his.append(lax.bitcast_convert_type(w & u32(0xFFFF0000), f32))
        los.append(lax.bitcast_convert_type(w << 16, f32))
    return jnp.concatenate(his + los, axis=1)


def _outproj_kernel(ya_ref, yg_ref, x_ref, w_ref, bo_ref, g_ref, b_ref, wr_ref, br_ref,
                    x1p_ref, logit_ref, z_ref):
    f32 = jnp.float32
    wr = wr_ref[...]
    wr_hi = wr.astype(jnp.bfloat16)
    wr_lo = (wr - wr_hi.astype(f32)).astype(jnp.bfloat16)
    wr_split = jnp.concatenate([wr_hi, wr_lo], axis=1)
    n_sub = OUT_BM // OUT_SUB

    def project(k):
        rows = slice(k * OUT_SUB, (k + 1) * OUT_SUB)
        z = jnp.dot(ya_ref[rows, :], w_ref[0:D_CONV, :], preferred_element_type=f32)
        z_ref[rows, :] = z + jnp.dot(yg_ref[rows, :], w_ref[D_CONV:, :], preferred_element_type=f32)

    def finish(k):
        rows = slice(k * OUT_SUB, (k + 1) * OUT_SUB)
        z = z_ref[rows, :] + bo_ref[...] + ALPHA * x_ref[rows, :]
        mu = jnp.mean(z, axis=-1, keepdims=True)
        d = z - mu
        var = jnp.mean(d * d, axis=-1, keepdims=True)
        x1 = d * lax.rsqrt(var + LN_EPS) * g_ref[...] + b_ref[...]
        x_hi = x1.astype(jnp.bfloat16)
        x1_rounded = x_hi.astype(f32)
        _store_packed_rounded_rows(x1p_ref.at[pl.ds(k * OUT_SUB * SLAB, OUT_SUB * SLAB)], x1_rounded)
        x_lo = (x1 - x1_rounded).astype(jnp.bfloat16)
        parts = jnp.dot(jnp.concatenate([x_hi, x_lo], axis=0), wr_split, preferred_element_type=f32)
        logits = ((parts[:OUT_SUB, :ROUTER_COLS] + parts[:OUT_SUB, ROUTER_COLS:])
                  + (parts[OUT_SUB:, :ROUTER_COLS] + parts[OUT_SUB:, ROUTER_COLS:]))
        logit_ref[:, rows] = (logits + br_ref[...]).T

    project(0)
    for k in range(n_sub):
        if k + 1 < n_sub:
            project(k + 1)
        finish(k)


def _outproj_call(ya, yg, x2d, w_out_b, b_out, ln_g, ln_b, w_router, b_router):
    grid = (SEQ // OUT_BM,)
    row_half = pl.BlockSpec((OUT_BM, D_CONV), lambda i: (i, 0))
    row_full = pl.BlockSpec((OUT_BM, D_MODEL), lambda i: (i, 0))
    vec = pl.BlockSpec((1, D_MODEL), lambda i: (0, 0))
    in_specs = [
        row_half, row_half, row_full,
        pl.BlockSpec((D_MODEL, D_MODEL), lambda i: (0, 0), pipeline_mode=pl.Buffered(1)),
        vec, vec, vec,
        pl.BlockSpec((D_MODEL, ROUTER_COLS), lambda i: (0, 0), pipeline_mode=pl.Buffered(1)),
        pl.BlockSpec((1, ROUTER_COLS), lambda i: (0, 0)),
    ]
    return pl.pallas_call(
        _outproj_kernel,
        grid=grid,
        in_specs=in_specs,
        out_specs=[pl.BlockSpec((OUT_BM * SLAB, 128), lambda i: (i, 0)),
                   pl.BlockSpec((ROUTER_COLS, OUT_BM), lambda i: (0, i))],
        out_shape=[jax.ShapeDtypeStruct((SEQ * SLAB, 128), jnp.uint32),
                   jax.ShapeDtypeStruct((ROUTER_COLS, SEQ), jnp.float32)],
        scratch_shapes=[pltpu.VMEM((OUT_BM, D_MODEL), jnp.float32)],
        compiler_params=pltpu.CompilerParams(
            dimension_semantics=("arbitrary",),
            vmem_limit_bytes=VMEM_LIMIT_BYTES),
        name="outproj",
    )(ya, yg, x2d, w_out_b, b_out, ln_g, ln_b, w_router, b_router)


def _dispatch_kernel(pos_ref, pad_ref, x1p_hbm, xs_hbm, buf, zbuf, lsem, ssem, zsem, tsem):
    i = pl.program_id(0)
    n_steps = pl.num_programs(0)
    rows = DISP_BM * SLAB
    n_slots = 3

    def load(tile, slot):
        src = pl.multiple_of(tile * rows, rows)
        return pltpu.make_async_copy(x1p_hbm.at[pl.ds(src, rows)], buf.at[slot], lsem.at[slot])

    def scatter_wait(slot):
        for _ in range(TOP_K):
            pltpu.make_async_copy(buf.at[slot], xs_hbm.at[pl.ds(0, rows)], ssem.at[slot]).wait()

    def pad_copy(r, n_rows):
        dst = pl.multiple_of(r * SLAB, SLAB)
        return pltpu.make_async_copy(zbuf.at[pl.ds(0, n_rows * SLAB)],
                                     xs_hbm.at[pl.ds(dst, n_rows * SLAB)], zsem)

    def tail_copy(blk):
        dst = pl.multiple_of(blk * (ROW_BLOCK * SLAB), ROW_BLOCK * SLAB)
        return pltpu.make_async_copy(zbuf, xs_hbm.at[pl.ds(dst, ROW_BLOCK * SLAB)], tsem)

    def for_each_tail_block(fn):
        used_blocks = lax.shift_right_logical(pad_ref[1, N_EXPERTS - 1], ROW_BLOCK.bit_length() - 1)

        def per_block(blk, carry):
            fn(blk)
            return carry
        lax.fori_loop(used_blocks, N_BLOCKS, per_block, 0)

    def for_each_pad_piece(fn):
        def per_expert(e, carry):
            lo = pad_ref[0, e]
            n_pad = pad_ref[1, e] - lo
            for bit in range(ROW_BLOCK.bit_length() - 1):
                size = 1 << bit

                @pl.when((n_pad & size) != 0)
                def _():
                    fn(lo + (n_pad & (size - 1)), size)
            return carry
        lax.fori_loop(0, N_EXPERTS, per_expert, 0)

    @pl.when(i == 0)
    def _():
        load(0, 0).start()
        zbuf[...] = jnp.zeros(zbuf.shape, zbuf.dtype)
        for_each_pad_piece(lambda r, n_rows: pad_copy(r, n_rows).start())
        for_each_tail_block(lambda blk: tail_copy(blk).start())

    slot = lax.rem(i, n_slots)

    @pl.when(i + 1 < n_steps)
    def _():
        load(i + 1, lax.rem(i + 1, n_slots)).start()

    load(i, slot).wait()

    def issue(r, carry):
        src = pl.multiple_of(r * SLAB, SLAB)
        for k in range(TOP_K):
            dst = pl.multiple_of(pos_ref[k, i * DISP_BM + r] * SLAB, SLAB)
            pltpu.make_async_copy(buf.at[slot, pl.ds(src, SLAB)], xs_hbm.at[pl.ds(dst, SLAB)],
                                  ssem.at[slot]).start(priority=k % 2)
        return carry
    lax.fori_loop(0, DISP_BM, issue, 0, unroll=4)

    @pl.when(i >= 1)
    def _():
        scatter_wait(lax.rem(i + n_slots - 1, n_slots))

    @pl.when(i == n_steps - 1)
    def _():
        scatter_wait(slot)
        for_each_pad_piece(lambda r, n_rows: pad_copy(r, n_rows).wait())
        for_each_tail_block(lambda blk: tail_copy(blk).wait())


def _dispatch_call(pos, pad, x1p):
    any_spec = pl.BlockSpec(memory_space=pl.ANY)
    grid_spec = pltpu.PrefetchScalarGridSpec(
        num_scalar_prefetch=2,
        grid=(SEQ // DISP_BM,),
        in_specs=[any_spec],
        out_specs=any_spec,
        scratch_shapes=[pltpu.VMEM((3, DISP_BM * SLAB, 128), jnp.uint32),
                        pltpu.VMEM((ROW_BLOCK * SLAB, 128), jnp.uint32),
                        pltpu.SemaphoreType.DMA((3,)),
                        pltpu.SemaphoreType.DMA((3,)),
                        pltpu.SemaphoreType.DMA(()),
                        pltpu.SemaphoreType.DMA(())],
    )
    return pl.pallas_call(
        _dispatch_kernel,
        grid_spec=grid_spec,
        out_shape=jax.ShapeDtypeStruct((N_BLOCKS * ROW_BLOCK * SLAB, 128), jnp.uint32),
        compiler_params=pltpu.CompilerParams(
            dimension_semantics=("arbitrary",),
            vmem_limit_bytes=VMEM_LIMIT_BYTES),
        name="dispatch",
    )(pos, pad, x1p)


def _expert_kernel(blk_ref, x_ref, wg_hbm, wu_hbm, wd_hbm, out_ref, wgb, wub, wdb, wsem):
    b = pl.program_id(0)
    n_used = blk_ref[BLK_USED, 0]

    def weight_copies(e, s):
        return (pltpu.make_async_copy(wg_hbm.at[e], wgb.at[s], wsem.at[s, 0]),
                pltpu.make_async_copy(wu_hbm.at[e], wub.at[s], wsem.at[s, 1]),
                pltpu.make_async_copy(wd_hbm.at[e], wdb.at[s], wsem.at[s, 2]))

    @pl.when(b == 0)
    def _():
        for c in weight_copies(blk_ref[BLK_EXPERT, 0], 0):
            c.start(priority=1)

    @pl.when(b < n_used)
    def _():
        ws = blk_ref[BLK_SLOT, b]

        @pl.when(blk_ref[BLK_FIRST, b] == 1)
        def _():
            @pl.when(blk_ref[BLK_NEXT, b] >= 0)
            def _():
                for c in weight_copies(blk_ref[BLK_NEXT, b], 1 - ws):
                    c.start(priority=1)
            for c in weight_copies(blk_ref[BLK_EXPERT, b], ws):
                c.wait()

        xb = _load_packed_rows(x_ref, ROW_BLOCK).astype(jnp.bfloat16)
        g = jnp.dot(xb, wgb[ws], preferred_element_type=jnp.float32)
        u = jnp.dot(xb, wub[ws], preferred_element_type=jnp.float32)
        h = jax.nn.silu(g) * u
        _store_packed_rows(out_ref, jnp.dot(h, wdb[ws], preferred_element_type=jnp.float32))

    @pl.when(b >= n_used)
    def _():
        out_ref[...] = jnp.zeros(out_ref.shape, out_ref.dtype)


def _expert_call(blk, xs, w_gate, w_up, w_down):
    any_spec = pl.BlockSpec(memory_space=pl.ANY)
    block = (ROW_BLOCK * SLAB, 128)
    grid_spec = pltpu.PrefetchScalarGridSpec(
        num_scalar_prefetch=1,
        grid=(N_BLOCKS,),
        in_specs=[pl.BlockSpec(block, lambda b, blk: (jnp.minimum(b, blk[BLK_USED, 0] - 1), 0)),
                  any_spec, any_spec, any_spec],
        out_specs=pl.BlockSpec(block, lambda b, blk: (b, 0)),
        scratch_shapes=[pltpu.VMEM((2, D_MODEL, D_EXPERT), jnp.float32),
                        pltpu.VMEM((2, D_MODEL, D_EXPERT), jnp.float32),
                        pltpu.VMEM((2, D_EXPERT, D_MODEL), jnp.float32),
                        pltpu.SemaphoreType.DMA((2, 3))],
    )
    return pl.pallas_call(
        _expert_kernel,
        grid_spec=grid_spec,
        out_shape=jax.ShapeDtypeStruct((N_BLOCKS * ROW_BLOCK * SLAB, 128), jnp.uint32),
        compiler_params=pltpu.CompilerParams(
            dimension_semantics=("arbitrary",),
            vmem_limit_bytes=VMEM_LIMIT_BYTES),
        name="experts",
    )(blk, xs, w_gate, w_up, w_down)


def _combine_kernel(pos_ref, ys_hbm, x1p_ref, gate_ref, g_ref, b_ref, out_ref, ybuf, sem):
    i = pl.program_id(0)
    n_steps = pl.num_programs(0)
    rows = TOP_K * COMB_BM

    def gather_start(tile, s):
        def issue(r, carry):
            for k in range(TOP_K):
                src = pl.multiple_of(pos_ref[k, tile * COMB_BM + r] * SLAB, SLAB)
                dst = pl.multiple_of((k * COMB_BM + r) * SLAB, SLAB)
                pltpu.make_async_copy(ys_hbm.at[pl.ds(src, SLAB)], ybuf.at[s, pl.ds(dst, SLAB)],
                                      sem.at[s]).start(priority=k % 2)
            return carry
        lax.fori_loop(0, COMB_BM, issue, 0, unroll=4)

    @pl.when(i == 0)
    def _():
        gather_start(0, 0)

    s = lax.rem(i, 2)

    @pl.when(i + 1 < n_steps)
    def _():
        gather_start(i + 1, 1 - s)

    pltpu.make_async_copy(ys_hbm.at[pl.ds(0, rows * SLAB)], ybuf.at[s], sem.at[s]).wait()
    gate = gate_ref[...]
    y0 = _load_packed_rows(ybuf.at[s, pl.ds(0, COMB_BM * SLAB)], COMB_BM)
    y1 = _load_packed_rows(ybuf.at[s, pl.ds(COMB_BM * SLAB, COMB_BM * SLAB)], COMB_BM)
    ffn = y0 * gate[:, 0:1] + y1 * gate[:, 1:2]
    z = ALPHA * _load_packed_rows(x1p_ref, COMB_BM) + ffn
    mu = jnp.mean(z, axis=-1, keepdims=True)
    d = z - mu
    var = jnp.mean(d * d, axis=-1, keepdims=True)
    out_ref[...] = d * lax.rsqrt(var + LN_EPS) * g_ref[...] + b_ref[...]


def _combine_call(pos, ys, x1p, gates, ln_g, ln_b):
    row_full = pl.BlockSpec((COMB_BM, D_MODEL), lambda i, pos: (i, 0))
    vec = pl.BlockSpec((1, D_MODEL), lambda i, pos: (0, 0))
    grid_spec = pltpu.PrefetchScalarGridSpec(
        num_scalar_prefetch=1,
        grid=(SEQ // COMB_BM,),
        in_specs=[
            pl.BlockSpec(memory_space=pl.ANY),
            pl.BlockSpec((COMB_BM * SLAB, 128), lambda i, pos: (i, 0)),
            pl.BlockSpec((COMB_BM, TOP_K), lambda i, pos: (i, 0)),
            vec, vec,
        ],
        out_specs=row_full,
        scratch_shapes=[pltpu.VMEM((2, TOP_K * COMB_BM * SLAB, 128), jnp.uint32),
                        pltpu.SemaphoreType.DMA((2,))],
    )
    return pl.pallas_call(
        _combine_kernel,
        grid_spec=grid_spec,
        out_shape=jax.ShapeDtypeStruct((SEQ, D_MODEL), jnp.float32),
        compiler_params=pltpu.CompilerParams(
            dimension_semantics=("arbitrary",),
            vmem_limit_bytes=VMEM_LIMIT_BYTES),
        name="combine",
    )(pos, ys, x1p, gates, ln_g, ln_b)


def _route_kernel(lt_ref, gates_ref, pos_ref, blk_ref, pad_ref, sel_ref, rank_ref):
    f32, i32 = jnp.float32, jnp.int32
    neg = jnp.float32(-jnp.inf)
    n_chunks = SEQ // ROUTE_TC
    erow = lax.broadcasted_iota(i32, (N_EXPERTS, ROUTE_TC), 0)
    grow = lax.broadcasted_iota(i32, (N_GROUPS, ROUTE_TC), 0)
    src_tok = lax.broadcasted_iota(i32, (ROUTE_TC, ROUTE_TC), 0)
    dst_tok = lax.broadcasted_iota(i32, (ROUTE_TC, ROUTE_TC), 1)
    before = (src_tok < dst_tok).astype(jnp.bfloat16)

    def first_index(mask, idx, size):
        return jnp.min(jnp.where(mask, idx, size), axis=0, keepdims=True)

    counts = jnp.zeros((N_EXPERTS, 1), f32)
    for c in range(n_chunks):
        lanes = slice(c * ROUTE_TC, (c + 1) * ROUTE_TC)
        gl = lt_ref[0:N_GROUPS, lanes]
        g_max = jnp.max(gl, axis=0, keepdims=True)
        g_sel = first_index(gl == g_max, grow, N_GROUPS)
        g_weight = 1.0 / jnp.sum(jnp.exp(gl - g_max), axis=0, keepdims=True)
        el = lt_ref[N_GROUPS:N_GROUPS + N_EXPERTS, lanes]
        in_group = (erow >= g_sel * EXPERTS_PER_GROUP) & (erow < (g_sel + 1) * EXPERTS_PER_GROUP)
        el = jnp.where(in_group, el, neg)
        t1 = jnp.max(el, axis=0, keepdims=True)
        e1 = first_index(el == t1, erow, N_EXPERTS)
        el2 = jnp.where(erow == e1, neg, el)
        t2 = jnp.max(el2, axis=0, keepdims=True)
        e2 = first_index(el2 == t2, erow, N_EXPERTS)
        r = jnp.exp(t2 - t1)
        gates_ref[0:1, lanes] = g_weight * (1.0 / (1.0 + r))
        gates_ref[1:2, lanes] = g_weight * (r / (1.0 + r))
        oh1 = erow == e1
        oh2 = erow == e2
        hits = oh1.astype(jnp.bfloat16) + oh2.astype(jnp.bfloat16)
        ahead = jnp.dot(hits, before, preferred_element_type=f32) + counts
        rank_ref[0:1, lanes] = jnp.sum(jnp.where(oh1, ahead, 0.0), axis=0, keepdims=True)
        rank_ref[1:2, lanes] = jnp.sum(jnp.where(oh2, ahead, 0.0), axis=0, keepdims=True)
        sel_ref[0:1, lanes] = e1
        sel_ref[1:2, lanes] = e2
        counts = counts + jnp.sum(hits.astype(f32), axis=1, keepdims=True)

    n_blk = lax.shift_right_logical(counts.astype(i32) + (ROW_BLOCK - 1), ROW_BLOCK.bit_length() - 1)
    e_src = lax.broadcasted_iota(i32, (N_EXPERTS, N_EXPERTS), 1)
    e_dst = lax.broadcasted_iota(i32, (N_EXPERTS, N_EXPERTS), 0)
    incl = (e_src <= e_dst).astype(jnp.bfloat16)
    n_blk_b = jnp.broadcast_to(n_blk.astype(f32), (N_EXPERTS, 128)).astype(jnp.bfloat16)
    blk_end = jnp.dot(incl, n_blk_b, preferred_element_type=f32)[:, 0:1].astype(i32)
    blk_start = blk_end - n_blk
    row_start = (blk_start * ROW_BLOCK).astype(f32)
    e_sub128 = lax.broadcasted_iota(i32, (N_EXPERTS, 128), 0)
    e_lane128 = lax.broadcasted_iota(i32, (N_EXPERTS, 128), 1)
    diag = e_sub128 == e_lane128
    pad_lo = jnp.sum(jnp.where(diag, blk_start * ROW_BLOCK + counts.astype(i32), 0), axis=0, keepdims=True)
    pad_hi = jnp.sum(jnp.where(diag, blk_end * ROW_BLOCK, 0), axis=0, keepdims=True)
    pad_ref[...] = jnp.concatenate([pad_lo, pad_hi, jnp.zeros((6, 128), i32)], axis=0)

    for c in range(n_chunks):
        lanes = slice(c * ROUTE_TC, (c + 1) * ROUTE_TC)
        for k in range(TOP_K):
            oh = erow == sel_ref[k:k + 1, lanes]
            start = jnp.sum(jnp.where(oh, row_start, 0.0), axis=0, keepdims=True)
            pos_ref[k:k + 1, lanes] = (start + rank_ref[k:k + 1, lanes]).astype(i32)

    b_lane = lax.broadcasted_iota(i32, (N_EXPERTS, BLK_LANES), 1)
    e_sub = lax.broadcasted_iota(i32, (N_EXPERTS, BLK_LANES), 0)
    used = n_blk > 0
    block_expert = jnp.minimum(jnp.sum((blk_end <= b_lane).astype(i32), axis=0, keepdims=True), N_EXPERTS - 1)
    first = jnp.sum((used & (blk_start == b_lane)).astype(i32), axis=0, keepdims=True)
    run_index = jnp.sum((used & (e_sub < block_expert)).astype(i32), axis=0, keepdims=True)
    nxt = jnp.min(jnp.where(used & (e_sub > block_expert), e_sub, N_EXPERTS), axis=0, keepdims=True)
    nxt = jnp.where(nxt >= N_EXPERTS, -1, nxt)
    n_used = jnp.broadcast_to(blk_end[N_EXPERTS - 1:N_EXPERTS, :], (1, BLK_LANES))
    blk_ref[...] = jnp.concatenate(
        [block_expert, first, run_index & 1, nxt, n_used, jnp.zeros((3, BLK_LANES), i32)], axis=0)


def _route_call(logits_t):
    return pl.pallas_call(
        _route_kernel,
        out_shape=[jax.ShapeDtypeStruct((TOP_K, SEQ), jnp.float32),
                   jax.ShapeDtypeStruct((TOP_K, SEQ), jnp.int32),
                   jax.ShapeDtypeStruct((8, BLK_LANES), jnp.int32),
                   jax.ShapeDtypeStruct((8, 128), jnp.int32)],
        scratch_shapes=[pltpu.VMEM((TOP_K, SEQ), jnp.int32),
                        pltpu.VMEM((TOP_K, SEQ), jnp.float32)],
        compiler_params=pltpu.CompilerParams(vmem_limit_bytes=VMEM_LIMIT_BYTES),
        name="route",
    )(logits_t)


def kernel(x, w_in, b_in, conv_w, conv_b, conv_norm_g, conv_norm_b, gmlp_norm_g, gmlp_norm_b,
           w_spatial, b_spatial, w_out, b_out, ln1_g, ln1_b, w_router_group, b_router_group,
           w_router_expert, b_router_expert, w_expert_gate, w_expert_up, w_expert_down,
           ln2_g, ln2_b):
    assert x.shape == (1, SEQ, D_MODEL) and w_in.shape[0] == 1
    x2d = x.reshape(SEQ, D_MODEL)
    ya, yg, w_out_b = _mixer_call(
        x2d, w_in[0].astype(jnp.bfloat16), b_in, conv_w[0], conv_b, conv_norm_g, conv_norm_b,
        gmlp_norm_g, gmlp_norm_b, w_spatial[0], b_spatial[0][:, :, None], w_out[0])

    unused_cols = ROUTER_COLS - N_GROUPS - N_EXPERTS
    w_router = jnp.concatenate(
        [w_router_group[0],
         jnp.transpose(w_router_expert[0], (1, 0, 2)).reshape(D_MODEL, N_EXPERTS),
         jnp.zeros((D_MODEL, unused_cols), jnp.float32)], axis=1)
    b_router = jnp.concatenate(
        [b_router_group[0], b_router_expert[0].reshape(-1), jnp.zeros((unused_cols,), jnp.float32)])[None, :]
    x1p, logits_t = _outproj_call(ya, yg, x2d, w_out_b, b_out, ln1_g, ln1_b,
                                    w_router, b_router)

    gates_t, pos, blk, pad = _route_call(logits_t)
    xs = _dispatch_call(pos, pad, x1p)
    ys = _expert_call(blk, xs, w_expert_gate[0], w_expert_up[0], w_expert_down[0])
    out = _combine_call(pos, ys, x1p, gates_t.T, ln2_g, ln2_b)
    return out.reshape(1, SEQ, D_MODEL)
```

```python
import jax
import jax.numpy as jnp
from jax import lax
from jax.experimental import pallas as pl
from jax.experimental.pallas import tpu as pltpu

D_MODEL = 4096
SEQ = 8192
D_CONV = D_MODEL // 2
D_GMLP = D_MODEL // 2
CONV_WIDTH = 31
GROUP_DIM = 128
CHUNK = 128
N_GROUPS = 8
EXPERTS_PER_GROUP = 8
N_EXPERTS = N_GROUPS * EXPERTS_PER_GROUP
TOP_K = 2
D_EXPERT = D_MODEL // 8
ROW_BLOCK = 128
LN_EPS = 1e-5
ALPHA = 2.0 ** 0.25

N_ROWS = SEQ * TOP_K
N_BLOCKS = N_ROWS // ROW_BLOCK + N_EXPERTS
ROUTER_COLS = 128
SLAB = D_MODEL // 2 // 128

VMEM_LIMIT_BYTES = 60 * 1024 * 1024

MIX_BM = 1024
MIX_SUB = 256
MIX_BN = 256
HALO = 32
CONV_ROWS = 128
SHIFT_ROWS = CONV_ROWS + HALO - 8

OUT_BM = 256
OUT_SUB = 128
ROUTE_TC = 512
BLK_LANES = 256
BLK_EXPERT, BLK_FIRST, BLK_SLOT, BLK_NEXT, BLK_USED = range(5)
COMB_BM = 256
DISP_BM = 1024


def _group_norm(v, g, b):
    outs = []
    for s in range(v.shape[1] // GROUP_DIM):
        blk = v[:, s * GROUP_DIM:(s + 1) * GROUP_DIM]
        mu = jnp.mean(blk, axis=-1, keepdims=True)
        d = blk - mu
        var = jnp.mean(d * d, axis=-1, keepdims=True)
        outs.append(d * lax.rsqrt(var + LN_EPS))
    return jnp.concatenate(outs, axis=-1) * g + b


def _mixer_kernel(x_ref, wa_ref, wg_ref, wu_ref, wv_ref, ba_ref, bg_ref, bu_ref, bv_ref,
                  cw_ref, cb_ref, cng_ref, cnb_ref, gng_ref, gnb_ref, wsp_ref, bsp_ref, wo_ref,
                  ya_ref, yg_ref, wob_ref, xs_ref, r_ref, sh_ref):
    i = pl.program_id(1)
    f32 = jnp.float32
    bf16 = jnp.bfloat16
    w_refs = (wa_ref, wg_ref, wu_ref, wv_ref)
    wob_ref[...] = wo_ref[...].astype(bf16)
    off = HALO - (CONV_WIDTH - 1)

    @pl.when(i == 0)
    def _():
        xs_ref[0:HALO, :] = jnp.zeros((HALO, MIX_BN), f32)

    row = lax.broadcasted_iota(jnp.int32, (CHUNK, CHUNK), 0)
    col = lax.broadcasted_iota(jnp.int32, (CHUNK, CHUNK), 1)
    w_sp = [jnp.where(row >= col, wsp_ref[h], 0.0).astype(bf16) for h in range(MIX_BN // GROUP_DIM)]

    def project(k):
        xk = x_ref[k * MIX_SUB:(k + 1) * MIX_SUB, :].astype(bf16)
        for q in range(4):
            r_ref[k % 2, q] = jnp.dot(xk, w_refs[q][...], preferred_element_type=f32)

    def finish_conv(k, c):
        t0 = k * MIX_SUB + c * CONV_ROWS
        rows = slice(c * CONV_ROWS, (c + 1) * CONV_ROWS)
        a = r_ref[k % 2, 0, rows, :] + ba_ref[...]
        gate = r_ref[k % 2, 1, rows, :] + bg_ref[...]
        xs_ref[HALO + t0:HALO + t0 + CONV_ROWS, :] = a * jax.nn.sigmoid(gate)
        sh = sh_ref.at[c % 2]
        for m in range(1, 8):
            sh[m - 1] = xs_ref[t0 + m:t0 + m + SHIFT_ROWS, :]
        acc = jnp.zeros((CONV_ROWS, MIX_BN), f32) + cb_ref[...]
        for k2 in range(CONV_WIDTH):
            m, j8 = (off + k2) % 8, 8 * ((off + k2) // 8)
            if m == 0:
                src = xs_ref[t0 + j8:t0 + j8 + CONV_ROWS, :]
            else:
                src = sh[m - 1, j8:j8 + CONV_ROWS, :]
            acc = acc + cw_ref[k2:k2 + 1, :] * src
        y = jax.nn.silu(_group_norm(acc, cng_ref[...], cnb_ref[...]))
        ya_ref[t0:t0 + CONV_ROWS, :] = y.astype(ya_ref.dtype)

    def finish_gmlp(k):
        base = k * MIX_SUB
        n_chunks = MIX_SUB // CHUNK
        us, vs = [], []
        for c in range(n_chunks):
            rows = slice(c * CHUNK, (c + 1) * CHUNK)
            us.append(jax.nn.gelu(r_ref[k % 2, 2, rows, :] + bu_ref[...]))
            v = jax.nn.gelu(r_ref[k % 2, 3, rows, :] + bv_ref[...])
            vs.append(_group_norm(v, gng_ref[...], gnb_ref[...]).astype(bf16))
        for h in range(MIX_BN // GROUP_DIM):
            lanes = slice(h * GROUP_DIM, (h + 1) * GROUP_DIM)
            v_h = jnp.concatenate([v[:, lanes] for v in vs], axis=1)
            s_h = jnp.dot(w_sp[h], v_h, preferred_element_type=f32) + bsp_ref[h]
            for c in range(n_chunks):
                g = us[c][:, lanes] * s_h[:, c * GROUP_DIM:(c + 1) * GROUP_DIM]
                yg_ref[base + c * CHUNK:base + (c + 1) * CHUNK, lanes] = g.astype(yg_ref.dtype)

    def finish(k):
        for c in range(MIX_SUB // CONV_ROWS):
            finish_conv(k, c)
        finish_gmlp(k)

    n_sub = MIX_BM // MIX_SUB
    for k in range(n_sub + 1):
        if k < n_sub:
            project(k)
        if k > 0:
            finish(k - 1)
    xs_ref[0:HALO, :] = xs_ref[MIX_BM:MIX_BM + HALO, :]


def _mixer_call(x2d, w_in_b, b_in, conv_w, conv_b, cng, cnb, gng, gnb, w_spatial, b_spatial_col, w_out):
    nq = D_CONV // MIX_BN
    ni = SEQ // MIX_BM
    grid = (nq, ni)
    wo_rows = D_MODEL // (nq * ni)
    wo_spec = pl.BlockSpec((wo_rows, D_MODEL), lambda j, i: (j * ni + i, 0))

    def wspec(q):
        return pl.BlockSpec((D_MODEL, MIX_BN), lambda j, i, q=q: (0, q * nq + j))

    def bspec(q):
        return pl.BlockSpec((1, MIX_BN), lambda j, i, q=q: (0, q * nq + j))

    vec = pl.BlockSpec((1, MIX_BN), lambda j, i: (0, j))
    heads_per_step = MIX_BN // GROUP_DIM
    in_specs = [
        pl.BlockSpec((MIX_BM, D_MODEL), lambda j, i: (i, 0)),
        wspec(0), wspec(1), wspec(2), wspec(3),
        bspec(0), bspec(1), bspec(2), bspec(3),
        pl.BlockSpec((CONV_WIDTH, MIX_BN), lambda j, i: (0, j)),
        vec, vec, vec, vec, vec,
        pl.BlockSpec((heads_per_step, CHUNK, CHUNK), lambda j, i: (j, 0, 0)),
        pl.BlockSpec((heads_per_step, CHUNK, 1), lambda j, i: (j, 0, 0)),
        wo_spec,
    ]
    out_spec = pl.BlockSpec((MIX_BM, MIX_BN), lambda j, i: (i, j))
    return pl.pallas_call(
        _mixer_kernel,
        grid=grid,
        in_specs=in_specs,
        out_specs=[out_spec, out_spec, wo_spec],
        out_shape=[jax.ShapeDtypeStruct((SEQ, D_CONV), jnp.bfloat16),
                   jax.ShapeDtypeStruct((SEQ, D_GMLP), jnp.bfloat16),
                   jax.ShapeDtypeStruct((D_MODEL, D_MODEL), jnp.bfloat16)],
        scratch_shapes=[pltpu.VMEM((HALO + MIX_BM, MIX_BN), jnp.float32),
                        pltpu.VMEM((2, 4, MIX_SUB, MIX_BN), jnp.float32),
                        pltpu.VMEM((2, 7, SHIFT_ROWS, MIX_BN), jnp.float32)],
        compiler_params=pltpu.CompilerParams(
            dimension_semantics=("arbitrary", "arbitrary"),
            vmem_limit_bytes=VMEM_LIMIT_BYTES),
        name="mixer",
    )(x2d, w_in_b, w_in_b, w_in_b, w_in_b, b_in, b_in, b_in, b_in,
      conv_w, conv_b, cng, cnb, gng, gnb, w_spatial, b_spatial_col, w_out)


def _store_packed_rows(dst_ref, x):
    _store_packed_rounded_rows(dst_ref, x.astype(jnp.bfloat16).astype(jnp.float32))


def _store_packed_rounded_rows(dst_ref, x):
    half = D_MODEL // 2
    bits = lax.bitcast_convert_type(x, jnp.uint32)
    words = bits[:, :half] | (bits[:, half:] >> 16)
    for j in range(SLAB):
        dst_ref[pl.ds(j, x.shape[0], stride=SLAB), :] = words[:, j * 128:(j + 1) * 128]


def _load_packed_rows(src_ref, rows):
    f32, u32 = jnp.float32, jnp.uint32
    his, los = [], []
    for j in range(SLAB):
        w = src_ref[pl.ds(j, rows, stride=SLAB), :]
        his.append(lax.bitcast_convert_type(w & u32(0xFFFF0000), f32))
        los.append(lax.bitcast_convert_type(w << 16, f32))
    return jnp.concatenate(his + los, axis=1)


def _outproj_kernel(ya_ref, yg_ref, x_ref, w_ref, bo_ref, g_ref, b_ref, wr_ref, br_ref,
                    x1p_ref, logit_ref, z_ref, wrs_ref):
    f32 = jnp.float32

    @pl.when(pl.program_id(0) == 0)
    def _():
        wr = wr_ref[...]
        wr_hi = wr.astype(jnp.bfloat16)
        wrs_ref[:, 0:ROUTER_COLS] = wr_hi
        wrs_ref[:, ROUTER_COLS:] = (wr - wr_hi.astype(f32)).astype(jnp.bfloat16)

    n_sub = OUT_BM // OUT_SUB

    def project(k):
        rows = slice(k * OUT_SUB, (k + 1) * OUT_SUB)
        z = jnp.dot(ya_ref[rows, :], w_ref[0:D_CONV, :], preferred_element_type=f32)
        z_ref[rows, :] = z + jnp.dot(yg_ref[rows, :], w_ref[D_CONV:, :], preferred_element_type=f32)

    def finish(k):
        rows = slice(k * OUT_SUB, (k + 1) * OUT_SUB)
        z = z_ref[rows, :] + bo_ref[...] + ALPHA * x_ref[rows, :]
        mu = jnp.mean(z, axis=-1, keepdims=True)
        d = z - mu
        var = jnp.mean(d * d, axis=-1, keepdims=True)
        x1 = d * lax.rsqrt(var + LN_EPS) * g_ref[...] + b_ref[...]
        x_hi = x1.astype(jnp.bfloat16)
        x1_rounded = x_hi.astype(f32)
        _store_packed_rounded_rows(x1p_ref.at[pl.ds(k * OUT_SUB * SLAB, OUT_SUB * SLAB)], x1_rounded)
        x_lo = (x1 - x1_rounded).astype(jnp.bfloat16)
        parts = jnp.dot(jnp.concatenate([x_hi, x_lo], axis=0), wrs_ref[...], preferred_element_type=f32)
        logits = ((parts[:OUT_SUB, :ROUTER_COLS] + parts[:OUT_SUB, ROUTER_COLS:])
                  + (parts[OUT_SUB:, :ROUTER_COLS] + parts[OUT_SUB:, ROUTER_COLS:]))
        logit_ref[:, rows] = (logits + br_ref[...]).T

    project(0)
    for k in range(n_sub):
        if k + 1 < n_sub:
            project(k + 1)
        finish(k)


def _outproj_call(ya, yg, x2d, w_out_b, b_out, ln_g, ln_b, w_router, b_router):
    grid = (SEQ // OUT_BM,)
    row_half = pl.BlockSpec((OUT_BM, D_CONV), lambda i: (i, 0))
    row_full = pl.BlockSpec((OUT_BM, D_MODEL), lambda i: (i, 0))
    vec = pl.BlockSpec((1, D_MODEL), lambda i: (0, 0))
    in_specs = [
        row_half, row_half, row_full,
        pl.BlockSpec((D_MODEL, D_MODEL), lambda i: (0, 0), pipeline_mode=pl.Buffered(1)),
        vec, vec, vec,
        pl.BlockSpec((D_MODEL, ROUTER_COLS), lambda i: (0, 0), pipeline_mode=pl.Buffered(1)),
        pl.BlockSpec((1, ROUTER_COLS), lambda i: (0, 0)),
    ]
    return pl.pallas_call(
        _outproj_kernel,
        grid=grid,
        in_specs=in_specs,
        out_specs=[pl.BlockSpec((OUT_BM * SLAB, 128), lambda i: (i, 0)),
                   pl.BlockSpec((ROUTER_COLS, OUT_BM), lambda i: (0, i))],
        out_shape=[jax.ShapeDtypeStruct((SEQ * SLAB, 128), jnp.uint32),
                   jax.ShapeDtypeStruct((ROUTER_COLS, SEQ), jnp.float32)],
        scratch_shapes=[pltpu.VMEM((OUT_BM, D_MODEL), jnp.float32),
                        pltpu.VMEM((D_MODEL, 2 * ROUTER_COLS), jnp.bfloat16)],
        compiler_params=pltpu.CompilerParams(
            dimension_semantics=("arbitrary",),
            vmem_limit_bytes=VMEM_LIMIT_BYTES),
        name="outproj",
    )(ya, yg, x2d, w_out_b, b_out, ln_g, ln_b, w_router, b_router)


def _dispatch_kernel(pos_ref, pad_ref, x1p_hbm, xs_hbm, buf, zbuf, lsem, ssem, zsem, tsem):
    i = pl.program_id(0)
    n_steps = pl.num_programs(0)
    rows = DISP_BM * SLAB
    n_slots = 3

    def load(tile, slot):
        src = pl.multiple_of(tile * rows, rows)
        return pltpu.make_async_copy(x1p_hbm.at[pl.ds(src, rows)], buf.at[slot], lsem.at[slot])

    def scatter_wait(slot):
        for _ in range(TOP_K):
            pltpu.make_async_copy(buf.at[slot], xs_hbm.at[pl.ds(0, rows)], ssem.at[slot]).wait()

    def pad_copy(r, n_rows):
        dst = pl.multiple_of(r * SLAB, SLAB)
        return pltpu.make_async_copy(zbuf.at[pl.ds(0, n_rows * SLAB)],
                                     xs_hbm.at[pl.ds(dst, n_rows * SLAB)], zsem)

    def tail_copy(blk):
        dst = pl.multiple_of(blk * (ROW_BLOCK * SLAB), ROW_BLOCK * SLAB)
        return pltpu.make_async_copy(zbuf, xs_hbm.at[pl.ds(dst, ROW_BLOCK * SLAB)], tsem)

    def for_each_tail_block(fn):
        used_blocks = lax.shift_right_logical(pad_ref[1, N_EXPERTS - 1], ROW_BLOCK.bit_length() - 1)

        def per_block(blk, carry):
            fn(blk)
            return carry
        lax.fori_loop(used_blocks, N_BLOCKS, per_block, 0)

    def for_each_pad_piece(fn):
        def per_expert(e, carry):
            lo = pad_ref[0, e]
            n_pad = pad_ref[1, e] - lo
            for bit in range(ROW_BLOCK.bit_length() - 1):
                size = 1 << bit

                @pl.when((n_pad & size) != 0)
                def _():
                    fn(lo + (n_pad & (size - 1)), size)
            return carry
        lax.fori_loop(0, N_EXPERTS, per_expert, 0)

    @pl.when(i == 0)
    def _():
        load(0, 0).start()
        zbuf[...] = jnp.zeros(zbuf.shape, zbuf.dtype)
        for_each_pad_piece(lambda r, n_rows: pad_copy(r, n_rows).start())
        for_each_tail_block(lambda blk: tail_copy(blk).start())

    slot = lax.rem(i, n_slots)

    @pl.when(i + 1 < n_steps)
    def _():
        load(i + 1, lax.rem(i + 1, n_slots)).start()

    load(i, slot).wait()

    def issue(r, carry):
        src = pl.multiple_of(r * SLAB, SLAB)
        for k in range(TOP_K):
            dst = pl.multiple_of(pos_ref[k, i * DISP_BM + r] * SLAB, SLAB)
            pltpu.make_async_copy(buf.at[slot, pl.ds(src, SLAB)], xs_hbm.at[pl.ds(dst, SLAB)],
                                  ssem.at[slot]).start(priority=k % 2)
        return carry
    lax.fori_loop(0, DISP_BM, issue, 0, unroll=4)

    @pl.when(i >= 1)
    def _():
        scatter_wait(lax.rem(i + n_slots - 1, n_slots))

    @pl.when(i == n_steps - 1)
    def _():
        scatter_wait(slot)
        for_each_pad_piece(lambda r, n_rows: pad_copy(r, n_rows).wait())
        for_each_tail_block(lambda blk: tail_copy(blk).wait())


def _dispatch_call(pos, pad, x1p):
    any_spec = pl.BlockSpec(memory_space=pl.ANY)
    grid_spec = pltpu.PrefetchScalarGridSpec(
        num_scalar_prefetch=2,
        grid=(SEQ // DISP_BM,),
        in_specs=[any_spec],
        out_specs=any_spec,
        scratch_shapes=[pltpu.VMEM((3, DISP_BM * SLAB, 128), jnp.uint32),
                        pltpu.VMEM((ROW_BLOCK * SLAB, 128), jnp.uint32),
                        pltpu.SemaphoreType.DMA((3,)),
                        pltpu.SemaphoreType.DMA((3,)),
                        pltpu.SemaphoreType.DMA(()),
                        pltpu.SemaphoreType.DMA(())],
    )
    return pl.pallas_call(
        _dispatch_kernel,
        grid_spec=grid_spec,
        out_shape=jax.ShapeDtypeStruct((N_BLOCKS * ROW_BLOCK * SLAB, 128), jnp.uint32),
        compiler_params=pltpu.CompilerParams(
            dimension_semantics=("arbitrary",),
            vmem_limit_bytes=VMEM_LIMIT_BYTES),
        name="dispatch",
    )(pos, pad, x1p)


def _expert_kernel(blk_ref, x_ref, wg_hbm, wu_hbm, wd_hbm, out_ref, wgb, wub, wdb, wsem):
    b = pl.program_id(0)
    n_used = blk_ref[BLK_USED, 0]

    def weight_copies(e, s):
        return (pltpu.make_async_copy(wg_hbm.at[e], wgb.at[s], wsem.at[s, 0]),
                pltpu.make_async_copy(wu_hbm.at[e], wub.at[s], wsem.at[s, 1]),
                pltpu.make_async_copy(wd_hbm.at[e], wdb.at[s], wsem.at[s, 2]))

    @pl.when(b == 0)
    def _():
        for c in weight_copies(blk_ref[BLK_EXPERT, 0], 0):
            c.start(priority=1)

    @pl.when(b < n_used)
    def _():
        ws = blk_ref[BLK_SLOT, b]

        @pl.when(blk_ref[BLK_FIRST, b] == 1)
        def _():
            @pl.when(blk_ref[BLK_NEXT, b] >= 0)
            def _():
                for c in weight_copies(blk_ref[BLK_NEXT, b], 1 - ws):
                    c.start(priority=1)
            for c in weight_copies(blk_ref[BLK_EXPERT, b], ws):
                c.wait()

        xb = _load_packed_rows(x_ref, ROW_BLOCK).astype(jnp.bfloat16)
        g = jnp.dot(xb, wgb[ws], preferred_element_type=jnp.float32)
        u = jnp.dot(xb, wub[ws], preferred_element_type=jnp.float32)
        h = jax.nn.silu(g) * u
        _store_packed_rows(out_ref, jnp.dot(h, wdb[ws], preferred_element_type=jnp.float32))

    @pl.when(b >= n_used)
    def _():
        out_ref[...] = jnp.zeros(out_ref.shape, out_ref.dtype)


def _expert_call(blk, xs, w_gate, w_up, w_down):
    any_spec = pl.BlockSpec(memory_space=pl.ANY)
    block = (ROW_BLOCK * SLAB, 128)
    grid_spec = pltpu.PrefetchScalarGridSpec(
        num_scalar_prefetch=1,
        grid=(N_BLOCKS,),
        in_specs=[pl.BlockSpec(block, lambda b, blk: (jnp.minimum(b, blk[BLK_USED, 0] - 1), 0)),
                  any_spec, any_spec, any_spec],
        out_specs=pl.BlockSpec(block, lambda b, blk: (b, 0)),
        scratch_shapes=[pltpu.VMEM((2, D_MODEL, D_EXPERT), jnp.float32),
                        pltpu.VMEM((2, D_MODEL, D_EXPERT), jnp.float32),
                        pltpu.VMEM((2, D_EXPERT, D_MODEL), jnp.float32),
                        pltpu.SemaphoreType.DMA((2, 3))],
    )
    return pl.pallas_call(
        _expert_kernel,
        grid_spec=grid_spec,
        out_shape=jax.ShapeDtypeStruct((N_BLOCKS * ROW_BLOCK * SLAB, 128), jnp.uint32),
        compiler_params=pltpu.CompilerParams(
            dimension_semantics=("arbitrary",),
            vmem_limit_bytes=VMEM_LIMIT_BYTES),
        name="experts",
    )(blk, xs, w_gate, w_up, w_down)


def _combine_kernel(pos_ref, ys_hbm, x1p_ref, gate_ref, g_ref, b_ref, out_ref, ybuf, sem):
    i = pl.program_id(0)
    n_steps = pl.num_programs(0)
    rows = TOP_K * COMB_BM

    def gather_start(tile, s):
        def issue(r, carry):
            for k in range(TOP_K):
                src = pl.multiple_of(pos_ref[k, tile * COMB_BM + r] * SLAB, SLAB)
                dst = pl.multiple_of((k * COMB_BM + r) * SLAB, SLAB)
                pltpu.make_async_copy(ys_hbm.at[pl.ds(src, SLAB)], ybuf.at[s, pl.ds(dst, SLAB)],
                                      sem.at[s]).start(priority=k % 2)
            return carry
        lax.fori_loop(0, COMB_BM, issue, 0, unroll=4)

    @pl.when(i == 0)
    def _():
        gather_start(0, 0)

    s = lax.rem(i, 2)

    @pl.when(i + 1 < n_steps)
    def _():
        gather_start(i + 1, 1 - s)

    pltpu.make_async_copy(ys_hbm.at[pl.ds(0, rows * SLAB)], ybuf.at[s], sem.at[s]).wait()
    gate = gate_ref[...]
    y0 = _load_packed_rows(ybuf.at[s, pl.ds(0, COMB_BM * SLAB)], COMB_BM)
    y1 = _load_packed_rows(ybuf.at[s, pl.ds(COMB_BM * SLAB, COMB_BM * SLAB)], COMB_BM)
    ffn = y0 * gate[:, 0:1] + y1 * gate[:, 1:2]
    z = ALPHA * _load_packed_rows(x1p_ref, COMB_BM) + ffn
    mu = jnp.mean(z, axis=-1, keepdims=True)
    d = z - mu
    var = jnp.mean(d * d, axis=-1, keepdims=True)
    out_ref[...] = d * lax.rsqrt(var + LN_EPS) * g_ref[...] + b_ref[...]


def _combine_call(pos, ys, x1p, gates, ln_g, ln_b):
    row_full = pl.BlockSpec((COMB_BM, D_MODEL), lambda i, pos: (i, 0))
    vec = pl.BlockSpec((1, D_MODEL), lambda i, pos: (0, 0))
    grid_spec = pltpu.PrefetchScalarGridSpec(
        num_scalar_prefetch=1,
        grid=(SEQ // COMB_BM,),
        in_specs=[
            pl.BlockSpec(memory_space=pl.ANY),
            pl.BlockSpec((COMB_BM * SLAB, 128), lambda i, pos: (i, 0)),
            pl.BlockSpec((COMB_BM, TOP_K), lambda i, pos: (i, 0)),
            vec, vec,
        ],
        out_specs=row_full,
        scratch_shapes=[pltpu.VMEM((2, TOP_K * COMB_BM * SLAB, 128), jnp.uint32),
                        pltpu.SemaphoreType.DMA((2,))],
    )
    return pl.pallas_call(
        _combine_kernel,
        grid_spec=grid_spec,
        out_shape=jax.ShapeDtypeStruct((SEQ, D_MODEL), jnp.float32),
        compiler_params=pltpu.CompilerParams(
            dimension_semantics=("arbitrary",),
            vmem_limit_bytes=VMEM_LIMIT_BYTES),
        name="combine",
    )(pos, ys, x1p, gates, ln_g, ln_b)


def _route_kernel(lt_ref, gates_ref, pos_ref, blk_ref, pad_ref, sel_ref, rank_ref):
    f32, i32 = jnp.float32, jnp.int32
    neg = jnp.float32(-jnp.inf)
    n_chunks = SEQ // ROUTE_TC
    erow = lax.broadcasted_iota(i32, (N_EXPERTS, ROUTE_TC), 0)
    grow = lax.broadcasted_iota(i32, (N_GROUPS, ROUTE_TC), 0)
    src_tok = lax.broadcasted_iota(i32, (ROUTE_TC, ROUTE_TC), 0)
    dst_tok = lax.broadcasted_iota(i32, (ROUTE_TC, ROUTE_TC), 1)
    before = (src_tok < dst_tok).astype(jnp.bfloat16)

    def first_index(mask, idx, size):
        return jnp.min(jnp.where(mask, idx, size), axis=0, keepdims=True)

    counts = jnp.zeros((N_EXPERTS, 1), f32)
    for c in range(n_chunks):
        lanes = slice(c * ROUTE_TC, (c + 1) * ROUTE_TC)
        gl = lt_ref[0:N_GROUPS, lanes]
        g_max = jnp.max(gl, axis=0, keepdims=True)
        g_sel = first_index(gl == g_max, grow, N_GROUPS)
        g_weight = 1.0 / jnp.sum(jnp.exp(gl - g_max), axis=0, keepdims=True)
        el = lt_ref[N_GROUPS:N_GROUPS + N_EXPERTS, lanes]
        in_group = (erow >= g_sel * EXPERTS_PER_GROUP) & (erow < (g_sel + 1) * EXPERTS_PER_GROUP)
        el = jnp.where(in_group, el, neg)
        t1 = jnp.max(el, axis=0, keepdims=True)
        e1 = first_index(el == t1, erow, N_EXPERTS)
        el2 = jnp.where(erow == e1, neg, el)
        t2 = jnp.max(el2, axis=0, keepdims=True)
        e2 = first_index(el2 == t2, erow, N_EXPERTS)
        r = jnp.exp(t2 - t1)
        gates_ref[0:1, lanes] = g_weight * (1.0 / (1.0 + r))
        gates_ref[1:2, lanes] = g_weight * (r / (1.0 + r))
        oh1 = erow == e1
        oh2 = erow == e2
        hits = oh1.astype(jnp.bfloat16) + oh2.astype(jnp.bfloat16)
        ahead = jnp.dot(hits, before, preferred_element_type=f32) + counts
        rank_ref[0:1, lanes] = jnp.sum(jnp.where(oh1, ahead, 0.0), axis=0, keepdims=True)
        rank_ref[1:2, lanes] = jnp.sum(jnp.where(oh2, ahead, 0.0), axis=0, keepdims=True)
        sel_ref[0:1, lanes] = e1
        sel_ref[1:2, lanes] = e2
        counts = counts + jnp.sum(hits.astype(f32), axis=1, keepdims=True)

    n_blk = lax.shift_right_logical(counts.astype(i32) + (ROW_BLOCK - 1), ROW_BLOCK.bit_length() - 1)
    e_src = lax.broadcasted_iota(i32, (N_EXPERTS, N_EXPERTS), 1)
    e_dst = lax.broadcasted_iota(i32, (N_EXPERTS, N_EXPERTS), 0)
    incl = (e_src <= e_dst).astype(jnp.bfloat16)
    n_blk_b = jnp.broadcast_to(n_blk.astype(f32), (N_EXPERTS, 128)).astype(jnp.bfloat16)
    blk_end = jnp.dot(incl, n_blk_b, preferred_element_type=f32)[:, 0:1].astype(i32)
    blk_start = blk_end - n_blk
    row_start = (blk_start * ROW_BLOCK).astype(f32)
    e_sub128 = lax.broadcasted_iota(i32, (N_EXPERTS, 128), 0)
    e_lane128 = lax.broadcasted_iota(i32, (N_EXPERTS, 128), 1)
    diag = e_sub128 == e_lane128
    pad_lo = jnp.sum(jnp.where(diag, blk_start * ROW_BLOCK + counts.astype(i32), 0), axis=0, keepdims=True)
    pad_hi = jnp.sum(jnp.where(diag, blk_end * ROW_BLOCK, 0), axis=0, keepdims=True)
    pad_ref[...] = jnp.concatenate([pad_lo, pad_hi, jnp.zeros((6, 128), i32)], axis=0)

    for c in range(n_chunks):
        lanes = slice(c * ROUTE_TC, (c + 1) * ROUTE_TC)
        for k in range(TOP_K):
            oh = erow == sel_ref[k:k + 1, lanes]
            start = jnp.sum(jnp.where(oh, row_start, 0.0), axis=0, keepdims=True)
            pos_ref[k:k + 1, lanes] = (start + rank_ref[k:k + 1, lanes]).astype(i32)

    b_lane = lax.broadcasted_iota(i32, (N_EXPERTS, BLK_LANES), 1)
    e_sub = lax.broadcasted_iota(i32, (N_EXPERTS, BLK_LANES), 0)
    used = n_blk > 0
    block_expert = jnp.minimum(jnp.sum((blk_end <= b_lane).astype(i32), axis=0, keepdims=True), N_EXPERTS - 1)
    first = jnp.sum((used & (blk_start == b_lane)).astype(i32), axis=0, keepdims=True)
    run_index = jnp.sum((used & (e_sub < block_expert)).astype(i32), axis=0, keepdims=True)
    nxt = jnp.min(jnp.where(used & (e_sub > block_expert), e_sub, N_EXPERTS), axis=0, keepdims=True)
    nxt = jnp.where(nxt >= N_EXPERTS, -1, nxt)
    n_used = jnp.broadcast_to(blk_end[N_EXPERTS - 1:N_EXPERTS, :], (1, BLK_LANES))
    blk_ref[...] = jnp.concatenate(
        [block_expert, first, run_index & 1, nxt, n_used, jnp.zeros((3, BLK_LANES), i32)], axis=0)


def _route_call(logits_t):
    return pl.pallas_call(
        _route_kernel,
        out_shape=[jax.ShapeDtypeStruct((TOP_K, SEQ), jnp.float32),
                   jax.ShapeDtypeStruct((TOP_K, SEQ), jnp.int32),
                   jax.ShapeDtypeStruct((8, BLK_LANES), jnp.int32),
                   jax.ShapeDtypeStruct((8, 128), jnp.int32)],
        scratch_shapes=[pltpu.VMEM((TOP_K, SEQ), jnp.int32),
                        pltpu.VMEM((TOP_K, SEQ), jnp.float32)],
        compiler_params=pltpu.CompilerParams(vmem_limit_bytes=VMEM_LIMIT_BYTES),
        name="route",
    )(logits_t)


def kernel(x, w_in, b_in, conv_w, conv_b, conv_norm_g, conv_norm_b, gmlp_norm_g, gmlp_norm_b,
           w_spatial, b_spatial, w_out, b_out, ln1_g, ln1_b, w_router_group, b_router_group,
           w_router_expert, b_router_expert, w_expert_gate, w_expert_up, w_expert_down,
           ln2_g, ln2_b):
    assert x.shape == (1, SEQ, D_MODEL) and w_in.shape[0] == 1
    x2d = x.reshape(SEQ, D_MODEL)
    ya, yg, w_out_b = _mixer_call(
        x2d, w_in[0].astype(jnp.bfloat16), b_in, conv_w[0], conv_b, conv_norm_g, conv_norm_b,
        gmlp_norm_g, gmlp_norm_b, w_spatial[0], b_spatial[0][:, :, None], w_out[0])

    unused_cols = ROUTER_COLS - N_GROUPS - N_EXPERTS
    w_router = jnp.concatenate(
        [w_router_group[0],
         jnp.transpose(w_router_expert[0], (1, 0, 2)).reshape(D_MODEL, N_EXPERTS),
         jnp.zeros((D_MODEL, unused_cols), jnp.float32)], axis=1)
    b_router = jnp.concatenate(
        [b_router_group[0], b_router_expert[0].reshape(-1), jnp.zeros((unused_cols,), jnp.float32)])[None, :]
    x1p, logits_t = _outproj_call(ya, yg, x2d, w_out_b, b_out, ln1_g, ln1_b,
                                    w_router, b_router)

    gates_t, pos, blk, pad = _route_call(logits_t)
    xs = _dispatch_call(pos, pad, x1p)
    ys = _expert_call(blk, xs, w_expert_gate[0], w_expert_up[0], w_expert_down[0])
    out = _combine_call(pos, ys, x1p, gates_t.T, ln2_g, ln2_b)
    return out.reshape(1, SEQ, D_MODEL)
```
